```python
import jax, jax.numpy as jnp
from jax import lax
import numpy as np

D_MODEL = 1024
BATCH = 2
SEQ = 8192
DEPTH = 2

GRID_W = 64
CTX_LEN = 256
N_MIXERS = 2
NORM_EPS = 1e-6

MLSTM_HEADS = 8
MLSTM_DV = D_MODEL // MLSTM_HEADS
MLSTM_DQK = MLSTM_DV // 2
MLSTM_QK_W = MLSTM_HEADS * MLSTM_DQK
MLSTM_V_W = MLSTM_HEADS * MLSTM_DV
MLSTM_O_W = MLSTM_V_W
MLSTM_GATE_W = 4 * MLSTM_HEADS
MLSTM_IN_W = 2 * MLSTM_QK_W + MLSTM_V_W + MLSTM_O_W + MLSTM_GATE_W
MLSTM_SPLITS = [MLSTM_QK_W, 2 * MLSTM_QK_W, 2 * MLSTM_QK_W + MLSTM_V_W, 2 * MLSTM_QK_W + MLSTM_V_W + MLSTM_O_W]
MLSTM_CHUNK = 64
GATE_SOFTCAP = 15.0

POOL_WINDOWS = (2, 4, 8, 16)
POOL_GROUPS = len(POOL_WINDOWS)
POOL_GW = D_MODEL // POOL_GROUPS

D_FF = 4 * D_MODEL

kernel_name = "hybrid_mlstm_pool_flow_backbone"


def rmsnorm(x, g):
    xf = x.astype(jnp.float32)
    y = xf * lax.rsqrt(jnp.mean(xf * xf, axis=-1, keepdims=True) + NORM_EPS)
    return (y * g.astype(jnp.float32)).astype(x.dtype)


def mlp(u, w1, w2):
    h = jnp.square(jax.nn.relu(u @ w1))
    return h @ w2


def mlstm_project(u, w_in, gate_b):
    bsz, L = u.shape[0], u.shape[1]
    p = u @ w_in
    q, k, v, o, g = jnp.split(p, MLSTM_SPLITS, axis=-1)

    def heads(t, d):
        return t.reshape(bsz, L, MLSTM_HEADS, d).transpose(0, 2, 1, 3).astype(jnp.float32)

    q = heads(q, MLSTM_DQK) * (MLSTM_DQK ** -0.5)
    k = heads(k, MLSTM_DQK)
    v = heads(v, MLSTM_DV)
    g = g.reshape(bsz, L, 4, MLSTM_HEADS).astype(jnp.float32) + gate_b.astype(jnp.float32)
    g = GATE_SOFTCAP * jnp.tanh(g / GATE_SOFTCAP)
    g = g.transpose(2, 0, 3, 1)
    log_i = g[0::2]
    log_f = jax.nn.log_sigmoid(g[1::2])
    return q, k, v, o, log_i, log_f


def mlstm_zero_state(bsz):
    return (jnp.zeros((bsz, MLSTM_HEADS, MLSTM_DV, MLSTM_DQK), jnp.float32),
            jnp.zeros((bsz, MLSTM_HEADS, MLSTM_DQK), jnp.float32),
            jnp.zeros((bsz, MLSTM_HEADS), jnp.float32))


def mlstm_state_update(state, k, v, log_i, log_f):
    C, n, m = state
    b = jnp.cumsum(log_f, axis=-1)
    b_end = b[..., -1]
    g = b_end[..., None] - b + log_i
    m_new = jnp.maximum(b_end + m, jnp.max(g, axis=-1))
    a = jnp.exp(b_end + m - m_new)
    w = jnp.exp(g - m_new[..., None])
    C = a[..., None, None] * C + jnp.einsum('bhsv,bhsd->bhvd', w[..., None] * v, k)
    n = a[..., None] * n + jnp.einsum('bhs,bhsd->bhd', w, k)
    return (C, n, m_new)


def mlstm_chunk(state, chunk):
    q, k, v, log_i, log_f = chunk
    C, n, m = state
    L = q.shape[-2]
    b = jnp.cumsum(log_f, axis=-1)
    order = jnp.tril(jnp.ones((L, L), dtype=bool))
    dmat = jnp.where(order, b[..., :, None] - b[..., None, :] + log_i[..., None, :], -jnp.inf)
    inter = b + m[..., None]
    m_t = jnp.maximum(inter, jnp.max(dmat, axis=-1))
    w_inter = jnp.exp(inter - m_t)
    s = jnp.einsum('bhtd,bhsd->bhts', q, k) * jnp.exp(dmat - m_t[..., None])
    num = w_inter[..., None] * jnp.einsum('bhvd,bhtd->bhtv', C, q) + jnp.einsum('bhts,bhsv->bhtv', s, v)
    den = w_inter * jnp.einsum('bhd,bhtd->bht', n, q) + jnp.sum(s, axis=-1)
    h = num / jnp.maximum(jnp.abs(den), jnp.exp(-m_t))[..., None]
    return mlstm_state_update(state, k, v, log_i, log_f), h


def mlstm_scan(state, q, k, v, log_i, log_f):
    bsz, nh, L = q.shape[0], q.shape[1], q.shape[2]
    nc = L // MLSTM_CHUNK

    def to_chunks(t):
        return jnp.moveaxis(t.reshape((bsz, nh, nc, MLSTM_CHUNK) + t.shape[3:]), 2, 0)

    final, h = lax.scan(mlstm_chunk, state, (to_chunks(q), to_chunks(k), to_chunks(v), to_chunks(log_i), to_chunks(log_f)))
    return jnp.moveaxis(h, 0, 2).reshape(bsz, nh, L, MLSTM_DV), final


def mlstm_output(h, o, norm_g, w_out):
    bsz, nh, L, dv = h.shape
    h = h * lax.rsqrt(jnp.mean(h * h, axis=-1, keepdims=True) + NORM_EPS)
    h = h.transpose(0, 2, 1, 3).reshape(bsz, L, nh * dv) * norm_g.astype(jnp.float32)
    y = (jax.nn.sigmoid(o.astype(jnp.float32)) * h).astype(o.dtype)
    return y @ w_out


def mlstm_mixer(xn, cn, w_in, gate_b, norm_g, w_out, ctx_out):
    qx, kx, vx, ox, lix, lfx = mlstm_project(xn, w_in, gate_b)
    qc, kc, vc, oc, lic, lfc = mlstm_project(cn, w_in, gate_b)
    hx = 0.0
    hc = 0.0
    for d in range(2):
        if d == 0:
            order = lambda t: t
        else:
            order = lambda t: jnp.flip(t, axis=2)
        zero = mlstm_zero_state(kc.shape[0])
        c_k, c_v, c_li, c_lf = order(kc), order(vc), order(lic[d]), order(lfc[d])
        if ctx_out:
            h_c, state = mlstm_scan(zero, order(qc), c_k, c_v, c_li, c_lf)
            hc = hc + order(h_c)
        else:
            state = mlstm_state_update(zero, c_k, c_v, c_li, c_lf)
        h_x, _ = mlstm_scan(state, order(qx), order(kx), order(vx), order(lix[d]), order(lfx[d]))
        hx = hx + order(h_x)
    y = mlstm_output(hx, ox, norm_g, w_out)
    y_ctx = mlstm_output(hc, oc, norm_g, w_out) if ctx_out else None
    return y, y_ctx


def centred_mean(u, window):
    L = u.shape[-2]
    cs = jnp.cumsum(u, axis=-2)
    cs = jnp.concatenate([jnp.zeros_like(cs[..., :1, :]), cs], axis=-2)
    t = jnp.arange(L)
    lo = jnp.clip(t - window // 2, 0, L)
    hi = jnp.clip(t - window // 2 + window, 0, L)
    total = jnp.take(cs, hi, axis=-2) - jnp.take(cs, lo, axis=-2)
    return total / (hi - lo).astype(u.dtype)[:, None]


def pool_mixer(u, w, scale):
    uf = u.astype(jnp.float32)
    pooled = jnp.stack([centred_mean(uf[..., gi * POOL_GW:(gi + 1) * POOL_GW], win)
                        for gi, win in enumerate(POOL_WINDOWS)], axis=-2)
    p = pooled - uf.reshape(uf.shape[:-1] + (POOL_GROUPS, POOL_GW))
    y = jnp.einsum('...gc,gcd->...gd', p.astype(u.dtype), w)
    return y.reshape(u.shape) * scale


def setup_inputs(seed: int = 0) -> dict:
    key = jax.random.key(seed)
    ks = jax.random.split(key, 18)
    D = D_MODEL
    n_a = (DEPTH + 1) // N_MIXERS
    n_b = DEPTH // N_MIXERS

    def nrm(k, shape, s):
        return jax.random.normal(k, shape, jnp.float32) * s

    f_bias = jnp.linspace(3.0, 6.0, MLSTM_HEADS, dtype=jnp.float32)
    zeros_h = jnp.zeros((MLSTM_HEADS,), jnp.float32)
    gate_base = jnp.stack([zeros_h, f_bias, zeros_h, f_bias])
    return {
        "x": nrm(ks[0], (BATCH, SEQ, D), 1.0),
        "c": nrm(ks[1], (BATCH, D), 1.0),
        "ctx": nrm(ks[2], (BATCH, CTX_LEN, D), 1.0),
        "c_ctx": nrm(ks[3], (D,), 1.0),
        "ada_w": nrm(ks[4], (DEPTH, D, 6 * D), 0.5 * D ** -0.5),
        "ada_b": nrm(ks[5], (DEPTH, 6 * D), 0.02),
        "norm1_g": 1.0 + nrm(ks[6], (DEPTH, D), 0.02),
        "norm2_g": 1.0 + nrm(ks[7], (DEPTH, D), 0.02),
        "mlstm_w_in": nrm(ks[8], (n_a, D, MLSTM_IN_W), D ** -0.5),
        "mlstm_gate_b": gate_base + nrm(ks[9], (n_a, 4, MLSTM_HEADS), 0.1),
        "mlstm_norm_g": 1.0 + nrm(ks[10], (n_a, MLSTM_V_W), 0.02),
        "mlstm_w_out": nrm(ks[11], (n_a, MLSTM_V_W, D), MLSTM_V_W ** -0.5),
        "pool_w": nrm(ks[12], (n_b, POOL_GROUPS, POOL_GW, POOL_GW), POOL_GW ** -0.5),
        "pool_scale": 1.0 + nrm(ks[13], (n_b, D), 0.02),
        "mlp_w1": nrm(ks[14], (DEPTH, D, D_FF), D ** -0.5),
        "mlp_w2": nrm(ks[15], (DEPTH, D_FF, D), D_FF ** -0.5),
        "final_g": 1.0 + nrm(ks[16], (D,), 0.02),
    }


def reference(x, c, ctx, c_ctx, ada_w, ada_b, norm1_g, norm2_g, mlstm_w_in, mlstm_gate_b,
              mlstm_norm_g, mlstm_w_out, pool_w, pool_scale, mlp_w1, mlp_w2, final_g):
    bsz, seq, dm = x.shape
    rows = seq // GRID_W
    silu_c = jax.nn.silu(c)
    silu_cc = jax.nn.silu(c_ctx)
    for i in range(DEPTH):
        kind = i % N_MIXERS
        j = i // N_MIXERS
        ctx_next = any(l % N_MIXERS == 0 for l in range(i + 1, DEPTH))
        mod = (silu_c @ ada_w[i] + ada_b[i])[:, None, :]
        sh1, sc1, g1, sh2, sc2, g2 = jnp.split(mod, 6, axis=-1)
        xn = rmsnorm(x, norm1_g[i]) * (1 + sc1) + sh1
        if kind == 0 or ctx_next:
            cmod = silu_cc @ ada_w[i] + ada_b[i]
            csh1, csc1, cg1, csh2, csc2, cg2 = jnp.split(cmod, 6)
            cn = rmsnorm(ctx, norm1_g[i]) * (1 + csc1) + csh1
        if kind == 0:
            y, y_ctx = mlstm_mixer(xn, cn, mlstm_w_in[j], mlstm_gate_b[j], mlstm_norm_g[j], mlstm_w_out[j], ctx_next)
        else:
            y = pool_mixer(xn.reshape(bsz, rows, GRID_W, dm), pool_w[j], pool_scale[j]).reshape(bsz, seq, dm)
            y_ctx = pool_mixer(cn, pool_w[j], pool_scale[j]) if ctx_next else None
        x = x + g1 * y
        x = x + g2 * mlp(rmsnorm(x, norm2_g[i]) * (1 + sc2) + sh2, mlp_w1[i], mlp_w2[i])
        if ctx_next:
            ctx = ctx + cg1 * y_ctx
            ctx = ctx + cg2 * mlp(rmsnorm(ctx, norm2_g[i]) * (1 + csc2) + csh2, mlp_w1[i], mlp_w2[i])
    return rmsnorm(x, final_g)
```

```python
import functools

import jax
import jax.numpy as jnp
from jax import lax
from jax.experimental import pallas as pl
from jax.experimental.pallas import tpu as pltpu

F32 = jnp.float32
BF16 = jnp.bfloat16

D_MODEL = 1024
HEADS = 8
DV = 128
DQK = 64
QK_W = HEADS * DQK
V_W = HEADS * DV
GATE_W = 4 * HEADS
D_FF = 4 * D_MODEL
NORM_EPS = 1e-6
GATE_SOFTCAP = 15.0
GRID_W = 64
POOL_WINDOWS = (2, 4, 8, 16)
POOL_GW = D_MODEL // len(POOL_WINDOWS)

LANES = 128
CHUNK = LANES
MOD_ROWS = 8
VMEM_LIMIT = 56 * 1024 * 1024
ROW_TILE = 512
FF_TILE = 1024
NEG_BIG = -1e30


def _params(n_axes):
    return pltpu.CompilerParams(
        dimension_semantics=("arbitrary",) * n_axes, vmem_limit_bytes=VMEM_LIMIT)


def _resident(shape):
    return pl.BlockSpec(shape, lambda *_: (0,) * len(shape), pipeline_mode=pl.Buffered(1))


def _rmsnorm(x, g):
    return x * lax.rsqrt(jnp.mean(x * x, axis=-1, keepdims=True) + NORM_EPS) * g


def _dot(a, b):
    return jnp.dot(a, b, preferred_element_type=F32)


def _dot_f32(a, b):
    return jnp.dot(a, b, precision=lax.Precision.HIGHEST, preferred_element_type=F32)


def _mod_kernel(c_ref, w_ref, b_ref, o_ref):
    c = c_ref[...]
    s = c * jax.nn.sigmoid(c)
    o_ref[0] = _dot_f32(s, w_ref[0]) + b_ref[0]


def _modulation(cmat, ada_w, ada_b):
    depth, _, n = ada_w.shape
    tn = 1536
    return pl.pallas_call(
        _mod_kernel,
        grid=(depth, n // tn),
        in_specs=[
            pl.BlockSpec((MOD_ROWS, D_MODEL), lambda i, j: (0, 0)),
            pl.BlockSpec((1, D_MODEL, tn), lambda i, j: (i, 0, j)),
            pl.BlockSpec((1, 1, tn), lambda i, j: (i, 0, j)),
        ],
        out_specs=pl.BlockSpec((1, MOD_ROWS, tn), lambda i, j: (i, 0, j)),
        out_shape=jax.ShapeDtypeStruct((depth, MOD_ROWS, n), F32),
        compiler_params=_params(2),
        name="adaln_modulation",
    )(cmat, ada_w, ada_b.reshape(depth, 1, n))


def _front_kernel(with_qo, x_ref, g_ref, sh_ref, sc_ref, wq_ref, wkt_ref, wv_ref, wo_ref,
                  wg_ref, wgt_ref, *outs):
    xn = _rmsnorm(x_ref[...], g_ref[...]) * (1.0 + sc_ref[0]) + sh_ref[0]
    xb = xn.astype(BF16)
    nt = (((1,), (1,)), ((), ()))
    if with_qo:
        q_ref, kt_ref, v_ref, o_ref, gc_ref, gr_ref = outs
        q_ref[...] = (_dot(xb, wq_ref[...]) * (DQK ** -0.5)).astype(BF16)
        o_ref[...] = _dot(xb, wo_ref[...])
    else:
        kt_ref, v_ref, gc_ref, gr_ref = outs
    kt_ref[...] = lax.dot_general(wkt_ref[...], xb, nt, preferred_element_type=F32).astype(BF16)
    v_ref[...] = _dot(xb, wv_ref[...]).astype(BF16)
    gc_ref[...] = _dot(xb, wg_ref[...])
    gr_ref[...] = lax.dot_general(wgt_ref[...], xb, nt, preferred_element_type=F32)


def _front(x2d, norm_g, sh, sc, weights, rows_per_mod, with_qo):
    n = x2d.shape[0]
    tm = min(ROW_TILE, n)
    tiles_per_mod = rows_per_mod // tm
    wq, wkt, wv, wo, wg, wgt = weights
    row = lambda i: (i, 0)
    mod = lambda i: (i // tiles_per_mod, 0, 0)
    out_shape, out_specs = [], []
    if with_qo:
        out_shape.append(jax.ShapeDtypeStruct((n, QK_W), BF16))
        out_specs.append(pl.BlockSpec((tm, QK_W), row))
    out_shape.append(jax.ShapeDtypeStruct((QK_W, n), BF16))
    out_specs.append(pl.BlockSpec((QK_W, tm), lambda i: (0, i)))
    out_shape.append(jax.ShapeDtypeStruct((n, V_W), BF16))
    out_specs.append(pl.BlockSpec((tm, V_W), row))
    if with_qo:
        out_shape.append(jax.ShapeDtypeStruct((n, V_W), F32))
        out_specs.append(pl.BlockSpec((tm, V_W), row))
    out_shape.append(jax.ShapeDtypeStruct((n, LANES), F32))
    out_specs.append(pl.BlockSpec((tm, LANES), row))
    out_shape.append(jax.ShapeDtypeStruct((GATE_W, n), F32))
    out_specs.append(pl.BlockSpec((GATE_W, tm), lambda i: (0, i)))
    return pl.pallas_call(
        functools.partial(_front_kernel, with_qo),
        grid=(n // tm,),
        in_specs=[
            pl.BlockSpec((tm, D_MODEL), row),
            _resident((1, D_MODEL)),
            pl.BlockSpec((1, 1, D_MODEL), mod),
            pl.BlockSpec((1, 1, D_MODEL), mod),
            _resident(wq.shape), _resident(wkt.shape), _resident(wv.shape),
            _resident(wo.shape), _resident(wg.shape), _resident(wgt.shape),
        ],
        out_specs=out_specs,
        out_shape=out_shape,
        compiler_params=_params(1),
        name="mlstm_front_qo" if with_qo else "mlstm_front_ctx",
    )(x2d, norm_g.reshape(1, D_MODEL), sh, sc, wq, wkt, wv, wo, wg, wgt)


def _softcap(g):
    return GATE_SOFTCAP * jnp.tanh(g * (1.0 / GATE_SOFTCAP))


def _log_sigmoid(g):
    return jnp.minimum(g, 0.0) - jnp.log1p(jnp.exp(-jnp.abs(g)))


def _gate_kernel(gc_ref, gr_ref, bc_ref, br_ref, col_ref, row_ref):
    t = gc_ref.shape[0]
    r = lax.broadcasted_iota(jnp.int32, (CHUNK, CHUNK), 0)
    c = lax.broadcasted_iota(jnp.int32, (CHUNK, CHUNK), 1)
    lower = (c <= r).astype(F32)
    upper = (c >= r).astype(F32)
    ones = jnp.ones((CHUNK, CHUNK), F32)
    row_rhs = jnp.concatenate([upper, lower, ones], axis=1)
    fwd_cols = (c >= HEADS) & (c < 2 * HEADS)
    bwd_cols = (c >= 3 * HEADS) & (c < 4 * HEADS)

    gr = _softcap(gr_ref[...] + br_ref[...])
    li_f, li_b = gr[0:HEADS], gr[2 * HEADS:3 * HEADS]
    lf = _log_sigmoid(jnp.concatenate([gr[HEADS:2 * HEADS], gr[3 * HEADS:4 * HEADS]], axis=0))
    lfc = _log_sigmoid(_softcap(gc_ref[...] + bc_ref[...]))

    for k in range(t // CHUNK):
        sl = slice(k * CHUNK, (k + 1) * CHUNK)
        x = lfc[sl]
        col_ref[sl, :] = jnp.where(fwd_cols, _dot_f32(lower, x),
                                   jnp.where(bwd_cols, _dot_f32(upper, x), 0.0))
        y = _dot_f32(lf[:, sl], row_rhs)
        cum_f, tot_f = y[0:HEADS, 0:CHUNK], y[0:HEADS, 2 * CHUNK:]
        cum_b, tot_b = y[HEADS:, CHUNK:2 * CHUNK], y[HEADS:, 2 * CHUNK:]
        e_f = li_f[:, sl] - cum_f
        e_b = li_b[:, sl] - cum_b
        row_ref[0, :, sl] = e_f
        row_ref[1, :, sl] = e_b
        row_ref[2, :, sl] = jnp.exp(tot_f + e_f)
        row_ref[3, :, sl] = jnp.exp(tot_b + e_b)
        row_ref[4, :, sl] = jnp.exp(tot_f)
        row_ref[5, :, sl] = jnp.exp(tot_b)


def _gate_prep(gcol_raw, grow_raw, bias_col, bias_row):
    n = gcol_raw.shape[0]
    t = min(1024, n)
    return pl.pallas_call(
        _gate_kernel,
        grid=(n // t,),
        in_specs=[
            pl.BlockSpec((t, LANES), lambda i: (i, 0)),
            pl.BlockSpec((GATE_W, t), lambda i: (0, i)),
            _resident((1, LANES)),
            _resident((GATE_W, 1)),
        ],
        out_specs=[
            pl.BlockSpec((t, LANES), lambda i: (i, 0)),
            pl.BlockSpec((6, HEADS, t), lambda i: (0, 0, i)),
        ],
        out_shape=[
            jax.ShapeDtypeStruct((n, LANES), F32),
            jax.ShapeDtypeStruct((6, HEADS, n), F32),
        ],
        compiler_params=_params(1),
        name="mlstm_gate_prep",
    )(gcol_raw, grow_raw, bias_col, bias_row)


def _scan_kernel(q_ref, kt_ref, v_ref, gcol_ref, grow_ref, ktc_ref, vc_ref, growc_ref,
                 h_ref, cst_ref):
    seq = q_ref.shape[0]
    nc = seq // CHUNK
    ncc = ktc_ref.shape[1] // CHUNK
    r = lax.broadcasted_iota(jnp.int32, (CHUNK, CHUNK), 0)
    c = lax.broadcasted_iota(jnp.int32, (CHUNK, CHUNK), 1)
    masks = (r >= c, r <= c)
    top = r < DQK
    left = c < DQK
    ones_blk = jnp.ones((CHUNK, DV), BF16)
    zeros_k = jnp.zeros((DQK, CHUNK), BF16)
    zeros_c = jnp.zeros((DQK, 2 * DV), BF16)

    cst_ref[...] = jnp.zeros(cst_ref.shape, F32)

    def state_increment(kt_j, v_j, w_row):
        ktw = (kt_j.astype(F32) * w_row).astype(BF16)
        return ktw, jnp.concatenate([v_j, ones_blk], axis=1)

    def decay2(a_row):
        return jnp.concatenate([a_row, a_row], axis=1)

    for d in range(2):
        for step in range(ncc):
            k = step if d == 0 else ncc - 1 - step
            sl = slice(k * CHUNK, (k + 1) * CHUNK)
            ktp = ktc_ref[:, sl]
            vp = vc_ref[sl, :]
            for j in range(2):
                ktw, rhs = state_increment(ktp[j * DQK:(j + 1) * DQK], vp[:, j * DV:(j + 1) * DV],
                                           growc_ref[2 + d, 0, j:j + 1, sl])
                cst_ref[d, j] = cst_ref[d, j] * decay2(growc_ref[4 + d, 0, j:j + 1, sl]) + _dot(ktw, rhs)

    def step(i, accumulate):
        for d in range(2):
            k = i if d == 0 else nc - 1 - i
            off = pl.multiple_of(k * CHUNK, CHUNK)
            rows = pl.ds(off, CHUNK)
            qp = q_ref[rows, :]
            ktp = kt_ref[:, rows]
            vp = v_ref[rows, :]
            bcol = gcol_ref[0, rows, :]
            kbd = jnp.concatenate([jnp.where(top, ktp, jnp.zeros_like(ktp)),
                                   jnp.where(top, jnp.zeros_like(ktp), ktp)], axis=1)
            s2 = _dot(qp, kbd)
            bb = [jnp.broadcast_to(bcol[:, 2 * d + j:2 * d + j + 1], (CHUNK, CHUNK)) for j in range(2)]
            qs = (qp.astype(F32) * jnp.exp(jnp.where(left, bb[0], bb[1]))).astype(BF16)
            for j in range(2):
                e_row = grow_ref[d, 0, j:j + 1, rows]
                dm = jnp.exp(jnp.where(masks[d], bb[j] + e_row, NEG_BIG))
                p = (s2[:, j * CHUNK:(j + 1) * CHUNK] * dm).astype(BF16)
                ktw, v1 = state_increment(ktp[j * DQK:(j + 1) * DQK], vp[:, j * DV:(j + 1) * DV],
                                          grow_ref[2 + d, 0, j:j + 1, rows])
                lhs = jnp.concatenate([jnp.concatenate([p, qs], axis=1),
                                       jnp.concatenate([ktw, zeros_k], axis=1)], axis=0)
                cst = cst_ref[d, j]
                cb = cst.astype(BF16)
                cpad = jnp.concatenate([cb, zeros_c] if j == 0 else [zeros_c, cb], axis=0)
                out = _dot(lhs, jnp.concatenate([v1, cpad], axis=0))
                num, den = out[:CHUNK, :DV], out[:CHUNK, DV:]
                h = num / jnp.maximum(jnp.abs(den), 1.0)
                cols = slice(j * DV, (j + 1) * DV)
                if accumulate:
                    h_ref[rows, cols] += h
                else:
                    h_ref[rows, cols] = h
                cst_ref[d, j] = cst * decay2(grow_ref[4 + d, 0, j:j + 1, rows]) + out[CHUNK:, :]

    def first(i, carry):
        step(i, False)
        return carry

    def second(i, carry):
        step(i, True)
        return carry

    lax.fori_loop(0, nc // 2, first, 0)
    lax.fori_loop(nc // 2, nc, second, 0)


def _scan(q, kt, v, gcol, grow, ktc, vc, growc, bsz, seq, ctx_len):
    pairs = HEADS // 2
    return pl.pallas_call(
        _scan_kernel,
        grid=(bsz, pairs),
        in_specs=[
            pl.BlockSpec((seq, 2 * DQK), lambda b, p: (b, p)),
            pl.BlockSpec((2 * DQK, seq), lambda b, p: (p, b)),
            pl.BlockSpec((seq, 2 * DV), lambda b, p: (b, p)),
            pl.BlockSpec((1, seq, 4), lambda b, p: (p, b, 0)),
            pl.BlockSpec((6, 1, 2, seq), lambda b, p: (0, p, 0, b)),
            pl.BlockSpec((2 * DQK, ctx_len), lambda b, p: (p, b)),
            pl.BlockSpec((ctx_len, 2 * DV), lambda b, p: (b, p)),
            pl.BlockSpec((6, 1, 2, ctx_len), lambda b, p: (0, p, 0, b)),
        ],
        out_specs=pl.BlockSpec((seq, 2 * DV), lambda b, p: (b, p)),
        out_shape=jax.ShapeDtypeStruct((bsz * seq, V_W), F32),
        scratch_shapes=[pltpu.VMEM((2, 2, DQK, 2 * DV), F32)],
        compiler_params=_params(2),
        name="mlstm_scan",
    )(q, kt, v, gcol, grow, ktc, vc, growc)


def _mlp_tail(x1, n2_ref, sh2_ref, sc2_ref, g2_ref, w1_ref, w2_ref):
    u = (_rmsnorm(x1, n2_ref[...]) * (1.0 + sc2_ref[0]) + sh2_ref[0]).astype(BF16)
    acc = jnp.zeros(x1.shape, F32)
    for k in range(D_FF // FF_TILE):
        hid = jnp.maximum(_dot(u, w1_ref[:, k * FF_TILE:(k + 1) * FF_TILE]), 0.0)
        acc = acc + _dot((hid * hid).astype(BF16), w2_ref[k * FF_TILE:(k + 1) * FF_TILE, :])
    return x1 + g2_ref[0] * acc


def _mlstm_post_kernel(h_ref, o_ref, x_ref, ng_ref, wout_ref, g1_ref, n2_ref, sh2_ref, sc2_ref,
                       g2_ref, w1_ref, w2_ref, out_ref):
    gate = jax.nn.sigmoid(o_ref[...])
    parts = []
    for k in range(HEADS):
        hk = h_ref[:, k * DV:(k + 1) * DV]
        parts.append(hk * lax.rsqrt(jnp.mean(hk * hk, axis=-1, keepdims=True) + NORM_EPS))
    hn = jnp.concatenate(parts, axis=1) * ng_ref[...]
    y = (gate * hn).astype(BF16)
    x1 = x_ref[...] + g1_ref[0] * _dot(y, wout_ref[...])
    out_ref[...] = _mlp_tail(x1, n2_ref, sh2_ref, sc2_ref, g2_ref, w1_ref, w2_ref)


def _mlstm_post(h, o, x2d, norm_g, w_out, g1, n2, sh2, sc2, g2, w1, w2, rows_per_mod):
    n = x2d.shape[0]
    tm = ROW_TILE
    tiles_per_mod = rows_per_mod // tm
    row = pl.BlockSpec((tm, D_MODEL), lambda i: (i, 0))
    mod = pl.BlockSpec((1, 1, D_MODEL), lambda i: (i // tiles_per_mod, 0, 0))
    return pl.pallas_call(
        _mlstm_post_kernel,
        grid=(n // tm,),
        in_specs=[row, row, row, _resident((1, V_W)), _resident(w_out.shape), mod,
                  _resident((1, D_MODEL)), mod, mod, mod, _resident(w1.shape), _resident(w2.shape)],
        out_specs=row,
        out_shape=jax.ShapeDtypeStruct((n, D_MODEL), F32),
        compiler_params=_params(1),
        name="mlstm_post_mlp",
    )(h, o, x2d, norm_g.reshape(1, V_W), w_out, g1, n2.reshape(1, D_MODEL), sh2, sc2, g2, w1, w2)


def _shift_rows(a, k, pos):
    n = a.shape[0]
    rolled = pltpu.roll(a, k % n, axis=0)
    valid = (pos >= k) if k > 0 else (pos < GRID_W + k)
    return jnp.where(valid, rolled, 0.0)


def _window_sum(u, window, pos):
    half = window // 2
    back, fwd, width = u, u, 1
    while width < half:
        back = back + _shift_rows(back, width, pos)
        fwd = fwd + _shift_rows(fwd, -width, pos)
        width *= 2
    return _shift_rows(back, 1, pos) + fwd


def _pool_layer_kernel(x_ref, n1_ref, sh1_ref, sc1_ref, g1_ref, pw_ref, ps_ref, n2_ref, sh2_ref,
                       sc2_ref, g2_ref, w1_ref, w2_ref, fg_ref, out_ref):
    x = x_ref[...]
    tm = x.shape[0]
    xn = _rmsnorm(x, n1_ref[...]) * (1.0 + sc1_ref[0]) + sh1_ref[0]
    pos = lax.broadcasted_iota(jnp.int32, (tm, POOL_GW), 0) % GRID_W
    ys = []
    for gi, window in enumerate(POOL_WINDOWS):
        u = xn[:, gi * POOL_GW:(gi + 1) * POOL_GW]
        lo = jnp.maximum(pos - window // 2, 0)
        hi = jnp.minimum(pos - window // 2 + window, GRID_W)
        pooled = _window_sum(u, window, pos) * (1.0 / (hi - lo).astype(F32))
        ys.append(_dot((pooled - u).astype(BF16), pw_ref[gi]))
    y = jnp.concatenate(ys, axis=1) * ps_ref[...]
    x1 = x + g1_ref[0] * y
    x2 = _mlp_tail(x1, n2_ref, sh2_ref, sc2_ref, g2_ref, w1_ref, w2_ref)
    out_ref[...] = _rmsnorm(x2, fg_ref[...])


def _pool_layer(x2d, n1, sh1, sc1, g1, pool_w, pool_scale, n2, sh2, sc2, g2, w1, w2, final_g,
                rows_per_mod):
    n = x2d.shape[0]
    tm = ROW_TILE
    tiles_per_mod = rows_per_mod // tm
    row = pl.BlockSpec((tm, D_MODEL), lambda i: (i, 0))
    mod = pl.BlockSpec((1, 1, D_MODEL), lambda i: (i // tiles_per_mod, 0, 0))
    vec = _resident((1, D_MODEL))
    return pl.pallas_call(
        _pool_layer_kernel,
        grid=(n // tm,),
        in_specs=[row, vec, mod, mod, mod, _resident(pool_w.shape), vec, vec, mod, mod, mod,
                  _resident(w1.shape), _resident(w2.shape), vec],
        out_specs=row,
        out_shape=jax.ShapeDtypeStruct((n, D_MODEL), F32),
        compiler_params=_params(1),
        name="pool_mlp_final",
    )(x2d, n1.reshape(1, D_MODEL), sh1, sc1, g1, pool_w, pool_scale.reshape(1, D_MODEL),
      n2.reshape(1, D_MODEL), sh2, sc2, g2, w1, w2, final_g.reshape(1, D_MODEL))


def kernel(x, c, ctx, c_ctx, ada_w, ada_b, norm1_g, norm2_g, mlstm_w_in, mlstm_gate_b,
           mlstm_norm_g, mlstm_w_out, pool_w, pool_scale, mlp_w1, mlp_w2, final_g):
    bsz, seq, dm = x.shape
    ctx_len = ctx.shape[1]
    assert dm == D_MODEL and seq % (2 * CHUNK) == 0 and ctx_len % CHUNK == 0
    assert seq % ROW_TILE == 0 and ROW_TILE % GRID_W == 0 and bsz + 1 <= MOD_ROWS
    assert ada_w.shape[0] == 2 and mlstm_w_in.shape[0] == 1 and pool_w.shape[0] == 1

    cmat = jnp.concatenate(
        [c, c_ctx[None, :], jnp.zeros((MOD_ROWS - bsz - 1, dm), F32)], axis=0)
    mod = _modulation(cmat, ada_w, ada_b)

    def mod_rows(layer, which, rows):
        return mod[layer, rows, which * dm:(which + 1) * dm][:, None, :]

    lat = slice(0, bsz)
    cx = slice(bsz, bsz + 1)

    w_in = mlstm_w_in[0]
    o_start = 2 * QK_W + V_W
    g_start = o_start + V_W
    wg = w_in[:, g_start:]
    weights = (
        w_in[:, :QK_W].astype(BF16),
        w_in[:, QK_W:2 * QK_W].T.astype(BF16),
        w_in[:, 2 * QK_W:o_start].astype(BF16),
        w_in[:, o_start:g_start].astype(BF16),
        jnp.pad(wg, ((0, 0), (0, LANES - GATE_W))).astype(BF16),
        wg.T.astype(BF16),
    )
    gate_b = mlstm_gate_b[0].reshape(GATE_W)
    bias_col = jnp.pad(gate_b, (0, LANES - GATE_W)).reshape(1, LANES)
    bias_row = gate_b.reshape(GATE_W, 1)

    x2d = x.reshape(bsz * seq, dm)
    q, kt, v, o, graw_c, graw_r = _front(
        x2d, norm1_g[0], mod_rows(0, 0, lat), mod_rows(0, 1, lat), weights, seq, True)
    ktc, vc, crawc, crawr = _front(
        ctx.reshape(bsz * ctx_len, dm), norm1_g[0], mod_rows(0, 0, cx), mod_rows(0, 1, cx),
        weights, bsz * ctx_len, False)
    gcol, grow = _gate_prep(graw_c, graw_r, bias_col, bias_row)
    _, growc = _gate_prep(crawc, crawr, bias_col, bias_row)

    pairs = HEADS // 2
    gcol = gcol[:, :GATE_W].reshape(bsz * seq, 4, pairs, 2)[:, 1::2]
    gcol = gcol.transpose(2, 0, 1, 3).reshape(pairs, bsz * seq, 4)
    grow = grow.reshape(6, pairs, 2, bsz * seq)
    growc = growc.reshape(6, pairs, 2, bsz * ctx_len)

    h = _scan(q, kt, v, gcol, grow, ktc, vc, growc, bsz, seq, ctx_len)
    x2d = _mlstm_post(
        h, o, x2d, mlstm_norm_g[0], mlstm_w_out[0].astype(BF16), mod_rows(0, 2, lat),
        norm2_g[0], mod_rows(0, 3, lat), mod_rows(0, 4, lat), mod_rows(0, 5, lat),
        mlp_w1[0].astype(BF16), mlp_w2[0].astype(BF16), seq)

    out = _pool_layer(
        x2d, norm1_g[1], mod_rows(1, 0, lat), mod_rows(1, 1, lat), mod_rows(1, 2, lat),
        pool_w[0].astype(BF16), pool_scale[0], norm2_g[1], mod_rows(1, 3, lat),
        mod_rows(1, 4, lat), mod_rows(1, 5, lat), mlp_w1[1].astype(BF16),
        mlp_w2[1].astype(BF16), final_g, seq)
    return out.reshape(bsz, seq, dm)
```

```python
import functools

import jax
import jax.numpy as jnp
from jax import lax
from jax.experimental import pallas as pl
from jax.experimental.pallas import tpu as pltpu

F32 = jnp.float32
BF16 = jnp.bfloat16

D_MODEL = 1024
HEADS = 8
DV = 128
DQK = 64
QK_W = HEADS * DQK
V_W = HEADS * DV
GATE_W = 4 * HEADS
D_FF = 4 * D_MODEL
NORM_EPS = 1e-6
GATE_SOFTCAP = 15.0
GRID_W = 64
POOL_WINDOWS = (2, 4, 8, 16)
POOL_GW = D_MODEL // len(POOL_WINDOWS)

LANES = 128
CHUNK = LANES
MOD_ROWS = 8
VMEM_LIMIT = 56 * 1024 * 1024
ROW_TILE = 512
FF_TILE = 1024
NEG_BIG = -1e30


def _params(n_axes):
    return pltpu.CompilerParams(
        dimension_semantics=("arbitrary",) * n_axes, vmem_limit_bytes=VMEM_LIMIT)


def _resident(shape):
    return pl.BlockSpec(shape, lambda *_: (0,) * len(shape), pipeline_mode=pl.Buffered(1))


def _rmsnorm(x, g):
    return x * lax.rsqrt(jnp.mean(x * x, axis=-1, keepdims=True) + NORM_EPS) * g


def _dot(a, b):
    return jnp.dot(a, b, preferred_element_type=F32)


def _dot_f32(a, b):
    return jnp.dot(a, b, precision=lax.Precision.HIGHEST, preferred_element_type=F32)


def _mod_kernel(c_ref, w_ref, b_ref, o_ref):
    c = c_ref[...]
    s = c * jax.nn.sigmoid(c)
    o_ref[0] = _dot_f32(s, w_ref[0]) + b_ref[0]


def _modulation(cmat, ada_w, ada_b):
    depth, _, n = ada_w.shape
    tn = 1536
    return pl.pallas_call(
        _mod_kernel,
        grid=(depth, n // tn),
        in_specs=[
            pl.BlockSpec((MOD_ROWS, D_MODEL), lambda i, j: (0, 0)),
            pl.BlockSpec((1, D_MODEL, tn), lambda i, j: (i, 0, j)),
            pl.BlockSpec((1, 1, tn), lambda i, j: (i, 0, j)),
        ],
        out_specs=pl.BlockSpec((1, MOD_ROWS, tn), lambda i, j: (i, 0, j)),
        out_shape=jax.ShapeDtypeStruct((depth, MOD_ROWS, n), F32),
        compiler_params=_params(2),
        name="adaln_modulation",
    )(cmat, ada_w, ada_b.reshape(depth, 1, n))


def _front_kernel(with_qo, x_ref, g_ref, sh_ref, sc_ref, wq_ref, wkt_ref, wv_ref, wo_ref,
                  wg_ref, wgt_ref, *outs):
    xn = _rmsnorm(x_ref[...], g_ref[...]) * (1.0 + sc_ref[0]) + sh_ref[0]
    xb = xn.astype(BF16)
    nt = (((1,), (1,)), ((), ()))
    if with_qo:
        q_ref, kt_ref, v_ref, o_ref, gc_ref, gr_ref = outs
        q_ref[...] = (_dot(xb, wq_ref[...]) * (DQK ** -0.5)).astype(BF16)
        o_ref[...] = _dot(xb, wo_ref[...])
    else:
        kt_ref, v_ref, gc_ref, gr_ref = outs
    kt_ref[...] = lax.dot_general(wkt_ref[...], xb, nt, preferred_element_type=F32).astype(BF16)
    v_ref[...] = _dot(xb, wv_ref[...]).astype(BF16)
    gc_ref[...] = _dot(xb, wg_ref[...])
    gr_ref[...] = lax.dot_general(wgt_ref[...], xb, nt, preferred_element_type=F32)


def _front(x2d, norm_g, sh, sc, weights, rows_per_mod, with_qo):
    n = x2d.shape[0]
    tm = min(ROW_TILE, n)
    tiles_per_mod = rows_per_mod // tm
    wq, wkt, wv, wo, wg, wgt = weights
    row = lambda i: (i, 0)
    mod = lambda i: (i // tiles_per_mod, 0, 0)
    out_shape, out_specs = [], []
    if with_qo:
        out_shape.append(jax.ShapeDtypeStruct((n, QK_W), BF16))
        out_specs.append(pl.BlockSpec((tm, QK_W), row))
    out_shape.append(jax.ShapeDtypeStruct((QK_W, n), BF16))
    out_specs.append(pl.BlockSpec((QK_W, tm), lambda i: (0, i)))
    out_shape.append(jax.ShapeDtypeStruct((n, V_W), BF16))
    out_specs.append(pl.BlockSpec((tm, V_W), row))
    if with_qo:
        out_shape.append(jax.ShapeDtypeStruct((n, V_W), F32))
        out_specs.append(pl.BlockSpec((tm, V_W), row))
    out_shape.append(jax.ShapeDtypeStruct((n, LANES), F32))
    out_specs.append(pl.BlockSpec((tm, LANES), row))
    out_shape.append(jax.ShapeDtypeStruct((GATE_W, n), F32))
    out_specs.append(pl.BlockSpec((GATE_W, tm), lambda i: (0, i)))
    return pl.pallas_call(
        functools.partial(_front_kernel, with_qo),
        grid=(n // tm,),
        in_specs=[
            pl.BlockSpec((tm, D_MODEL), row),
            _resident((1, D_MODEL)),
            pl.BlockSpec((1, 1, D_MODEL), mod),
            pl.BlockSpec((1, 1, D_MODEL), mod),
            _resident(wq.shape), _resident(wkt.shape), _resident(wv.shape),
            _resident(wo.shape), _resident(wg.shape), _resident(wgt.shape),
        ],
        out_specs=out_specs,
        out_shape=out_shape,
        compiler_params=_params(1),
        name="mlstm_front_qo" if with_qo else "mlstm_front_ctx",
    )(x2d, norm_g.reshape(1, D_MODEL), sh, sc, wq, wkt, wv, wo, wg, wgt)


def _softcap(g):
    return GATE_SOFTCAP * jnp.tanh(g * (1.0 / GATE_SOFTCAP))


def _log_sigmoid(g):
    return jnp.minimum(g, 0.0) - jnp.log1p(jnp.exp(-jnp.abs(g)))


def _gate_kernel(gc_ref, gr_ref, bc_ref, br_ref, col_ref, row_ref):
    t = gc_ref.shape[0]
    r = lax.broadcasted_iota(jnp.int32, (CHUNK, CHUNK), 0)
    c = lax.broadcasted_iota(jnp.int32, (CHUNK, CHUNK), 1)
    lower = (c <= r).astype(F32)
    upper = (c >= r).astype(F32)
    ones = jnp.ones((CHUNK, CHUNK), F32)
    row_rhs = jnp.concatenate([upper, lower, ones], axis=1)
    fwd_cols = (c >= HEADS) & (c < 2 * HEADS)
    bwd_cols = (c >= 3 * HEADS) & (c < 4 * HEADS)

    gr = _softcap(gr_ref[...] + br_ref[...])
    li_f, li_b = gr[0:HEADS], gr[2 * HEADS:3 * HEADS]
    lf = _log_sigmoid(jnp.concatenate([gr[HEADS:2 * HEADS], gr[3 * HEADS:4 * HEADS]], axis=0))
    lfc = _log_sigmoid(_softcap(gc_ref[...] + bc_ref[...]))

    for k in range(t // CHUNK):
        sl = slice(k * CHUNK, (k + 1) * CHUNK)
        x = lfc[sl]
        cum = jnp.where(fwd_cols, _dot_f32(lower, x),
                        jnp.where(bwd_cols, _dot_f32(upper, x), 0.0))
        hi = cum.astype(BF16)
        rest = cum - hi.astype(F32)
        mid = rest.astype(BF16)
        col_ref[0, sl, :] = hi
        col_ref[1, sl, :] = mid
        col_ref[2, sl, :] = (rest - mid.astype(F32)).astype(BF16)
        y = _dot_f32(lf[:, sl], row_rhs)
        cum_f, tot_f = y[0:HEADS, 0:CHUNK], y[0:HEADS, 2 * CHUNK:]
        cum_b, tot_b = y[HEADS:, CHUNK:2 * CHUNK], y[HEADS:, 2 * CHUNK:]
        e_f = li_f[:, sl] - cum_f
        e_b = li_b[:, sl] - cum_b
        row_ref[0, :, sl] = e_f
        row_ref[1, :, sl] = e_b
        row_ref[2, :, sl] = jnp.exp(tot_f + e_f)
        row_ref[3, :, sl] = jnp.exp(tot_b + e_b)
        row_ref[4, :, sl] = jnp.exp(tot_f)
        row_ref[5, :, sl] = jnp.exp(tot_b)


def _gate_prep(gcol_raw, grow_raw, bias_col, bias_row):
    n = gcol_raw.shape[0]
    t = min(1024, n)
    return pl.pallas_call(
        _gate_kernel,
        grid=(n // t,),
        in_specs=[
            pl.BlockSpec((t, LANES), lambda i: (i, 0)),
            pl.BlockSpec((GATE_W, t), lambda i: (0, i)),
            _resident((1, LANES)),
            _resident((GATE_W, 1)),
        ],
        out_specs=[
            pl.BlockSpec((3, t, LANES), lambda i: (0, i, 0)),
            pl.BlockSpec((6, HEADS, t), lambda i: (0, 0, i)),
        ],
        out_shape=[
            jax.ShapeDtypeStruct((3, n, LANES), BF16),
            jax.ShapeDtypeStruct((6, HEADS, n), F32),
        ],
        compiler_params=_params(1),
        name="mlstm_gate_prep",
    )(gcol_raw, grow_raw, bias_col, bias_row)


SCAN_SPLITS = 2
STATE_UNROLL = 4
OUTPUT_UNROLL = 4
BCOL_W = 16


def _scan_kernel(q_ref, kt_ref, v_ref, gcol_ref, grow_ref, ktc_ref, vc_ref, growc_ref,
                 h_ref, cst_ref, cs_ref):
    seq = q_ref.shape[0]
    nc = seq // CHUNK
    ncc = ktc_ref.shape[1] // CHUNK
    nc_part = nc // SCAN_SPLITS
    part = pl.program_id(2)
    r = lax.broadcasted_iota(jnp.int32, (CHUNK, CHUNK), 0)
    c = lax.broadcasted_iota(jnp.int32, (CHUNK, CHUNK), 1)
    masks = (r >= c, r <= c)
    top = r < DQK
    left = c < DQK
    ones_blk = jnp.ones((CHUNK, DV), BF16)
    piece = lax.broadcasted_iota(jnp.int32, (BCOL_W, 4 * CHUNK), 0)
    block = lax.broadcasted_iota(jnp.int32, (BCOL_W, 4 * CHUNK), 1) // CHUNK
    sel = ((piece % 4 == block) & (piece < 12)).astype(BF16)

    def state_increment(kt_j, v_j, w_row):
        ktw = (kt_j.astype(F32) * w_row).astype(BF16)
        return _dot(ktw, jnp.concatenate([v_j, ones_blk], axis=1))

    def decay2(a_row):
        return jnp.concatenate([a_row, a_row], axis=1)

    @pl.when(part == 0)
    def _():
        cst_ref[...] = jnp.zeros(cst_ref.shape, F32)
        for d in range(2):
            for step in range(ncc):
                k = step if d == 0 else ncc - 1 - step
                sl = slice(k * CHUNK, (k + 1) * CHUNK)
                ktp = ktc_ref[:, sl]
                vp = vc_ref[sl, :]
                for j in range(2):
                    inc = state_increment(ktp[j * DQK:(j + 1) * DQK], vp[:, j * DV:(j + 1) * DV],
                                          growc_ref[2 + d, 0, j:j + 1, sl])
                    cst_ref[d, j] = cst_ref[d, j] * decay2(growc_ref[4 + d, 0, j:j + 1, sl]) + inc

        def state_step(i, carry):
            for d in range(2):
                k = i if d == 0 else nc - 1 - i
                rows = pl.ds(pl.multiple_of(k * CHUNK, CHUNK), CHUNK)
                ktp = kt_ref[:, rows]
                vp = v_ref[rows, :]
                for j in range(2):
                    cst = cst_ref[d, j]
                    cs_ref[d, k, j * DQK:(j + 1) * DQK, :] = cst.astype(BF16)
                    inc = state_increment(ktp[j * DQK:(j + 1) * DQK], vp[:, j * DV:(j + 1) * DV],
                                          grow_ref[2 + d, 0, j:j + 1, rows])
                    cst_ref[d, j] = cst * decay2(grow_ref[4 + d, 0, j:j + 1, rows]) + inc
            return carry

        lax.fori_loop(0, nc, state_step, 0, unroll=STATE_UNROLL)

    def decay_stage(k):
        rows = pl.ds(pl.multiple_of(k * CHUNK, CHUNK), CHUNK)
        qp = q_ref[rows, :]
        ktp = kt_ref[:, rows]
        bb_all = _dot(gcol_ref[0, rows, :], sel)
        kbd = jnp.concatenate([jnp.where(top, ktp, jnp.zeros_like(ktp)),
                               jnp.where(top, jnp.zeros_like(ktp), ktp)], axis=1)
        s2 = _dot(qp, kbd)
        qf = qp.astype(F32)
        lhs = []
        for d in range(2):
            bb = [bb_all[:, (2 * d + j) * CHUNK:(2 * d + j + 1) * CHUNK] for j in range(2)]
            qs = (qf * jnp.exp(jnp.where(left, bb[0], bb[1]))).astype(BF16)
            for j in range(2):
                e_row = grow_ref[d, 0, j:j + 1, rows]
                dm = jnp.exp(jnp.where(masks[d], bb[j] + e_row, NEG_BIG))
                p = (s2[:, j * CHUNK:(j + 1) * CHUNK] * dm).astype(BF16)
                qs_j = jnp.where(left if j == 0 else ~left, qs, jnp.zeros_like(qs))
                lhs.append(jnp.concatenate([p, qs_j], axis=1))
        return lhs

    def value_stage(k, i, lhs):
        rows = pl.ds(pl.multiple_of(k * CHUNK, CHUNK), CHUNK)
        out_rows = pl.ds(pl.multiple_of(i * CHUNK, CHUNK), CHUNK)
        vp = v_ref[rows, :]
        for j in range(2):
            v1 = jnp.concatenate([vp[:, j * DV:(j + 1) * DV], ones_blk], axis=1)
            h = None
            for d in range(2):
                out = _dot(lhs[2 * d + j], jnp.concatenate([v1, cs_ref[d, k]], axis=0))
                hd = out[:, :DV] / jnp.maximum(jnp.abs(out[:, DV:]), 1.0)
                h = hd if d == 0 else h + hd
            h_ref[out_rows, j * DV:(j + 1) * DV] = h

    def output_trip(t, carry):
        i0 = t * OUTPUT_UNROLL
        staged = [decay_stage(part * nc_part + i0 + g) for g in range(OUTPUT_UNROLL)]
        for g in range(OUTPUT_UNROLL):
            value_stage(part * nc_part + i0 + g, i0 + g, staged[g])
        return carry

    lax.fori_loop(0, nc_part // OUTPUT_UNROLL, output_trip, 0)


def _scan(q, kt, v, gcol, grow, ktc, vc, growc, bsz, seq, ctx_len):
    pairs = HEADS // 2
    nc = seq // CHUNK
    return pl.pallas_call(
        _scan_kernel,
        grid=(bsz, pairs, SCAN_SPLITS),
        in_specs=[
            pl.BlockSpec((seq, 2 * DQK), lambda b, p, s: (b, p)),
            pl.BlockSpec((2 * DQK, seq), lambda b, p, s: (p, b)),
            pl.BlockSpec((seq, 2 * DV), lambda b, p, s: (b, p)),
            pl.BlockSpec((1, seq, BCOL_W), lambda b, p, s: (p, b, 0)),
            pl.BlockSpec((6, 1, 2, seq), lambda b, p, s: (0, p, 0, b)),
            pl.BlockSpec((2 * DQK, ctx_len), lambda b, p, s: (p, b)),
            pl.BlockSpec((ctx_len, 2 * DV), lambda b, p, s: (b, p)),
            pl.BlockSpec((6, 1, 2, ctx_len), lambda b, p, s: (0, p, 0, b)),
        ],
        out_specs=pl.BlockSpec((seq // SCAN_SPLITS, 2 * DV),
                               lambda b, p, s: (b * SCAN_SPLITS + s, p)),
        out_shape=jax.ShapeDtypeStruct((bsz * seq, V_W), F32),
        scratch_shapes=[pltpu.VMEM((2, 2, DQK, 2 * DV), F32),
                        pltpu.VMEM((2, nc, 2 * DQK, 2 * DV), BF16)],
        compiler_params=_params(3),
        name="mlstm_scan",
    )(q, kt, v, gcol, grow, ktc, vc, growc)


def _mlp_tail(x1, n2_ref, sh2_ref, sc2_ref, g2_ref, w1_ref, w2_ref):
    u = (_rmsnorm(x1, n2_ref[...]) * (1.0 + sc2_ref[0]) + sh2_ref[0]).astype(BF16)
    acc = jnp.zeros(x1.shape, F32)
    for k in range(D_FF // FF_TILE):
        hid = jnp.maximum(_dot(u, w1_ref[:, k * FF_TILE:(k + 1) * FF_TILE]), 0.0)
        acc = acc + _dot((hid * hid).astype(BF16), w2_ref[k * FF_TILE:(k + 1) * FF_TILE, :])
    return x1 + g2_ref[0] * acc


def _mlstm_post_kernel(h_ref, o_ref, x_ref, ng_ref, wout_ref, g1_ref, n2_ref, sh2_ref, sc2_ref,
                       g2_ref, w1_ref, w2_ref, out_ref):
    gate = jax.nn.sigmoid(o_ref[...])
    parts = []
    for k in range(HEADS):
        hk = h_ref[:, k * DV:(k + 1) * DV]
        parts.append(hk * lax.rsqrt(jnp.mean(hk * hk, axis=-1, keepdims=True) + NORM_EPS))
    hn = jnp.concatenate(parts, axis=1) * ng_ref[...]
    y = (gate * hn).astype(BF16)
    x1 = x_ref[...] + g1_ref[0] * _dot(y, wout_ref[...])
    out_ref[...] = _mlp_tail(x1, n2_ref, sh2_ref, sc2_ref, g2_ref, w1_ref, w2_ref)


def _mlstm_post(h, o, x2d, norm_g, w_out, g1, n2, sh2, sc2, g2, w1, w2, rows_per_mod):
    n = x2d.shape[0]
    tm = ROW_TILE
    tiles_per_mod = rows_per_mod // tm
    row = pl.BlockSpec((tm, D_MODEL), lambda i: (i, 0))
    mod = pl.BlockSpec((1, 1, D_MODEL), lambda i: (i // tiles_per_mod, 0, 0))
    return pl.pallas_call(
        _mlstm_post_kernel,
        grid=(n // tm,),
        in_specs=[row, row, row, _resident((1, V_W)), _resident(w_out.shape), mod,
                  _resident((1, D_MODEL)), mod, mod, mod, _resident(w1.shape), _resident(w2.shape)],
        out_specs=row,
        out_shape=jax.ShapeDtypeStruct((n, D_MODEL), F32),
        compiler_params=_params(1),
        name="mlstm_post_mlp",
    )(h, o, x2d, norm_g.reshape(1, V_W), w_out, g1, n2.reshape(1, D_MODEL), sh2, sc2, g2, w1, w2)


def _shift_rows(a, k, pos):
    n = a.shape[0]
    rolled = pltpu.roll(a, k % n, axis=0)
    valid = (pos >= k) if k > 0 else (pos < GRID_W + k)
    return jnp.where(valid, rolled, 0.0)


def _window_sum(u, window, pos):
    half = window // 2
    back, fwd, width = u, u, 1
    while width < half:
        back = back + _shift_rows(back, width, pos)
        fwd = fwd + _shift_rows(fwd, -width, pos)
        width *= 2
    return _shift_rows(back, 1, pos) + fwd


def _pool_layer_kernel(x_ref, n1_ref, sh1_ref, sc1_ref, g1_ref, pw_ref, ps_ref, n2_ref, sh2_ref,
                       sc2_ref, g2_ref, w1_ref, w2_ref, fg_ref, out_ref):
    x = x_ref[...]
    tm = x.shape[0]
    xn = _rmsnorm(x, n1_ref[...]) * (1.0 + sc1_ref[0]) + sh1_ref[0]
    pos = lax.broadcasted_iota(jnp.int32, (tm, POOL_GW), 0) % GRID_W
    ys = []
    for gi, window in enumerate(POOL_WINDOWS):
        u = xn[:, gi * POOL_GW:(gi + 1) * POOL_GW]
        lo = jnp.maximum(pos - window // 2, 0)
        hi = jnp.minimum(pos - window // 2 + window, GRID_W)
        pooled = _window_sum(u, window, pos) * (1.0 / (hi - lo).astype(F32))
        ys.append(_dot((pooled - u).astype(BF16), pw_ref[gi]))
    y = jnp.concatenate(ys, axis=1) * ps_ref[...]
    x1 = x + g1_ref[0] * y
    x2 = _mlp_tail(x1, n2_ref, sh2_ref, sc2_ref, g2_ref, w1_ref, w2_ref)
    out_ref[...] = _rmsnorm(x2, fg_ref[...])


def _pool_layer(x2d, n1, sh1, sc1, g1, pool_w, pool_scale, n2, sh2, sc2, g2, w1, w2, final_g,
                rows_per_mod):
    n = x2d.shape[0]
    tm = ROW_TILE
    tiles_per_mod = rows_per_mod // tm
    row = pl.BlockSpec((tm, D_MODEL), lambda i: (i, 0))
    mod = pl.BlockSpec((1, 1, D_MODEL), lambda i: (i // tiles_per_mod, 0, 0))
    vec = _resident((1, D_MODEL))
    return pl.pallas_call(
        _pool_layer_kernel,
        grid=(n // tm,),
        in_specs=[row, vec, mod, mod, mod, _resident(pool_w.shape), vec, vec, mod, mod, mod,
                  _resident(w1.shape), _resident(w2.shape), vec],
        out_specs=row,
        out_shape=jax.ShapeDtypeStruct((n, D_MODEL), F32),
        compiler_params=_params(1),
        name="pool_mlp_final",
    )(x2d, n1.reshape(1, D_MODEL), sh1, sc1, g1, pool_w, pool_scale.reshape(1, D_MODEL),
      n2.reshape(1, D_MODEL), sh2, sc2, g2, w1, w2, final_g.reshape(1, D_MODEL))


def kernel(x, c, ctx, c_ctx, ada_w, ada_b, norm1_g, norm2_g, mlstm_w_in, mlstm_gate_b,
           mlstm_norm_g, mlstm_w_out, pool_w, pool_scale, mlp_w1, mlp_w2, final_g):
    bsz, seq, dm = x.shape
    ctx_len = ctx.shape[1]
    assert dm == D_MODEL and seq % (2 * CHUNK) == 0 and ctx_len % CHUNK == 0
    assert seq % ROW_TILE == 0 and ROW_TILE % GRID_W == 0 and bsz + 1 <= MOD_ROWS
    assert ada_w.shape[0] == 2 and mlstm_w_in.shape[0] == 1 and pool_w.shape[0] == 1

    cmat = jnp.concatenate(
        [c, c_ctx[None, :], jnp.zeros((MOD_ROWS - bsz - 1, dm), F32)], axis=0)
    mod = _modulation(cmat, ada_w, ada_b)

    def mod_rows(layer, which, rows):
        return mod[layer, rows, which * dm:(which + 1) * dm][:, None, :]

    lat = slice(0, bsz)
    cx = slice(bsz, bsz + 1)

    w_in = mlstm_w_in[0]
    o_start = 2 * QK_W + V_W
    g_start = o_start + V_W
    wg = w_in[:, g_start:]
    weights = (
        w_in[:, :QK_W].astype(BF16),
        w_in[:, QK_W:2 * QK_W].T.astype(BF16),
        w_in[:, 2 * QK_W:o_start].astype(BF16),
        w_in[:, o_start:g_start].astype(BF16),
        jnp.pad(wg, ((0, 0), (0, LANES - GATE_W))).astype(BF16),
        wg.T.astype(BF16),
    )
    gate_b = mlstm_gate_b[0].reshape(GATE_W)
    bias_col = jnp.pad(gate_b, (0, LANES - GATE_W)).reshape(1, LANES)
    bias_row = gate_b.reshape(GATE_W, 1)

    x2d = x.reshape(bsz * seq, dm)
    q, kt, v, o, graw_c, graw_r = _front(
        x2d, norm1_g[0], mod_rows(0, 0, lat), mod_rows(0, 1, lat), weights, seq, True)
    ktc, vc, crawc, crawr = _front(
        ctx.reshape(bsz * ctx_len, dm), norm1_g[0], mod_rows(0, 0, cx), mod_rows(0, 1, cx),
        weights, bsz * ctx_len, False)
    gcol, grow = _gate_prep(graw_c, graw_r, bias_col, bias_row)
    _, growc = _gate_prep(crawc, crawr, bias_col, bias_row)

    pairs = HEADS // 2
    gcol = gcol[:, :, :GATE_W].reshape(3, bsz * seq, 4, pairs, 2)[:, :, 1::2]
    gcol = gcol.transpose(3, 1, 0, 2, 4).reshape(pairs, bsz * seq, 12)
    gcol = jnp.pad(gcol, ((0, 0), (0, 0), (0, BCOL_W - 12)))
    grow = grow.reshape(6, pairs, 2, bsz * seq)
    growc = growc.reshape(6, pairs, 2, bsz * ctx_len)

    h = _scan(q, kt, v, gcol, grow, ktc, vc, growc, bsz, seq, ctx_len)
    x2d = _mlstm_post(
        h, o, x2d, mlstm_norm_g[0], mlstm_w_out[0].astype(BF16), mod_rows(0, 2, lat),
        norm2_g[0], mod_rows(0, 3, lat), mod_rows(0, 4, lat), mod_rows(0, 5, lat),
        mlp_w1[0].astype(BF16), mlp_w2[0].astype(BF16), seq)

    out = _pool_layer(
        x2d, norm1_g[1], mod_rows(1, 0, lat), mod_rows(1, 1, lat), mod_rows(1, 2, lat),
        pool_w[0].astype(BF16), pool_scale[0], norm2_g[1], mod_rows(1, 3, lat),
        mod_rows(1, 4, lat), mod_rows(1, 5, lat), mlp_w1[1].astype(BF16),
        mlp_w2[1].astype(BF16), final_g, seq)
    return out.reshape(bsz, seq, dm)
```

```python
import functools

import jax
import jax.numpy as jnp
import numpy as np
from jax import lax
from jax.experimental import pallas as pl
from jax.experimental.pallas import tpu as pltpu

F32 = jnp.float32
BF16 = jnp.bfloat16

D_MODEL = 1024
HEADS = 8
DV = 128
DQK = 64
QK_W = HEADS * DQK
V_W = HEADS * DV
GATE_W = 4 * HEADS
D_FF = 4 * D_MODEL
NORM_EPS = 1e-6
GATE_SOFTCAP = 15.0
GRID_W = 64
POOL_WINDOWS = (2, 4, 8, 16)
POOL_GW = D_MODEL // len(POOL_WINDOWS)

LANES = 128
CHUNK = LANES
MOD_ROWS = 8
VMEM_LIMIT = 56 * 1024 * 1024
ROW_TILE = 512
FF_TILE = 1024
NEG_BIG = -1e30


def _params(n_axes):
    return pltpu.CompilerParams(
        dimension_semantics=("arbitrary",) * n_axes, vmem_limit_bytes=VMEM_LIMIT)


def _resident(shape):
    return pl.BlockSpec(shape, lambda *_: (0,) * len(shape), pipeline_mode=pl.Buffered(1))


def _layer_slab(shape, layer):
    return pl.BlockSpec((1,) + tuple(shape[1:]), lambda *_: (layer,) + (0,) * (len(shape) - 1),
                        pipeline_mode=pl.Buffered(1))


def _rmsnorm(x, g):
    return x * lax.rsqrt(jnp.mean(x * x, axis=-1, keepdims=True) + NORM_EPS) * g


def _dot(a, b):
    return jnp.dot(a, b, preferred_element_type=F32)


def _dot_f32(a, b):
    return jnp.dot(a, b, precision=lax.Precision.HIGHEST, preferred_element_type=F32)


def _mod_kernel(c_ref, w_ref, b_ref, o_ref):
    c = c_ref[...]
    s = c * jax.nn.sigmoid(c)
    o_ref[0] = _dot_f32(s, w_ref[0]) + b_ref[0]


def _modulation(cmat, ada_w, ada_b):
    depth, _, n = ada_w.shape
    tn = 1536
    return pl.pallas_call(
        _mod_kernel,
        grid=(depth, n // tn),
        in_specs=[
            pl.BlockSpec((MOD_ROWS, D_MODEL), lambda i, j: (0, 0)),
            pl.BlockSpec((1, D_MODEL, tn), lambda i, j: (i, 0, j)),
            pl.BlockSpec((1, 1, tn), lambda i, j: (i, 0, j)),
        ],
        out_specs=pl.BlockSpec((1, MOD_ROWS, tn), lambda i, j: (i, 0, j)),
        out_shape=jax.ShapeDtypeStruct((depth, MOD_ROWS, n), F32),
        compiler_params=_params(2),
        name="adaln_modulation",
    )(cmat, ada_w, ada_b.reshape(depth, 1, n))


V_START = 2 * QK_W
O_START = V_START + V_W
G_START = O_START + V_W


def _front_kernel(with_qo, x_ref, g_ref, sh_ref, sc_ref, w_ref, wkgt_ref, *outs):
    xn = _rmsnorm(x_ref[...], g_ref[...]) * (1.0 + sc_ref[0]) + sh_ref[0]
    xb = xn.astype(BF16)
    if with_qo:
        q_ref, kt_ref, v_ref, o_ref, gr_ref = outs
        q_ref[...] = (_dot(xb, w_ref[:, :QK_W]) * (DQK ** -0.5)).astype(BF16)
        o_ref[...] = _dot(xb, w_ref[:, O_START:G_START])
    else:
        kt_ref, v_ref, gr_ref = outs
    kg = lax.dot_general(wkgt_ref[...], xb, (((1,), (1,)), ((), ())), preferred_element_type=F32)
    kt_ref[...] = kg[:QK_W].astype(BF16)
    gr_ref[...] = kg[QK_W:]
    v_ref[...] = _dot(xb, w_ref[:, V_START:O_START]).astype(BF16)


def _front(x2d, norm_g, sh, sc, w_in, wkgt, rows_per_mod, with_qo):
    n = x2d.shape[0]
    tm = min(ROW_TILE, n)
    tiles_per_mod = rows_per_mod // tm
    row = lambda i: (i, 0)
    mod = lambda i: (i // tiles_per_mod, 0, 0)
    out_shape, out_specs = [], []
    if with_qo:
        out_shape.append(jax.ShapeDtypeStruct((n, QK_W), BF16))
        out_specs.append(pl.BlockSpec((tm, QK_W), row))
    out_shape.append(jax.ShapeDtypeStruct((QK_W, n), BF16))
    out_specs.append(pl.BlockSpec((QK_W, tm), lambda i: (0, i)))
    out_shape.append(jax.ShapeDtypeStruct((n, V_W), BF16))
    out_specs.append(pl.BlockSpec((tm, V_W), row))
    if with_qo:
        out_shape.append(jax.ShapeDtypeStruct((n, V_W), F32))
        out_specs.append(pl.BlockSpec((tm, V_W), row))
    out_shape.append(jax.ShapeDtypeStruct((GATE_W, n), F32))
    out_specs.append(pl.BlockSpec((GATE_W, tm), lambda i: (0, i)))
    return pl.pallas_call(
        functools.partial(_front_kernel, with_qo),
        grid=(n // tm,),
        in_specs=[
            pl.BlockSpec((tm, D_MODEL), row),
            _resident((1, D_MODEL)),
            pl.BlockSpec((1, 1, D_MODEL), mod),
            pl.BlockSpec((1, 1, D_MODEL), mod),
            _resident(w_in.shape), _resident(wkgt.shape),
        ],
        out_specs=out_specs,
        out_shape=out_shape,
        compiler_params=_params(1),
        name="mlstm_front_qo" if with_qo else "mlstm_front_ctx",
    )(x2d, norm_g.reshape(1, D_MODEL), sh, sc, w_in, wkgt)


def _softcap(g):
    return GATE_SOFTCAP * jnp.tanh(g * (1.0 / GATE_SOFTCAP))


def _log_sigmoid(g):
    return jnp.minimum(g, 0.0) - jnp.log1p(jnp.exp(-jnp.abs(g)))


PAIRS = HEADS // 2
BCOL_W = 16
GCOL_W = PAIRS * BCOL_W


def _split3(x):
    hi = x.astype(BF16)
    rest = x - hi.astype(F32)
    mid = rest.astype(BF16)
    return hi, mid, (rest - mid.astype(F32)).astype(BF16)


def _gate_kernel(gr_ref, br_ref, col_ref, row_ref):
    t = gr_ref.shape[1]
    r = lax.broadcasted_iota(jnp.int32, (CHUNK, CHUNK), 0)
    c = lax.broadcasted_iota(jnp.int32, (CHUNK, CHUNK), 1)
    lower = (c <= r).astype(BF16)
    upper = (c >= r).astype(BF16)
    ones = jnp.ones((CHUNK, CHUNK), BF16)
    row_rhs = jnp.concatenate([upper, lower, ones], axis=1)
    col_lhs = jnp.concatenate([lower, upper], axis=1)
    src = lax.broadcasted_iota(jnp.int32, (GATE_W, GCOL_W), 0)
    dst = lax.broadcasted_iota(jnp.int32, (GATE_W, GCOL_W), 1)
    place = [((src % 8 < 4) & (dst == (src // 8) * BCOL_W + 4 * i + src % 8)).astype(BF16)
             for i in range(3)]
    m = lax.broadcasted_iota(jnp.int32, (GATE_W, CHUNK), 0) % 8
    fwd_rows = m < 2
    bwd_rows = (m >= 2) & (m < 4)
    nt = (((1,), (1,)), ((), ()))

    gr = _softcap(gr_ref[...] + br_ref[...])
    li = pltpu.roll(gr, GATE_W - 4, axis=0)
    lf3 = _split3(_log_sigmoid(gr))

    for k in range(t // CHUNK):
        sl = slice(k * CHUNK, (k + 1) * CHUNK)
        cum = jnp.zeros((CHUNK, GATE_W), F32)
        y = jnp.zeros((GATE_W, 3 * CHUNK), F32)
        for piece in lf3:
            x = piece[:, sl]
            zero = jnp.zeros_like(x)
            xd = jnp.concatenate([jnp.where(fwd_rows, x, zero), jnp.where(bwd_rows, x, zero)],
                                 axis=1)
            cum = cum + lax.dot_general(col_lhs, xd, nt, preferred_element_type=F32)
            y = y + _dot(x, row_rhs)
        out = jnp.zeros((CHUNK, GCOL_W), F32)
        for piece, sel in zip(_split3(cum), place):
            out = out + _dot(piece, sel)
        col_ref[sl, :] = out.astype(BF16)
        total = y[:, 2 * CHUNK:]
        e = li[:, sl] - jnp.where(fwd_rows, y[:, :CHUNK], y[:, CHUNK:2 * CHUNK])
        used = fwd_rows | bwd_rows
        row_ref[0, :, sl] = jnp.where(used, e, 0.0)
        row_ref[1, :, sl] = jnp.where(used, jnp.exp(total + e), 0.0)
        row_ref[2, :, sl] = jnp.where(used, jnp.exp(total), 0.0)


def _gate_prep(gates_raw, bias):
    n = gates_raw.shape[1]
    t = min(1024, n)
    return pl.pallas_call(
        _gate_kernel,
        grid=(n // t,),
        in_specs=[
            pl.BlockSpec((GATE_W, t), lambda i: (0, i)),
            _resident((GATE_W, 1)),
        ],
        out_specs=[
            pl.BlockSpec((t, GCOL_W), lambda i: (i, 0)),
            pl.BlockSpec((3, GATE_W, t), lambda i: (0, 0, i)),
        ],
        out_shape=[
            jax.ShapeDtypeStruct((n, GCOL_W), BF16),
            jax.ShapeDtypeStruct((3, GATE_W, n), F32),
        ],
        compiler_params=_params(1),
        name="mlstm_gate_prep",
    )(gates_raw, bias)


SCAN_SPLITS = 2
STATE_UNROLL = 4
OUTPUT_UNROLL = 4


def _scan_kernel(q_ref, kt_ref, v_ref, gcol_ref, grow_ref, ktc_ref, vc_ref, growc_ref,
                 h_ref, cst_ref, cs_ref):
    seq = q_ref.shape[0]
    nc = seq // CHUNK
    ncc = ktc_ref.shape[1] // CHUNK
    nc_part = nc // SCAN_SPLITS
    part = pl.program_id(2)
    r = lax.broadcasted_iota(jnp.int32, (CHUNK, CHUNK), 0)
    c = lax.broadcasted_iota(jnp.int32, (CHUNK, CHUNK), 1)
    masks = (r >= c, r <= c)
    top = r < DQK
    left = c < DQK
    ones_blk = jnp.ones((CHUNK, DV), BF16)
    pair = pl.program_id(1)
    src = lax.broadcasted_iota(jnp.int32, (GCOL_W, 4 * CHUNK), 0)
    block = lax.broadcasted_iota(jnp.int32, (GCOL_W, 4 * CHUNK), 1) // CHUNK
    piece = src % BCOL_W
    sel = ((src // BCOL_W == pair) & (piece % 4 == block) & (piece < 12)).astype(BF16)

    def gate_row(ref, quantity, d, j, lanes):
        return ref[quantity, 2 * d + j:2 * d + j + 1, lanes]

    def state_increment(kt_j, v_j, w_row):
        ktw = (kt_j.astype(F32) * w_row).astype(BF16)
        return _dot(ktw, jnp.concatenate([v_j, ones_blk], axis=1))

    def decay2(a_row):
        return jnp.concatenate([a_row, a_row], axis=1)

    @pl.when(part == 0)
    def _():
        cst_ref[...] = jnp.zeros(cst_ref.shape, F32)
        for d in range(2):
            for step in range(ncc):
                k = step if d == 0 else ncc - 1 - step
                sl = slice(k * CHUNK, (k + 1) * CHUNK)
                ktp = ktc_ref[:, sl]
                vp = vc_ref[sl, :]
                for j in range(2):
                    inc = state_increment(ktp[j * DQK:(j + 1) * DQK], vp[:, j * DV:(j + 1) * DV],
                                          gate_row(growc_ref, 1, d, j, sl))
                    cst_ref[d, j] = cst_ref[d, j] * decay2(gate_row(growc_ref, 2, d, j, sl)) + inc

        def state_step(i, carry):
            for d in range(2):
                k = i if d == 0 else nc - 1 - i
                rows = pl.ds(pl.multiple_of(k * CHUNK, CHUNK), CHUNK)
                ktp = kt_ref[:, rows]
                vp = v_ref[rows, :]
                for j in range(2):
                    cst = cst_ref[d, j]
                    cs_ref[d, k, j * DQK:(j + 1) * DQK, :] = cst.astype(BF16)
                    inc = state_increment(ktp[j * DQK:(j + 1) * DQK], vp[:, j * DV:(j + 1) * DV],
                                          gate_row(grow_ref, 1, d, j, rows))
                    cst_ref[d, j] = cst * decay2(gate_row(grow_ref, 2, d, j, rows)) + inc
            return carry

        lax.fori_loop(0, nc, state_step, 0, unroll=STATE_UNROLL)

    def decay_stage(k):
        rows = pl.ds(pl.multiple_of(k * CHUNK, CHUNK), CHUNK)
        qp = q_ref[rows, :]
        ktp = kt_ref[:, rows]
        bb_all = _dot(gcol_ref[rows, :], sel)
        kbd = jnp.concatenate([jnp.where(top, ktp, jnp.zeros_like(ktp)),
                               jnp.where(top, jnp.zeros_like(ktp), ktp)], axis=1)
        s2 = _dot(qp, kbd)
        qf = qp.astype(F32)
        lhs = []
        for d in range(2):
            bb = [bb_all[:, (2 * d + j) * CHUNK:(2 * d + j + 1) * CHUNK] for j in range(2)]
            qs = (qf * jnp.exp(jnp.where(left, bb[0], bb[1]))).astype(BF16)
            for j in range(2):
                e_row = gate_row(grow_ref, 0, d, j, rows)
                dm = jnp.exp(jnp.where(masks[d], bb[j] + e_row, NEG_BIG))
                p = (s2[:, j * CHUNK:(j + 1) * CHUNK] * dm).astype(BF16)
                qs_j = jnp.where(left if j == 0 else ~left, qs, jnp.zeros_like(qs))
                lhs.append(jnp.concatenate([p, qs_j], axis=1))
        return lhs

    def value_stage(k, i, lhs):
        rows = pl.ds(pl.multiple_of(k * CHUNK, CHUNK), CHUNK)
        out_rows = pl.ds(pl.multiple_of(i * CHUNK, CHUNK), CHUNK)
        vp = v_ref[rows, :]
        for j in range(2):
            v1 = jnp.concatenate([vp[:, j * DV:(j + 1) * DV], ones_blk], axis=1)
            h = None
            for d in range(2):
                out = _dot(lhs[2 * d + j], jnp.concatenate([v1, cs_ref[d, k]], axis=0))
                hd = out[:, :DV] / jnp.maximum(jnp.abs(out[:, DV:]), 1.0)
                h = hd if d == 0 else h + hd
            h_ref[out_rows, j * DV:(j + 1) * DV] = h

    def output_trip(t, carry):
        i0 = t * OUTPUT_UNROLL
        staged = [decay_stage(part * nc_part + i0 + g) for g in range(OUTPUT_UNROLL)]
        for g in range(OUTPUT_UNROLL):
            value_stage(part * nc_part + i0 + g, i0 + g, staged[g])
        return carry

    lax.fori_loop(0, nc_part // OUTPUT_UNROLL, output_trip, 0)


def _scan(q, kt, v, gcol, grow, ktc, vc, growc, bsz, seq, ctx_len):
    nc = seq // CHUNK
    return pl.pallas_call(
        _scan_kernel,
        grid=(bsz, PAIRS, SCAN_SPLITS),
        in_specs=[
            pl.BlockSpec((seq, 2 * DQK), lambda b, p, s: (b, p)),
            pl.BlockSpec((2 * DQK, seq), lambda b, p, s: (p, b)),
            pl.BlockSpec((seq, 2 * DV), lambda b, p, s: (b, p)),
            pl.BlockSpec((seq, GCOL_W), lambda b, p, s: (b, 0)),
            pl.BlockSpec((3, 8, seq), lambda b, p, s: (0, p, b)),
            pl.BlockSpec((2 * DQK, ctx_len), lambda b, p, s: (p, b)),
            pl.BlockSpec((ctx_len, 2 * DV), lambda b, p, s: (b, p)),
            pl.BlockSpec((3, 8, ctx_len), lambda b, p, s: (0, p, b)),
        ],
        out_specs=pl.BlockSpec((seq // SCAN_SPLITS, 2 * DV),
                               lambda b, p, s: (b * SCAN_SPLITS + s, p)),
        out_shape=jax.ShapeDtypeStruct((bsz * seq, V_W), F32),
        scratch_shapes=[pltpu.VMEM((2, 2, DQK, 2 * DV), F32),
                        pltpu.VMEM((2, nc, 2 * DQK, 2 * DV), BF16)],
        compiler_params=_params(3),
        name="mlstm_scan",
    )(q, kt, v, gcol, grow, ktc, vc, growc)


def _mlp_tail(x1, n2_ref, sh2_ref, sc2_ref, g2_ref, w1_ref, w2_ref):
    u = (_rmsnorm(x1, n2_ref[...]) * (1.0 + sc2_ref[0]) + sh2_ref[0]).astype(BF16)
    acc = jnp.zeros(x1.shape, F32)
    for k in range(D_FF // FF_TILE):
        hid = jnp.maximum(_dot(u, w1_ref[0, :, k * FF_TILE:(k + 1) * FF_TILE]), 0.0)
        acc = acc + _dot((hid * hid).astype(BF16), w2_ref[0, k * FF_TILE:(k + 1) * FF_TILE, :])
    return x1 + g2_ref[0] * acc


def _mlstm_post_kernel(h_ref, o_ref, x_ref, ng_ref, wout_ref, g1_ref, n2_ref, sh2_ref, sc2_ref,
                       g2_ref, w1_ref, w2_ref, out_ref):
    gate = jax.nn.sigmoid(o_ref[...])
    parts = []
    for k in range(HEADS):
        hk = h_ref[:, k * DV:(k + 1) * DV]
        parts.append(hk * lax.rsqrt(jnp.mean(hk * hk, axis=-1, keepdims=True) + NORM_EPS))
    hn = jnp.concatenate(parts, axis=1) * ng_ref[...]
    y = (gate * hn).astype(BF16)
    x1 = x_ref[...] + g1_ref[0] * _dot(y, wout_ref[...])
    out_ref[...] = _mlp_tail(x1, n2_ref, sh2_ref, sc2_ref, g2_ref, w1_ref, w2_ref)


def _mlstm_post(h, o, x2d, norm_g, w_out, g1, n2, sh2, sc2, g2, w1, w2, layer, rows_per_mod):
    n = x2d.shape[0]
    tm = ROW_TILE
    tiles_per_mod = rows_per_mod // tm
    row = pl.BlockSpec((tm, D_MODEL), lambda i: (i, 0))
    mod = pl.BlockSpec((1, 1, D_MODEL), lambda i: (i // tiles_per_mod, 0, 0))
    return pl.pallas_call(
        _mlstm_post_kernel,
        grid=(n // tm,),
        in_specs=[row, row, row, _resident((1, V_W)), _resident(w_out.shape), mod,
                  _resident((1, D_MODEL)), mod, mod, mod, _layer_slab(w1.shape, layer), _layer_slab(w2.shape, layer)],
        out_specs=row,
        out_shape=jax.ShapeDtypeStruct((n, D_MODEL), F32),
        compiler_params=_params(1),
        name="mlstm_post_mlp",
    )(h, o, x2d, norm_g.reshape(1, V_W), w_out, g1, n2.reshape(1, D_MODEL), sh2, sc2, g2, w1, w2)


POOL_ROWS = 256


def _pool_constants():
    t = np.arange(POOL_ROWS)
    pos = t % GRID_W
    bands, inv = [], []
    for window in POOL_WINDOWS:
        lo = np.clip(pos - window // 2, 0, GRID_W)
        hi = np.clip(pos - window // 2 + window, 0, GRID_W)
        same_group = t[:, None] // GRID_W == t[None, :] // GRID_W
        bands.append(same_group & (pos[None, :] >= lo[:, None]) & (pos[None, :] < hi[:, None]))
        inv.append(np.broadcast_to((1.0 / (hi - lo))[:, None], (POOL_ROWS, POOL_GW)))
    return (jnp.asarray(np.stack(bands), dtype=BF16), jnp.asarray(np.stack(inv), dtype=F32))


def _pool_layer_kernel(x_ref, n1_ref, sh1_ref, sc1_ref, g1_ref, band_ref, inv_ref, pw_ref, ps_ref,
                       n2_ref, sh2_ref, sc2_ref, g2_ref, w1_ref, w2_ref, fg_ref, out_ref):
    x = x_ref[...]
    tm = x.shape[0]
    xn = _rmsnorm(x, n1_ref[...]) * (1.0 + sc1_ref[0]) + sh1_ref[0]
    hi = xn.astype(BF16)
    lo = (xn - hi.astype(F32)).astype(BF16)
    blocks = []
    for rc in range(tm // POOL_ROWS):
        rows = slice(rc * POOL_ROWS, (rc + 1) * POOL_ROWS)
        ys = []
        for gi in range(len(POOL_WINDOWS)):
            cols = slice(gi * POOL_GW, (gi + 1) * POOL_GW)
            band = band_ref[gi]
            total = _dot(band, hi[rows, cols]) + _dot(band, lo[rows, cols])
            p = total * inv_ref[gi] - xn[rows, cols]
            ys.append(_dot(p.astype(BF16), pw_ref[gi]))
        blocks.append(jnp.concatenate(ys, axis=1))
    y = jnp.concatenate(blocks, axis=0) * ps_ref[...]
    x1 = x + g1_ref[0] * y
    x2 = _mlp_tail(x1, n2_ref, sh2_ref, sc2_ref, g2_ref, w1_ref, w2_ref)
    out_ref[...] = _rmsnorm(x2, fg_ref[...])


def _pool_layer(x2d, n1, sh1, sc1, g1, pool_w, pool_scale, n2, sh2, sc2, g2, w1, w2, layer,
                final_g, rows_per_mod):
    n = x2d.shape[0]
    tm = ROW_TILE
    tiles_per_mod = rows_per_mod // tm
    row = pl.BlockSpec((tm, D_MODEL), lambda i: (i, 0))
    mod = pl.BlockSpec((1, 1, D_MODEL), lambda i: (i // tiles_per_mod, 0, 0))
    vec = _resident((1, D_MODEL))
    bands, inv = _pool_constants()
    return pl.pallas_call(
        _pool_layer_kernel,
        grid=(n // tm,),
        in_specs=[row, vec, mod, mod, mod, _resident(bands.shape), _resident(inv.shape),
                  _resident(pool_w.shape), vec, vec, mod, mod, mod,
                  _layer_slab(w1.shape, layer), _layer_slab(w2.shape, layer), vec],
        out_specs=row,
        out_shape=jax.ShapeDtypeStruct((n, D_MODEL), F32),
        compiler_params=_params(1),
        name="pool_mlp_final",
    )(x2d, n1.reshape(1, D_MODEL), sh1, sc1, g1, bands, inv, pool_w,
      pool_scale.reshape(1, D_MODEL),
      n2.reshape(1, D_MODEL), sh2, sc2, g2, w1, w2, final_g.reshape(1, D_MODEL))


def kernel(x, c, ctx, c_ctx, ada_w, ada_b, norm1_g, norm2_g, mlstm_w_in, mlstm_gate_b,
           mlstm_norm_g, mlstm_w_out, pool_w, pool_scale, mlp_w1, mlp_w2, final_g):
    bsz, seq, dm = x.shape
    ctx_len = ctx.shape[1]
    assert dm == D_MODEL and seq % (2 * CHUNK) == 0 and ctx_len % CHUNK == 0
    assert seq % ROW_TILE == 0 and ROW_TILE % GRID_W == 0 and bsz + 1 <= MOD_ROWS
    assert ada_w.shape[0] == 2 and mlstm_w_in.shape[0] == 1 and pool_w.shape[0] == 1

    cmat = jnp.concatenate(
        [c, c_ctx[None, :], jnp.zeros((MOD_ROWS - bsz - 1, dm), F32)], axis=0)
    mod = _modulation(cmat, ada_w, ada_b)

    def mod_rows(layer, which, rows):
        return mod[layer, rows, which * dm:(which + 1) * dm][:, None, :]

    lat = slice(0, bsz)
    cx = slice(bsz, bsz + 1)

    def pair_major(g):
        g4 = g.reshape(g.shape[:-1] + (4, PAIRS, 2))
        g4 = jnp.stack([g4[..., t, :, :] for t in (1, 3, 0, 2)], axis=-2)
        return g4.reshape(g.shape)

    w_in = mlstm_w_in[0].astype(BF16)
    wkgt = jnp.concatenate([w_in[:, QK_W:V_START], pair_major(w_in[:, G_START:])], axis=1).T
    gate_b = pair_major(mlstm_gate_b[0].reshape(GATE_W)).reshape(GATE_W, 1)
    w1 = mlp_w1.astype(BF16)
    w2 = mlp_w2.astype(BF16)

    x2d = x.reshape(bsz * seq, dm)
    q, kt, v, o, gates = _front(
        x2d, norm1_g[0], mod_rows(0, 0, lat), mod_rows(0, 1, lat), w_in, wkgt, seq, True)
    ktc, vc, gates_c = _front(
        ctx.reshape(bsz * ctx_len, dm), norm1_g[0], mod_rows(0, 0, cx), mod_rows(0, 1, cx),
        w_in, wkgt, bsz * ctx_len, False)
    gcol, grow = _gate_prep(gates, gate_b)
    _, growc = _gate_prep(gates_c, gate_b)

    h = _scan(q, kt, v, gcol, grow, ktc, vc, growc, bsz, seq, ctx_len)
    x2d = _mlstm_post(
        h, o, x2d, mlstm_norm_g[0], mlstm_w_out[0].astype(BF16), mod_rows(0, 2, lat),
        norm2_g[0], mod_rows(0, 3, lat), mod_rows(0, 4, lat), mod_rows(0, 5, lat), w1, w2, 0, seq)

    out = _pool_layer(
        x2d, norm1_g[1], mod_rows(1, 0, lat), mod_rows(1, 1, lat), mod_rows(1, 2, lat),
        pool_w[0].astype(BF16), pool_scale[0], norm2_g[1], mod_rows(1, 3, lat),
        mod_rows(1, 4, lat), mod_rows(1, 5, lat), w1, w2, 1, final_g, seq)
    return out.reshape(bsz, seq, dm)
```

```python
import functools

import jax
import jax.numpy as jnp
import numpy as np
from jax import lax
from jax.experimental import pallas as pl
from jax.experimental.pallas import tpu as pltpu

F32 = jnp.float32
BF16 = jnp.bfloat16

D_MODEL = 1024
HEADS = 8
DV = 128
DQK = 64
QK_W = HEADS * DQK
V_W = HEADS * DV
GATE_W = 4 * HEADS
D_FF = 4 * D_MODEL
NORM_EPS = 1e-6
GATE_SOFTCAP = 15.0
GRID_W = 64
POOL_WINDOWS = (2, 4, 8, 16)
POOL_GW = D_MODEL // len(POOL_WINDOWS)

LANES = 128
CHUNK = LANES
MOD_ROWS = 8
VMEM_LIMIT = 56 * 1024 * 1024
ROW_TILE = 512
FF_TILE = 1024
NEG_BIG = -1e30
LOG2E = 1.4426950408889634


def _params(n_axes):
    return pltpu.CompilerParams(
        dimension_semantics=("arbitrary",) * n_axes, vmem_limit_bytes=VMEM_LIMIT)


def _resident(shape):
    return pl.BlockSpec(shape, lambda *_: (0,) * len(shape), pipeline_mode=pl.Buffered(1))


def _layer_slab(shape, layer):
    return pl.BlockSpec((1,) + tuple(shape[1:]), lambda *_: (layer,) + (0,) * (len(shape) - 1),
                        pipeline_mode=pl.Buffered(1))


def _rmsnorm(x, g):
    return x * lax.rsqrt(jnp.mean(x * x, axis=-1, keepdims=True) + NORM_EPS) * g


def _dot(a, b):
    return jnp.dot(a, b, preferred_element_type=F32)


def _dot_f32(a, b):
    return jnp.dot(a, b, precision=lax.Precision.HIGHEST, preferred_element_type=F32)


def _mod_kernel(c_ref, w_ref, b_ref, o_ref):
    c = c_ref[...]
    s = c * jax.nn.sigmoid(c)
    o_ref[0] = _dot_f32(s, w_ref[0]) + b_ref[0]


def _modulation(cmat, ada_w, ada_b):
    depth, _, n = ada_w.shape
    tn = 1536
    return pl.pallas_call(
        _mod_kernel,
        grid=(depth, n // tn),
        in_specs=[
            pl.BlockSpec((MOD_ROWS, D_MODEL), lambda i, j: (0, 0)),
            pl.BlockSpec((1, D_MODEL, tn), lambda i, j: (i, 0, j)),
            pl.BlockSpec((1, 1, tn), lambda i, j: (i, 0, j)),
        ],
        out_specs=pl.BlockSpec((1, MOD_ROWS, tn), lambda i, j: (i, 0, j)),
        out_shape=jax.ShapeDtypeStruct((depth, MOD_ROWS, n), F32),
        compiler_params=_params(2),
        name="adaln_modulation",
    )(cmat, ada_w, ada_b.reshape(depth, 1, n))


V_START = 2 * QK_W
O_START = V_START + V_W
G_START = O_START + V_W


def _front_kernel(with_qo, x_ref, g_ref, sh_ref, sc_ref, w_ref, wkgt_ref, *outs):
    xn = _rmsnorm(x_ref[...], g_ref[...]) * (1.0 + sc_ref[0]) + sh_ref[0]
    xb = xn.astype(BF16)
    if with_qo:
        q_ref, kt_ref, v_ref, o_ref, gr_ref = outs
        q_ref[...] = (_dot(xb, w_ref[:, :QK_W]) * (DQK ** -0.5)).astype(BF16)
        o_ref[...] = _dot(xb, w_ref[:, O_START:G_START])
    else:
        kt_ref, v_ref, gr_ref = outs
    kg = lax.dot_general(wkgt_ref[...], xb, (((1,), (1,)), ((), ())), preferred_element_type=F32)
    kt_ref[...] = kg[:QK_W].astype(BF16)
    gr_ref[...] = kg[QK_W:]
    v_ref[...] = _dot(xb, w_ref[:, V_START:O_START]).astype(BF16)


def _front(x2d, norm_g, sh, sc, w_in, wkgt, rows_per_mod, with_qo):
    n = x2d.shape[0]
    tm = min(ROW_TILE, n)
    tiles_per_mod = rows_per_mod // tm
    row = lambda i: (i, 0)
    mod = lambda i: (i // tiles_per_mod, 0, 0)
    out_shape, out_specs = [], []
    if with_qo:
        out_shape.append(jax.ShapeDtypeStruct((n, QK_W), BF16))
        out_specs.append(pl.BlockSpec((tm, QK_W), row))
    out_shape.append(jax.ShapeDtypeStruct((QK_W, n), BF16))
    out_specs.append(pl.BlockSpec((QK_W, tm), lambda i: (0, i)))
    out_shape.append(jax.ShapeDtypeStruct((n, V_W), BF16))
    out_specs.append(pl.BlockSpec((tm, V_W), row))
    if with_qo:
        out_shape.append(jax.ShapeDtypeStruct((n, V_W), F32))
        out_specs.append(pl.BlockSpec((tm, V_W), row))
    out_shape.append(jax.ShapeDtypeStruct((GATE_W, n), F32))
    out_specs.append(pl.BlockSpec((GATE_W, tm), lambda i: (0, i)))
    return pl.pallas_call(
        functools.partial(_front_kernel, with_qo),
        grid=(n // tm,),
        in_specs=[
            pl.BlockSpec((tm, D_MODEL), row),
            _resident((1, D_MODEL)),
            pl.BlockSpec((1, 1, D_MODEL), mod),
            pl.BlockSpec((1, 1, D_MODEL), mod),
            _resident(w_in.shape), _resident(wkgt.shape),
        ],
        out_specs=out_specs,
        out_shape=out_shape,
        compiler_params=_params(1),
        name="mlstm_front_qo" if with_qo else "mlstm_front_ctx",
    )(x2d, norm_g.reshape(1, D_MODEL), sh, sc, w_in, wkgt)


def _softcap(g):
    return GATE_SOFTCAP * jnp.tanh(g * (1.0 / GATE_SOFTCAP))


def _log_sigmoid(g):
    return jnp.minimum(g, 0.0) - jnp.log1p(jnp.exp(-jnp.abs(g)))


PAIRS = HEADS // 2
BCOL_W = 16
GCOL_W = PAIRS * BCOL_W


def _split3(x):
    hi = x.astype(BF16)
    rest = x - hi.astype(F32)
    mid = rest.astype(BF16)
    return hi, mid, (rest - mid.astype(F32)).astype(BF16)


def _gate_kernel(gr_ref, br_ref, col_ref, row_ref):
    t = gr_ref.shape[1]
    r = lax.broadcasted_iota(jnp.int32, (CHUNK, CHUNK), 0)
    c = lax.broadcasted_iota(jnp.int32, (CHUNK, CHUNK), 1)
    lower = (c <= r).astype(BF16)
    upper = (c >= r).astype(BF16)
    ones = jnp.ones((CHUNK, CHUNK), BF16)
    row_rhs = jnp.concatenate([upper, lower, ones], axis=1)
    col_lhs = jnp.concatenate([lower, upper], axis=1)
    src = lax.broadcasted_iota(jnp.int32, (GATE_W, GCOL_W), 0)
    dst = lax.broadcasted_iota(jnp.int32, (GATE_W, GCOL_W), 1)
    place = [((src % 8 < 4) & (dst == (src // 8) * BCOL_W + 4 * i + src % 8)).astype(BF16)
             for i in range(3)]
    m = lax.broadcasted_iota(jnp.int32, (GATE_W, CHUNK), 0) % 8
    fwd_rows = m < 2
    bwd_rows = (m >= 2) & (m < 4)
    nt = (((1,), (1,)), ((), ()))

    gr = _softcap(gr_ref[...] + br_ref[...])
    li = pltpu.roll(gr, GATE_W - 4, axis=0)
    lf3 = _split3(_log_sigmoid(gr))

    for k in range(t // CHUNK):
        sl = slice(k * CHUNK, (k + 1) * CHUNK)
        cum = jnp.zeros((CHUNK, GATE_W), F32)
        y = jnp.zeros((GATE_W, 3 * CHUNK), F32)
        for piece in lf3:
            x = piece[:, sl]
            zero = jnp.zeros_like(x)
            xd = jnp.concatenate([jnp.where(fwd_rows, x, zero), jnp.where(bwd_rows, x, zero)],
                                 axis=1)
            cum = cum + lax.dot_general(col_lhs, xd, nt, preferred_element_type=F32)
            y = y + _dot(x, row_rhs)
        out = jnp.zeros((CHUNK, GCOL_W), F32)
        for piece, sel in zip(_split3(cum * LOG2E), place):
            out = out + _dot(piece, sel)
        col_ref[sl, :] = out.astype(BF16)
        total = y[:, 2 * CHUNK:]
        e = li[:, sl] - jnp.where(fwd_rows, y[:, :CHUNK], y[:, CHUNK:2 * CHUNK])
        used = fwd_rows | bwd_rows
        row_ref[0, :, sl] = jnp.where(used, e * LOG2E, 0.0)
        row_ref[1, :, sl] = jnp.where(used, jnp.exp(total + e), 0.0)
        row_ref[2, :, sl] = jnp.where(used, jnp.exp(total), 0.0)


def _gate_prep(gates_raw, bias):
    n = gates_raw.shape[1]
    t = min(1024, n)
    return pl.pallas_call(
        _gate_kernel,
        grid=(n // t,),
        in_specs=[
            pl.BlockSpec((GATE_W, t), lambda i: (0, i)),
            _resident((GATE_W, 1)),
        ],
        out_specs=[
            pl.BlockSpec((t, GCOL_W), lambda i: (i, 0)),
            pl.BlockSpec((3, GATE_W, t), lambda i: (0, 0, i)),
        ],
        out_shape=[
            jax.ShapeDtypeStruct((n, GCOL_W), BF16),
            jax.ShapeDtypeStruct((3, GATE_W, n), F32),
        ],
        compiler_params=_params(1),
        name="mlstm_gate_prep",
    )(gates_raw, bias)


SCAN_SPLITS = 2
STATE_UNROLL = 8
OUTPUT_UNROLL = 8


def _scan_kernel(q_ref, kt_ref, v_ref, gcol_ref, grow_ref, ktc_ref, vc_ref, growc_ref,
                 h_ref, cst_ref, cs_ref):
    seq = q_ref.shape[0]
    nc = seq // CHUNK
    ncc = ktc_ref.shape[1] // CHUNK
    nc_part = nc // SCAN_SPLITS
    part = pl.program_id(2)
    r = lax.broadcasted_iota(jnp.int32, (CHUNK, CHUNK), 0)
    c = lax.broadcasted_iota(jnp.int32, (CHUNK, CHUNK), 1)
    masks = (r >= c, r <= c)
    top = r < DQK
    left = c < DQK
    ones_blk = jnp.ones((CHUNK, DV), BF16)
    pair = pl.program_id(1)
    src = lax.broadcasted_iota(jnp.int32, (GCOL_W, 4 * CHUNK), 0)
    block = lax.broadcasted_iota(jnp.int32, (GCOL_W, 4 * CHUNK), 1) // CHUNK
    piece = src % BCOL_W
    sel = ((src // BCOL_W == pair) & (piece % 4 == block) & (piece < 12)).astype(BF16)

    def gate_row(ref, quantity, d, j, lanes):
        return ref[quantity, 2 * d + j:2 * d + j + 1, lanes]

    def state_increment(kt_j, v_j, w_row):
        ktw = (kt_j.astype(F32) * w_row).astype(BF16)
        return _dot(ktw, jnp.concatenate([v_j, ones_blk], axis=1))

    def decay2(a_row):
        return jnp.concatenate([a_row, a_row], axis=1)

    @pl.when(part == 0)
    def _():
        cst_ref[...] = jnp.zeros(cst_ref.shape, F32)
        for d in range(2):
            for step in range(ncc):
                k = step if d == 0 else ncc - 1 - step
                sl = slice(k * CHUNK, (k + 1) * CHUNK)
                ktp = ktc_ref[:, sl]
                vp = vc_ref[sl, :]
                for j in range(2):
                    inc = state_increment(ktp[j * DQK:(j + 1) * DQK], vp[:, j * DV:(j + 1) * DV],
                                          gate_row(growc_ref, 1, d, j, sl))
                    cst_ref[d, j] = cst_ref[d, j] * decay2(gate_row(growc_ref, 2, d, j, sl)) + inc

        def state_step(i, carry):
            for d in range(2):
                k = i if d == 0 else nc - 1 - i
                rows = pl.ds(pl.multiple_of(k * CHUNK, CHUNK), CHUNK)
                ktp = kt_ref[:, rows]
                vp = v_ref[rows, :]
                for j in range(2):
                    cst = cst_ref[d, j]
                    cs_ref[d, k, j * DQK:(j + 1) * DQK, :] = cst.astype(BF16)
                    inc = state_increment(ktp[j * DQK:(j + 1) * DQK], vp[:, j * DV:(j + 1) * DV],
                                          gate_row(grow_ref, 1, d, j, rows))
                    cst_ref[d, j] = cst * decay2(gate_row(grow_ref, 2, d, j, rows)) + inc
            return carry

        lax.fori_loop(0, nc, state_step, 0, unroll=STATE_UNROLL)

    def decay_stage(k):
        rows = pl.ds(pl.multiple_of(k * CHUNK, CHUNK), CHUNK)
        qp = q_ref[rows, :]
        ktp = kt_ref[:, rows]
        bb_all = _dot(gcol_ref[rows, :], sel)
        kbd = jnp.concatenate([jnp.where(top, ktp, jnp.zeros_like(ktp)),
                               jnp.where(top, jnp.zeros_like(ktp), ktp)], axis=1)
        s2 = _dot(qp, kbd)
        qf = qp.astype(F32)
        lhs = []
        for d in range(2):
            bb = [bb_all[:, (2 * d + j) * CHUNK:(2 * d + j + 1) * CHUNK] for j in range(2)]
            qs = (qf * jnp.exp2(jnp.where(left, bb[0], bb[1]))).astype(BF16)
            for j in range(2):
                e_row = gate_row(grow_ref, 0, d, j, rows)
                dm = jnp.exp2(jnp.where(masks[d], bb[j] + e_row, NEG_BIG))
                p = (s2[:, j * CHUNK:(j + 1) * CHUNK] * dm).astype(BF16)
                qs_j = jnp.where(left if j == 0 else ~left, qs, jnp.zeros_like(qs))
                lhs.append(jnp.concatenate([p, qs_j], axis=1))
        return lhs

    def value_stage(k, i, lhs):
        rows = pl.ds(pl.multiple_of(k * CHUNK, CHUNK), CHUNK)
        out_rows = pl.ds(pl.multiple_of(i * CHUNK, CHUNK), CHUNK)
        vp = v_ref[rows, :]
        for j in range(2):
            v1 = jnp.concatenate([vp[:, j * DV:(j + 1) * DV], ones_blk], axis=1)
            h = None
            for d in range(2):
                out = _dot(lhs[2 * d + j], jnp.concatenate([v1, cs_ref[d, k]], axis=0))
                hd = out[:, :DV] / jnp.maximum(jnp.abs(out[:, DV:]), 1.0)
                h = hd if d == 0 else h + hd
            h_ref[out_rows, j * DV:(j + 1) * DV] = h

    def output_trip(t, carry):
        i0 = t * OUTPUT_UNROLL
        staged = [decay_stage(part * nc_part + i0 + g) for g in range(OUTPUT_UNROLL)]
        for g in range(OUTPUT_UNROLL):
            value_stage(part * nc_part + i0 + g, i0 + g, staged[g])
        return carry

    lax.fori_loop(0, nc_part // OUTPUT_UNROLL, output_trip, 0)


def _scan(q, kt, v, gcol, grow, ktc, vc, growc, bsz, seq, ctx_len):
    nc = seq // CHUNK
    return pl.pallas_call(
        _scan_kernel,
        grid=(bsz, PAIRS, SCAN_SPLITS),
        in_specs=[
            pl.BlockSpec((seq, 2 * DQK), lambda b, p, s: (b, p)),
            pl.BlockSpec((2 * DQK, seq), lambda b, p, s: (p, b)),
            pl.BlockSpec((seq, 2 * DV), lambda b, p, s: (b, p)),
            pl.BlockSpec((seq, GCOL_W), lambda b, p, s: (b, 0)),
            pl.BlockSpec((3, 8, seq), lambda b, p, s: (0, p, b)),
            pl.BlockSpec((2 * DQK, ctx_len), lambda b, p, s: (p, b)),
            pl.BlockSpec((ctx_len, 2 * DV), lambda b, p, s: (b, p)),
            pl.BlockSpec((3, 8, ctx_len), lambda b, p, s: (0, p, b)),
        ],
        out_specs=pl.BlockSpec((seq // SCAN_SPLITS, 2 * DV),
                               lambda b, p, s: (b * SCAN_SPLITS + s, p)),
        out_shape=jax.ShapeDtypeStruct((bsz * seq, V_W), F32),
        scratch_shapes=[pltpu.VMEM((2, 2, DQK, 2 * DV), F32),
                        pltpu.VMEM((2, nc, 2 * DQK, 2 * DV), BF16)],
        compiler_params=_params(3),
        name="mlstm_scan",
    )(q, kt, v, gcol, grow, ktc, vc, growc)


MLP_TILE = 1024
SUB_ROWS = 256


def _mlp_pipeline(n_sub, mixer, finish, n2_ref, sh2_ref, sc2_ref, g2_ref, w1_ref, w2_ref):
    def normed(x1):
        return (_rmsnorm(x1, n2_ref[...]) * (1.0 + sc2_ref[0]) + sh2_ref[0]).astype(BF16)

    def slab(u, k):
        cols = slice(k * FF_TILE, (k + 1) * FF_TILE)
        hid = jnp.maximum(_dot(u, w1_ref[0, :, cols]), 0.0)
        return _dot((hid * hid).astype(BF16), w2_ref[0, cols, :])

    def advance(gen):
        try:
            next(gen)
            return gen, None
        except StopIteration as done:
            return None, done.value

    gen, x1 = mixer(0), None
    while gen is not None:
        gen, x1 = advance(gen)
    u = normed(x1)
    for s in range(n_sub):
        gen = mixer(s + 1) if s + 1 < n_sub else None
        x1_next = u_next = None
        acc = slab(u, 0)
        for k in range(1, D_FF // FF_TILE):
            if gen is not None:
                gen, x1_next = advance(gen)
                if gen is None:
                    u_next = normed(x1_next)
            acc = acc + slab(u, k)
        assert gen is None, "mixer has more phases than hidden slabs to hide them under"
        finish(s, x1 + g2_ref[0] * acc)
        x1, u = x1_next, u_next


def _mlstm_post_kernel(h_ref, o_ref, x_ref, ng_ref, wout_ref, g1_ref, n2_ref, sh2_ref, sc2_ref,
                       g2_ref, w1_ref, w2_ref, out_ref):
    def mixer(s):
        rows = slice(s * SUB_ROWS, (s + 1) * SUB_ROWS)
        gate = jax.nn.sigmoid(o_ref[rows, :])
        parts = []
        for k in range(HEADS):
            hk = h_ref[rows, k * DV:(k + 1) * DV]
            parts.append(hk * lax.rsqrt(jnp.mean(hk * hk, axis=-1, keepdims=True) + NORM_EPS))
        hn = jnp.concatenate(parts, axis=1) * ng_ref[...]
        y = (gate * hn).astype(BF16)
        return x_ref[rows, :] + g1_ref[0] * _dot(y, wout_ref[...])
        yield

    def finish(s, x2):
        out_ref[s * SUB_ROWS:(s + 1) * SUB_ROWS, :] = x2

    _mlp_pipeline(x_ref.shape[0] // SUB_ROWS, mixer, finish, n2_ref, sh2_ref, sc2_ref, g2_ref,
                  w1_ref, w2_ref)


def _mlstm_post(h, o, x2d, norm_g, w_out, g1, n2, sh2, sc2, g2, w1, w2, layer, rows_per_mod):
    n = x2d.shape[0]
    tm = MLP_TILE
    tiles_per_mod = rows_per_mod // tm
    row = pl.BlockSpec((tm, D_MODEL), lambda i: (i, 0))
    mod = pl.BlockSpec((1, 1, D_MODEL), lambda i: (i // tiles_per_mod, 0, 0))
    return pl.pallas_call(
        _mlstm_post_kernel,
        grid=(n // tm,),
        in_specs=[row, row, row, _resident((1, V_W)), _resident(w_out.shape), mod,
                  _resident((1, D_MODEL)), mod, mod, mod, _layer_slab(w1.shape, layer), _layer_slab(w2.shape, layer)],
        out_specs=row,
        out_shape=jax.ShapeDtypeStruct((n, D_MODEL), F32),
        compiler_params=_params(1),
        name="mlstm_post_mlp",
    )(h, o, x2d, norm_g.reshape(1, V_W), w_out, g1, n2.reshape(1, D_MODEL), sh2, sc2, g2, w1, w2)


POOL_ROWS = SUB_ROWS


def _pool_constants():
    t = np.arange(POOL_ROWS)
    pos = t % GRID_W
    bands, inv = [], []
    for window in POOL_WINDOWS:
        lo = np.clip(pos - window // 2, 0, GRID_W)
        hi = np.clip(pos - window // 2 + window, 0, GRID_W)
        same_group = t[:, None] // GRID_W == t[None, :] // GRID_W
        bands.append(same_group & (pos[None, :] >= lo[:, None]) & (pos[None, :] < hi[:, None]))
        inv.append(np.broadcast_to((1.0 / (hi - lo))[:, None], (POOL_ROWS, POOL_GW)))
    return (jnp.asarray(np.stack(bands), dtype=BF16), jnp.asarray(np.stack(inv), dtype=F32))


def _pool_layer_kernel(x_ref, n1_ref, sh1_ref, sc1_ref, g1_ref, band_ref, inv_ref, pw_ref, ps_ref,
                       n2_ref, sh2_ref, sc2_ref, g2_ref, w1_ref, w2_ref, fg_ref, out_ref):
    def mixer(s):
        x = x_ref[s * SUB_ROWS:(s + 1) * SUB_ROWS, :]
        xn = _rmsnorm(x, n1_ref[...]) * (1.0 + sc1_ref[0]) + sh1_ref[0]
        hi = xn.astype(BF16)
        lo = (xn - hi.astype(F32)).astype(BF16)
        ps = []
        for gi in range(len(POOL_WINDOWS)):
            cols = slice(gi * POOL_GW, (gi + 1) * POOL_GW)
            band = band_ref[gi]
            total = _dot(band, hi[:, cols]) + _dot(band, lo[:, cols])
            ps.append((total * inv_ref[gi] - xn[:, cols]).astype(BF16))
        yield
        ys = [_dot(p, pw_ref[gi]) for gi, p in enumerate(ps)]
        return x + g1_ref[0] * (jnp.concatenate(ys, axis=1) * ps_ref[...])

    def finish(s, x2):
        out_ref[s * SUB_ROWS:(s + 1) * SUB_ROWS, :] = _rmsnorm(x2, fg_ref[...])

    _mlp_pipeline(x_ref.shape[0] // SUB_ROWS, mixer, finish, n2_ref, sh2_ref, sc2_ref, g2_ref,
                  w1_ref, w2_ref)


def _pool_layer(x2d, n1, sh1, sc1, g1, pool_w, pool_scale, n2, sh2, sc2, g2, w1, w2, layer,
                final_g, rows_per_mod):
    n = x2d.shape[0]
    tm = MLP_TILE
    tiles_per_mod = rows_per_mod // tm
    row = pl.BlockSpec((tm, D_MODEL), lambda i: (i, 0))
    mod = pl.BlockSpec((1, 1, D_MODEL), lambda i: (i // tiles_per_mod, 0, 0))
    vec = _resident((1, D_MODEL))
    bands, inv = _pool_constants()
    return pl.pallas_call(
        _pool_layer_kernel,
        grid=(n // tm,),
        in_specs=[row, vec, mod, mod, mod, _resident(bands.shape), _resident(inv.shape),
                  _resident(pool_w.shape), vec, vec, mod, mod, mod,
                  _layer_slab(w1.shape, layer), _layer_slab(w2.shape, layer), vec],
        out_specs=row,
        out_shape=jax.ShapeDtypeStruct((n, D_MODEL), F32),
        compiler_params=_params(1),
        name="pool_mlp_final",
    )(x2d, n1.reshape(1, D_MODEL), sh1, sc1, g1, bands, inv, pool_w,
      pool_scale.reshape(1, D_MODEL),
      n2.reshape(1, D_MODEL), sh2, sc2, g2, w1, w2, final_g.reshape(1, D_MODEL))


def kernel(x, c, ctx, c_ctx, ada_w, ada_b, norm1_g, norm2_g, mlstm_w_in, mlstm_gate_b,
           mlstm_norm_g, mlstm_w_out, pool_w, pool_scale, mlp_w1, mlp_w2, final_g):
    bsz, seq, dm = x.shape
    ctx_len = ctx.shape[1]
    assert dm == D_MODEL and seq % (2 * CHUNK) == 0 and ctx_len % CHUNK == 0
    assert seq % ROW_TILE == 0 and seq % MLP_TILE == 0 and bsz + 1 <= MOD_ROWS
    assert MLP_TILE % SUB_ROWS == 0 and SUB_ROWS % GRID_W == 0
    assert ada_w.shape[0] == 2 and mlstm_w_in.shape[0] == 1 and pool_w.shape[0] == 1

    cmat = jnp.concatenate(
        [c, c_ctx[None, :], jnp.zeros((MOD_ROWS - bsz - 1, dm), F32)], axis=0)
    mod = _modulation(cmat, ada_w, ada_b)

    def mod_rows(layer, which, rows):
        return mod[layer, rows, which * dm:(which + 1) * dm][:, None, :]

    lat = slice(0, bsz)
    cx = slice(bsz, bsz + 1)

    def pair_major(g):
        g4 = g.reshape(g.shape[:-1] + (4, PAIRS, 2))
        g4 = jnp.stack([g4[..., t, :, :] for t in (1, 3, 0, 2)], axis=-2)
        return g4.reshape(g.shape)

    w_in = mlstm_w_in[0].astype(BF16)
    wkgt = jnp.concatenate([w_in[:, QK_W:V_START], pair_major(w_in[:, G_START:])], axis=1).T
    gate_b = pair_major(mlstm_gate_b[0].reshape(GATE_W)).reshape(GATE_W, 1)
    w1 = mlp_w1.astype(BF16)
    w2 = mlp_w2.astype(BF16)

    x2d = x.reshape(bsz * seq, dm)
    q, kt, v, o, gates = _front(
        x2d, norm1_g[0], mod_rows(0, 0, lat), mod_rows(0, 1, lat), w_in, wkgt, seq, True)
    ktc, vc, gates_c = _front(
        ctx.reshape(bsz * ctx_len, dm), norm1_g[0], mod_rows(0, 0, cx), mod_rows(0, 1, cx),
        w_in, wkgt, bsz * ctx_len, False)
    gcol, grow = _gate_prep(gates, gate_b)
    _, growc = _gate_prep(gates_c, gate_b)

    h = _scan(q, kt, v, gcol, grow, ktc, vc, growc, bsz, seq, ctx_len)
    x2d = _mlstm_post(
        h, o, x2d, mlstm_norm_g[0], mlstm_w_out[0].astype(BF16), mod_rows(0, 2, lat),
        norm2_g[0], mod_rows(0, 3, lat), mod_rows(0, 4, lat), mod_rows(0, 5, lat), w1, w2, 0, seq)

    out = _pool_layer(
        x2d, norm1_g[1], mod_rows(1, 0, lat), mod_rows(1, 1, lat), mod_rows(1, 2, lat),
        pool_w[0].astype(BF16), pool_scale[0], norm2_g[1], mod_rows(1, 3, lat),
        mod_rows(1, 4, lat), mod_rows(1, 5, lat), w1, w2, 1, final_g, seq)
    return out.reshape(bsz, seq, dm)
```

```python
import functools

import jax
import jax.numpy as jnp
import numpy as np
from jax import lax
from jax.experimental import pallas as pl
from jax.experimental.pallas import tpu as pltpu

F32 = jnp.float32
BF16 = jnp.bfloat16

D_MODEL = 1024
HEADS = 8
DV = 128
DQK = 64
QK_W = HEADS * DQK
V_W = HEADS * DV
GATE_W = 4 * HEADS
D_FF = 4 * D_MODEL
NORM_EPS = 1e-6
GATE_SOFTCAP = 15.0
GRID_W = 64
POOL_WINDOWS = (2, 4, 8, 16)
POOL_GW = D_MODEL // len(POOL_WINDOWS)

LANES = 128
CHUNK = LANES
MOD_ROWS = 8
VMEM_LIMIT = 56 * 1024 * 1024
ROW_TILE = 512
FF_TILE = 1024
NEG_BIG = -1e30
LOG2E = 1.4426950408889634


def _params(n_axes):
    return pltpu.CompilerParams(
        dimension_semantics=("arbitrary",) * n_axes, vmem_limit_bytes=VMEM_LIMIT)


def _resident(shape):
    return pl.BlockSpec(shape, lambda *_: (0,) * len(shape), pipeline_mode=pl.Buffered(1))


def _layer_slab(shape, layer):
    return pl.BlockSpec((1,) + tuple(shape[1:]), lambda *_: (layer,) + (0,) * (len(shape) - 1),
                        pipeline_mode=pl.Buffered(1))


def _rmsnorm(x, g):
    return x * lax.rsqrt(jnp.mean(x * x, axis=-1, keepdims=True) + NORM_EPS) * g


def _dot(a, b):
    return jnp.dot(a, b, preferred_element_type=F32)


def _split3(x):
    hi = x.astype(BF16)
    rest = x - hi.astype(F32)
    mid = rest.astype(BF16)
    return hi, mid, (rest - mid.astype(F32)).astype(BF16)


def _mod_kernel(c_ref, w_ref, b_ref, o_ref):
    c = c_ref[...]
    s = c * jax.nn.sigmoid(c)
    lhs = jnp.concatenate([p.astype(F32) for p in _split3(s)], axis=0).astype(BF16)
    acc = jnp.zeros((3 * MOD_ROWS, w_ref.shape[2]), F32)
    for piece in _split3(w_ref[0]):
        acc = acc + _dot(lhs, piece)
    o_ref[0] = acc[:MOD_ROWS] + acc[MOD_ROWS:2 * MOD_ROWS] + acc[2 * MOD_ROWS:] + b_ref[0]


def _modulation(cmat, ada_w, ada_b):
    depth, _, n = ada_w.shape
    tn = 1536
    return pl.pallas_call(
        _mod_kernel,
        grid=(depth, n // tn),
        in_specs=[
            pl.BlockSpec((MOD_ROWS, D_MODEL), lambda i, j: (0, 0)),
            pl.BlockSpec((1, D_MODEL, tn), lambda i, j: (i, 0, j)),
            pl.BlockSpec((1, 1, tn), lambda i, j: (i, 0, j)),
        ],
        out_specs=pl.BlockSpec((1, MOD_ROWS, tn), lambda i, j: (i, 0, j)),
        out_shape=jax.ShapeDtypeStruct((depth, MOD_ROWS, n), F32),
        compiler_params=_params(2),
        name="adaln_modulation",
    )(cmat, ada_w, ada_b.reshape(depth, 1, n))


V_START = 2 * QK_W
O_START = V_START + V_W
G_START = O_START + V_W


def _front_kernel(with_qo, x_ref, g_ref, sh_ref, sc_ref, w_ref, wkgt_ref, *outs):
    xn = _rmsnorm(x_ref[...], g_ref[...]) * (1.0 + sc_ref[0]) + sh_ref[0]
    xb = xn.astype(BF16)
    if with_qo:
        q_ref, kt_ref, v_ref, o_ref, gr_ref = outs
        q_ref[...] = (_dot(xb, w_ref[:, :QK_W]) * (DQK ** -0.5)).astype(BF16)
        o_ref[...] = _dot(xb, w_ref[:, O_START:G_START])
    else:
        kt_ref, v_ref, gr_ref = outs
    kg = lax.dot_general(wkgt_ref[...], xb, (((1,), (1,)), ((), ())), preferred_element_type=F32)
    kt_ref[...] = kg[:QK_W].astype(BF16)
    gr_ref[...] = kg[QK_W:]
    v_ref[...] = _dot(xb, w_ref[:, V_START:O_START]).astype(BF16)


def _front(x2d, norm_g, sh, sc, w_in, wkgt, rows_per_mod, with_qo):
    n = x2d.shape[0]
    tm = min(ROW_TILE, n)
    tiles_per_mod = rows_per_mod // tm
    row = lambda i: (i, 0)
    mod = lambda i: (i // tiles_per_mod, 0, 0)
    out_shape, out_specs = [], []
    if with_qo:
        out_shape.append(jax.ShapeDtypeStruct((n, QK_W), BF16))
        out_specs.append(pl.BlockSpec((tm, QK_W), row))
    out_shape.append(jax.ShapeDtypeStruct((QK_W, n), BF16))
    out_specs.append(pl.BlockSpec((QK_W, tm), lambda i: (0, i)))
    out_shape.append(jax.ShapeDtypeStruct((n, V_W), BF16))
    out_specs.append(pl.BlockSpec((tm, V_W), row))
    if with_qo:
        out_shape.append(jax.ShapeDtypeStruct((n, V_W), F32))
        out_specs.append(pl.BlockSpec((tm, V_W), row))
    out_shape.append(jax.ShapeDtypeStruct((GATE_W, n), F32))
    out_specs.append(pl.BlockSpec((GATE_W, tm), lambda i: (0, i)))
    return pl.pallas_call(
        functools.partial(_front_kernel, with_qo),
        grid=(n // tm,),
        in_specs=[
            pl.BlockSpec((tm, D_MODEL), row),
            _resident((1, D_MODEL)),
            pl.BlockSpec((1, 1, D_MODEL), mod),
            pl.BlockSpec((1, 1, D_MODEL), mod),
            _resident(w_in.shape), _resident(wkgt.shape),
        ],
        out_specs=out_specs,
        out_shape=out_shape,
        compiler_params=_params(1),
        name="mlstm_front_qo" if with_qo else "mlstm_front_ctx",
    )(x2d, norm_g.reshape(1, D_MODEL), sh, sc, w_in, wkgt)


def _softcap(g):
    return GATE_SOFTCAP * jnp.tanh(g * (1.0 / GATE_SOFTCAP))


def _log_sigmoid(g):
    return jnp.minimum(g, 0.0) - jnp.log1p(jnp.exp(-jnp.abs(g)))


PAIRS = HEADS // 2
BCOL_W = 16
GCOL_W = PAIRS * BCOL_W


def _gate_kernel(gr_ref, br_ref, col_ref, row_ref):
    t = gr_ref.shape[1]
    r = lax.broadcasted_iota(jnp.int32, (CHUNK, CHUNK), 0)
    c = lax.broadcasted_iota(jnp.int32, (CHUNK, CHUNK), 1)
    lower = (c <= r).astype(BF16)
    upper = (c >= r).astype(BF16)
    ones = jnp.ones((CHUNK, CHUNK), BF16)
    row_rhs = jnp.concatenate([upper, lower, ones], axis=1)
    col_lhs = jnp.concatenate([lower, upper], axis=1)
    src = lax.broadcasted_iota(jnp.int32, (GATE_W, GCOL_W), 0)
    dst = lax.broadcasted_iota(jnp.int32, (GATE_W, GCOL_W), 1)
    place = [((src % 8 < 4) & (dst == (src // 8) * BCOL_W + 4 * i + src % 8)).astype(BF16)
             for i in range(3)]
    m = lax.broadcasted_iota(jnp.int32, (GATE_W, CHUNK), 0) % 8
    fwd_rows = m < 2
    bwd_rows = (m >= 2) & (m < 4)
    nt = (((1,), (1,)), ((), ()))

    gr = _softcap(gr_ref[...] + br_ref[...])
    li = pltpu.roll(gr, GATE_W - 4, axis=0)
    lf3 = _split3(_log_sigmoid(gr))

    for k in range(t // CHUNK):
        sl = slice(k * CHUNK, (k + 1) * CHUNK)
        cum = jnp.zeros((CHUNK, GATE_W), F32)
        y = jnp.zeros((GATE_W, 3 * CHUNK), F32)
        for piece in lf3:
            x = piece[:, sl]
            zero = jnp.zeros_like(x)
            xd = jnp.concatenate([jnp.where(fwd_rows, x, zero), jnp.where(bwd_rows, x, zero)],
                                 axis=1)
            cum = cum + lax.dot_general(col_lhs, xd, nt, preferred_element_type=F32)
            y = y + _dot(x, row_rhs)
        out = jnp.zeros((CHUNK, GCOL_W), F32)
        for piece, sel in zip(_split3(cum * LOG2E), place):
            out = out + _dot(piece, sel)
        col_ref[sl, :] = out.astype(BF16)
        total = y[:, 2 * CHUNK:]
        e = li[:, sl] - jnp.where(fwd_rows, y[:, :CHUNK], y[:, CHUNK:2 * CHUNK])
        used = fwd_rows | bwd_rows
        row_ref[0, :, sl] = jnp.where(used, e * LOG2E, 0.0)
        row_ref[1, :, sl] = jnp.where(used, jnp.exp(total + e), 0.0)
        row_ref[2, :, sl] = jnp.where(used, jnp.exp(total), 0.0)


def _gate_prep(gates_raw, bias):
    n = gates_raw.shape[1]
    t = min(1024, n)
    return pl.pallas_call(
        _gate_kernel,
        grid=(n // t,),
        in_specs=[
            pl.BlockSpec((GATE_W, t), lambda i: (0, i)),
            _resident((GATE_W, 1)),
        ],
        out_specs=[
            pl.BlockSpec((t, GCOL_W), lambda i: (i, 0)),
            pl.BlockSpec((3, GATE_W, t), lambda i: (0, 0, i)),
        ],
        out_shape=[
            jax.ShapeDtypeStruct((n, GCOL_W), BF16),
            jax.ShapeDtypeStruct((3, GATE_W, n), F32),
        ],
        compiler_params=_params(1),
        name="mlstm_gate_prep",
    )(gates_raw, bias)


SCAN_SPLITS = 2
STATE_UNROLL = 8
OUTPUT_UNROLL = 8


def _scan_kernel(q_ref, kt_ref, v_ref, gcol_ref, grow_ref, ktc_ref, vc_ref, growc_ref,
                 h_ref, cst_ref, cs_ref):
    seq = q_ref.shape[0]
    nc = seq // CHUNK
    ncc = ktc_ref.shape[1] // CHUNK
    nc_part = nc // SCAN_SPLITS
    part = pl.program_id(2)
    r = lax.broadcasted_iota(jnp.int32, (CHUNK, CHUNK), 0)
    c = lax.broadcasted_iota(jnp.int32, (CHUNK, CHUNK), 1)
    masks = (r >= c, r <= c)
    left = c < DQK
    ones_blk = jnp.ones((CHUNK, DV), BF16)
    zeros_k = jnp.zeros((DQK, CHUNK), BF16)
    zeros_q = jnp.zeros((CHUNK, DQK), BF16)
    pair = pl.program_id(1)
    src = lax.broadcasted_iota(jnp.int32, (GCOL_W, 4 * CHUNK), 0)
    block = lax.broadcasted_iota(jnp.int32, (GCOL_W, 4 * CHUNK), 1) // CHUNK
    piece = src % BCOL_W
    sel = ((src // BCOL_W == pair) & (piece % 4 == block) & (piece < 12)).astype(BF16)

    def gate_row(ref, quantity, d, j, lanes):
        return ref[quantity, 2 * d + j:2 * d + j + 1, lanes]

    def state_increment(kt_j, v_j, w_row):
        ktw = (kt_j.astype(F32) * w_row).astype(BF16)
        return _dot(ktw, jnp.concatenate([v_j, ones_blk], axis=1))

    def decay2(a_row):
        return jnp.concatenate([a_row, a_row], axis=1)

    @pl.when(part == 0)
    def _():
        cst_ref[...] = jnp.zeros(cst_ref.shape, F32)
        for d in range(2):
            for step in range(ncc):
                k = step if d == 0 else ncc - 1 - step
                sl = slice(k * CHUNK, (k + 1) * CHUNK)
                ktp = ktc_ref[:, sl]
                vp = vc_ref[sl, :]
                for j in range(2):
                    inc = state_increment(ktp[j * DQK:(j + 1) * DQK], vp[:, j * DV:(j + 1) * DV],
                                          gate_row(growc_ref, 1, d, j, sl))
                    cst_ref[d, j] = cst_ref[d, j] * decay2(gate_row(growc_ref, 2, d, j, sl)) + inc

        def state_step(i, carry):
            for d in range(2):
                k = i if d == 0 else nc - 1 - i
                rows = pl.ds(pl.multiple_of(k * CHUNK, CHUNK), CHUNK)
                ktp = kt_ref[:, rows]
                vp = v_ref[rows, :]
                for j in range(2):
                    cst = cst_ref[d, j]
                    cs_ref[d, k, j * DQK:(j + 1) * DQK, :] = cst.astype(BF16)
                    inc = state_increment(ktp[j * DQK:(j + 1) * DQK], vp[:, j * DV:(j + 1) * DV],
                                          gate_row(grow_ref, 1, d, j, rows))
                    cst_ref[d, j] = cst * decay2(gate_row(grow_ref, 2, d, j, rows)) + inc
            return carry

        lax.fori_loop(0, nc, state_step, 0, unroll=STATE_UNROLL)

    def decay_stage(k):
        rows = pl.ds(pl.multiple_of(k * CHUNK, CHUNK), CHUNK)
        qp = q_ref[rows, :]
        ktp = kt_ref[:, rows]
        bb_all = _dot(gcol_ref[rows, :], sel)
        kbd = jnp.concatenate([jnp.concatenate([ktp[:DQK], zeros_k], axis=1),
                               jnp.concatenate([zeros_k, ktp[DQK:]], axis=1)], axis=0)
        s2 = _dot(qp, kbd)
        qf = qp.astype(F32)
        lhs = []
        for d in range(2):
            bb = [bb_all[:, (2 * d + j) * CHUNK:(2 * d + j + 1) * CHUNK] for j in range(2)]
            qs = (qf * jnp.exp2(jnp.where(left, bb[0], bb[1]))).astype(BF16)
            for j in range(2):
                e_row = gate_row(grow_ref, 0, d, j, rows)
                dm = jnp.exp2(jnp.where(masks[d], bb[j] + e_row, NEG_BIG))
                p = (s2[:, j * CHUNK:(j + 1) * CHUNK] * dm).astype(BF16)
                qs_j = [qs[:, :DQK], zeros_q] if j == 0 else [zeros_q, qs[:, DQK:]]
                lhs.append(jnp.concatenate([p] + qs_j, axis=1))
        return lhs

    def value_stage(k, i, lhs):
        rows = pl.ds(pl.multiple_of(k * CHUNK, CHUNK), CHUNK)
        out_rows = pl.ds(pl.multiple_of(i * CHUNK, CHUNK), CHUNK)
        vp = v_ref[rows, :]
        for j in range(2):
            v1 = jnp.concatenate([vp[:, j * DV:(j + 1) * DV], ones_blk], axis=1)
            h = None
            for d in range(2):
                out = _dot(lhs[2 * d + j], jnp.concatenate([v1, cs_ref[d, k]], axis=0))
                hd = out[:, :DV] / jnp.maximum(jnp.abs(out[:, DV:]), 1.0)
                h = hd if d == 0 else h + hd
            h_ref[out_rows, j * DV:(j + 1) * DV] = h

    def output_trip(t, carry):
        i0 = t * OUTPUT_UNROLL
        staged = [decay_stage(part * nc_part + i0 + g) for g in range(OUTPUT_UNROLL)]
        for g in range(OUTPUT_UNROLL):
            value_stage(part * nc_part + i0 + g, i0 + g, staged[g])
        return carry

    lax.fori_loop(0, nc_part // OUTPUT_UNROLL, output_trip, 0)


def _scan(q, kt, v, gcol, grow, ktc, vc, growc, bsz, seq, ctx_len):
    nc = seq // CHUNK
    return pl.pallas_call(
        _scan_kernel,
        grid=(bsz, PAIRS, SCAN_SPLITS),
        in_specs=[
            pl.BlockSpec((seq, 2 * DQK), lambda b, p, s: (b, p)),
            pl.BlockSpec((2 * DQK, seq), lambda b, p, s: (p, b)),
            pl.BlockSpec((seq, 2 * DV), lambda b, p, s: (b, p)),
            pl.BlockSpec((seq, GCOL_W), lambda b, p, s: (b, 0)),
            pl.BlockSpec((3, 8, seq), lambda b, p, s: (0, p, b)),
            pl.BlockSpec((2 * DQK, ctx_len), lambda b, p, s: (p, b)),
            pl.BlockSpec((ctx_len, 2 * DV), lambda b, p, s: (b, p)),
            pl.BlockSpec((3, 8, ctx_len), lambda b, p, s: (0, p, b)),
        ],
        out_specs=pl.BlockSpec((seq // SCAN_SPLITS, 2 * DV),
                               lambda b, p, s: (b * SCAN_SPLITS + s, p)),
        out_shape=jax.ShapeDtypeStruct((bsz * seq, V_W), F32),
        scratch_shapes=[pltpu.VMEM((2, 2, DQK, 2 * DV), F32),
                        pltpu.VMEM((2, nc, 2 * DQK, 2 * DV), BF16)],
        compiler_params=_params(3),
        name="mlstm_scan",
    )(q, kt, v, gcol, grow, ktc, vc, growc)


MLP_TILE = 1024
SUB_ROWS = 256


def _mlp_pipeline(n_sub, mixer, finish, n2_ref, sh2_ref, sc2_ref, g2_ref, w1_ref, w2_ref):
    def normed(x1):
        return (_rmsnorm(x1, n2_ref[...]) * (1.0 + sc2_ref[0]) + sh2_ref[0]).astype(BF16)

    def slab(u, k):
        cols = slice(k * FF_TILE, (k + 1) * FF_TILE)
        hid = jnp.maximum(_dot(u, w1_ref[0, :, cols]), 0.0)
        return _dot((hid * hid).astype(BF16), w2_ref[0, cols, :])

    def advance(gen):
        try:
            next(gen)
            return gen, None
        except StopIteration as done:
            return None, done.value

    gen, x1 = mixer(0), None
    while gen is not None:
        gen, x1 = advance(gen)
    u = normed(x1)
    for s in range(n_sub):
        gen = mixer(s + 1) if s + 1 < n_sub else None
        x1_next = u_next = None
        acc = slab(u, 0)
        for k in range(1, D_FF // FF_TILE):
            if gen is not None:
                gen, x1_next = advance(gen)
                if gen is None:
                    u_next = normed(x1_next)
            acc = acc + slab(u, k)
        assert gen is None, "mixer has more phases than hidden slabs to hide them under"
        finish(s, x1 + g2_ref[0] * acc)
        x1, u = x1_next, u_next


def _mlstm_post_kernel(h_ref, o_ref, x_ref, ng_ref, wout_ref, g1_ref, n2_ref, sh2_ref, sc2_ref,
                       g2_ref, w1_ref, w2_ref, out_ref):
    def mixer(s):
        rows = slice(s * SUB_ROWS, (s + 1) * SUB_ROWS)
        gate = jax.nn.sigmoid(o_ref[rows, :])
        parts = []
        for k in range(HEADS):
            hk = h_ref[rows, k * DV:(k + 1) * DV]
            parts.append(hk * lax.rsqrt(jnp.mean(hk * hk, axis=-1, keepdims=True) + NORM_EPS))
        hn = jnp.concatenate(parts, axis=1) * ng_ref[...]
        y = (gate * hn).astype(BF16)
        return x_ref[rows, :] + g1_ref[0] * _dot(y, wout_ref[...])
        yield

    def finish(s, x2):
        out_ref[s * SUB_ROWS:(s + 1) * SUB_ROWS, :] = x2

    _mlp_pipeline(x_ref.shape[0] // SUB_ROWS, mixer, finish, n2_ref, sh2_ref, sc2_ref, g2_ref,
                  w1_ref, w2_ref)


def _mlstm_post(h, o, x2d, norm_g, w_out, g1, n2, sh2, sc2, g2, w1, w2, layer, rows_per_mod):
    n = x2d.shape[0]
    tm = MLP_TILE
    tiles_per_mod = rows_per_mod // tm
    row = pl.BlockSpec((tm, D_MODEL), lambda i: (i, 0))
    mod = pl.BlockSpec((1, 1, D_MODEL), lambda i: (i // tiles_per_mod, 0, 0))
    return pl.pallas_call(
        _mlstm_post_kernel,
        grid=(n // tm,),
        in_specs=[row, row, row, _resident((1, V_W)), _resident(w_out.shape), mod,
                  _resident((1, D_MODEL)), mod, mod, mod, _layer_slab(w1.shape, layer), _layer_slab(w2.shape, layer)],
        out_specs=row,
        out_shape=jax.ShapeDtypeStruct((n, D_MODEL), F32),
        compiler_params=_params(1),
        name="mlstm_post_mlp",
    )(h, o, x2d, norm_g.reshape(1, V_W), w_out, g1, n2.reshape(1, D_MODEL), sh2, sc2, g2, w1, w2)


def _pool_constants():
    t = np.arange(SUB_ROWS)
    pos = t % GRID_W
    bands, inv = [], []
    for window in POOL_WINDOWS:
        lo = np.clip(pos - window // 2, 0, GRID_W)
        hi = np.clip(pos - window // 2 + window, 0, GRID_W)
        same_group = t[:, None] // GRID_W == t[None, :] // GRID_W
        bands.append(same_group & (pos[None, :] >= lo[:, None]) & (pos[None, :] < hi[:, None]))
        inv.append(np.broadcast_to((1.0 / (hi - lo))[:, None], (SUB_ROWS, POOL_GW)))
    return (jnp.asarray(np.stack(bands), dtype=BF16), jnp.asarray(np.stack(inv), dtype=F32))


def _pool_layer_kernel(x_ref, n1_ref, sh1_ref, sc1_ref, g1_ref, band_ref, inv_ref, pw_ref, ps_ref,
                       n2_ref, sh2_ref, sc2_ref, g2_ref, w1_ref, w2_ref, fg_ref, out_ref):
    def mixer(s):
        x = x_ref[s * SUB_ROWS:(s + 1) * SUB_ROWS, :]
        xn = _rmsnorm(x, n1_ref[...]) * (1.0 + sc1_ref[0]) + sh1_ref[0]
        hi = xn.astype(BF16)
        lo = (xn - hi.astype(F32)).astype(BF16)
        ps = []
        for gi in range(len(POOL_WINDOWS)):
            cols = slice(gi * POOL_GW, (gi + 1) * POOL_GW)
            band = band_ref[gi]
            total = _dot(band, hi[:, cols]) + _dot(band, lo[:, cols])
            ps.append((total * inv_ref[gi] - xn[:, cols]).astype(BF16))
        yield
        ys = [_dot(p, pw_ref[gi]) for gi, p in enumerate(ps)]
        return x + g1_ref[0] * (jnp.concatenate(ys, axis=1) * ps_ref[...])

    def finish(s, x2):
        out_ref[s * SUB_ROWS:(s + 1) * SUB_ROWS, :] = _rmsnorm(x2, fg_ref[...])

    _mlp_pipeline(x_ref.shape[0] // SUB_ROWS, mixer, finish, n2_ref, sh2_ref, sc2_ref, g2_ref,
                  w1_ref, w2_ref)


def _pool_layer(x2d, n1, sh1, sc1, g1, pool_w, pool_scale, n2, sh2, sc2, g2, w1, w2, layer,
                final_g, rows_per_mod):
    n = x2d.shape[0]
    tm = MLP_TILE
    tiles_per_mod = rows_per_mod // tm
    row = pl.BlockSpec((tm, D_MODEL), lambda i: (i, 0))
    mod = pl.BlockSpec((1, 1, D_MODEL), lambda i: (i // tiles_per_mod, 0, 0))
    vec = _resident((1, D_MODEL))
    bands, inv = _pool_constants()
    return pl.pallas_call(
        _pool_layer_kernel,
        grid=(n // tm,),
        in_specs=[row, vec, mod, mod, mod, _resident(bands.shape), _resident(inv.shape),
                  _resident(pool_w.shape), vec, vec, mod, mod, mod,
                  _layer_slab(w1.shape, layer), _layer_slab(w2.shape, layer), vec],
        out_specs=row,
        out_shape=jax.ShapeDtypeStruct((n, D_MODEL), F32),
        compiler_params=_params(1),
        name="pool_mlp_final",
    )(x2d, n1.reshape(1, D_MODEL), sh1, sc1, g1, bands, inv, pool_w,
      pool_scale.reshape(1, D_MODEL),
      n2.reshape(1, D_MODEL), sh2, sc2, g2, w1, w2, final_g.reshape(1, D_MODEL))


def kernel(x, c, ctx, c_ctx, ada_w, ada_b, norm1_g, norm2_g, mlstm_w_in, mlstm_gate_b,
           mlstm_norm_g, mlstm_w_out, pool_w, pool_scale, mlp_w1, mlp_w2, final_g):
    bsz, seq, dm = x.shape
    ctx_len = ctx.shape[1]
    assert dm == D_MODEL and seq % (2 * CHUNK) == 0 and ctx_len % CHUNK == 0
    assert seq % ROW_TILE == 0 and seq % MLP_TILE == 0 and bsz + 1 <= MOD_ROWS
    assert MLP_TILE % SUB_ROWS == 0 and SUB_ROWS % GRID_W == 0
    assert ada_w.shape[0] == 2 and mlstm_w_in.shape[0] == 1 and pool_w.shape[0] == 1

    cmat = jnp.concatenate(
        [c, c_ctx[None, :], jnp.zeros((MOD_ROWS - bsz - 1, dm), F32)], axis=0)
    mod = _modulation(cmat, ada_w, ada_b)

    def mod_rows(layer, which, rows):
        return mod[layer, rows, which * dm:(which + 1) * dm][:, None, :]

    lat = slice(0, bsz)
    cx = slice(bsz, bsz + 1)

    def pair_major(g):
        g4 = g.reshape(g.shape[:-1] + (4, PAIRS, 2))
        g4 = jnp.stack([g4[..., t, :, :] for t in (1, 3, 0, 2)], axis=-2)
        return g4.reshape(g.shape)

    w_in = mlstm_w_in[0].astype(BF16)
    wkgt = jnp.concatenate([mlstm_w_in[0, :, QK_W:V_START], pair_major(mlstm_w_in[0, :, G_START:])],
                           axis=1).T.astype(BF16)
    gate_b = pair_major(mlstm_gate_b[0].reshape(GATE_W)).reshape(GATE_W, 1)
    w1 = mlp_w1.astype(BF16)
    w2 = mlp_w2.astype(BF16)

    x2d = x.reshape(bsz * seq, dm)
    q, kt, v, o, gates = _front(
        x2d, norm1_g[0], mod_rows(0, 0, lat), mod_rows(0, 1, lat), w_in, wkgt, seq, True)
    ktc, vc, gates_c = _front(
        ctx.reshape(bsz * ctx_len, dm), norm1_g[0], mod_rows(0, 0, cx), mod_rows(0, 1, cx),
        w_in, wkgt, bsz * ctx_len, False)
    gcol, grow = _gate_prep(gates, gate_b)
    _, growc = _gate_prep(gates_c, gate_b)

    h = _scan(q, kt, v, gcol, grow, ktc, vc, growc, bsz, seq, ctx_len)
    x2d = _mlstm_post(
        h, o, x2d, mlstm_norm_g[0], mlstm_w_out[0].astype(BF16), mod_rows(0, 2, lat),
        norm2_g[0], mod_rows(0, 3, lat), mod_rows(0, 4, lat), mod_rows(0, 5, lat), w1, w2, 0, seq)

    out = _pool_layer(
        x2d, norm1_g[1], mod_rows(1, 0, lat), mod_rows(1, 1, lat), mod_rows(1, 2, lat),
        pool_w[0].astype(BF16), pool_scale[0], norm2_g[1], mod_rows(1, 3, lat),
        mod_rows(1, 4, lat), mod_rows(1, 5, lat), w1, w2, 1, final_g, seq)
    return out.reshape(bsz, seq, dm)
```

```python
import functools

import jax
import jax.numpy as jnp
import numpy as np
from jax import lax
from jax.experimental import pallas as pl
from jax.experimental.pallas import tpu as pltpu

F32 = jnp.float32
BF16 = jnp.bfloat16

D_MODEL = 1024
HEADS = 8
DV = 128
DQK = 64
QK_W = HEADS * DQK
V_W = HEADS * DV
GATE_W = 4 * HEADS
D_FF = 4 * D_MODEL
NORM_EPS = 1e-6
GATE_SOFTCAP = 15.0
GRID_W = 64
POOL_WINDOWS = (2, 4, 8, 16)
POOL_GW = D_MODEL // len(POOL_WINDOWS)

LANES = 128
CHUNK = LANES
MOD_ROWS = 8
VMEM_LIMIT = 62 * 1024 * 1024
ROW_TILE = 512
FF_TILE = 1024
NEG_BIG = -1e30
LOG2E = 1.4426950408889634


def _params(n_axes):
    return pltpu.CompilerParams(
        dimension_semantics=("arbitrary",) * n_axes, vmem_limit_bytes=VMEM_LIMIT)


def _resident(shape):
    return pl.BlockSpec(shape, lambda *_: (0,) * len(shape), pipeline_mode=pl.Buffered(1))


def _layer_slab(shape, layer):
    return pl.BlockSpec((1,) + tuple(shape[1:]), lambda *_: (layer,) + (0,) * (len(shape) - 1),
                        pipeline_mode=pl.Buffered(1))


def _rmsnorm(x, g):
    return x * lax.rsqrt(jnp.mean(x * x, axis=-1, keepdims=True) + NORM_EPS) * g


def _dot(a, b):
    return jnp.dot(a, b, preferred_element_type=F32)


def _split3(x):
    hi = x.astype(BF16)
    rest = x - hi.astype(F32)
    mid = rest.astype(BF16)
    return hi, mid, (rest - mid.astype(F32)).astype(BF16)


def _mod_kernel(c_ref, w_ref, b_ref, o_ref):
    c = c_ref[...]
    s = c * jax.nn.sigmoid(c)
    lhs = jnp.concatenate([p.astype(F32) for p in _split3(s)], axis=0).astype(BF16)
    acc = jnp.zeros((3 * MOD_ROWS, w_ref.shape[2]), F32)
    for piece in _split3(w_ref[0]):
        acc = acc + _dot(lhs, piece)
    o_ref[0] = acc[:MOD_ROWS] + acc[MOD_ROWS:2 * MOD_ROWS] + acc[2 * MOD_ROWS:] + b_ref[0]


def _modulation(cmat, ada_w, ada_b):
    depth, _, n = ada_w.shape
    tn = 1536
    return pl.pallas_call(
        _mod_kernel,
        grid=(depth, n // tn),
        in_specs=[
            pl.BlockSpec((MOD_ROWS, D_MODEL), lambda i, j: (0, 0)),
            pl.BlockSpec((1, D_MODEL, tn), lambda i, j: (i, 0, j)),
            pl.BlockSpec((1, 1, tn), lambda i, j: (i, 0, j)),
        ],
        out_specs=pl.BlockSpec((1, MOD_ROWS, tn), lambda i, j: (i, 0, j)),
        out_shape=jax.ShapeDtypeStruct((depth, MOD_ROWS, n), F32),
        compiler_params=_params(2),
        name="adaln_modulation",
    )(cmat, ada_w, ada_b.reshape(depth, 1, n))


V_START = 2 * QK_W
O_START = V_START + V_W
G_START = O_START + V_W


def _front_kernel(with_qo, x_ref, g_ref, sh_ref, sc_ref, w_ref, wkgt_ref, *outs):
    xn = _rmsnorm(x_ref[...], g_ref[...]) * (1.0 + sc_ref[0]) + sh_ref[0]
    xb = xn.astype(BF16)
    if with_qo:
        q_ref, kt_ref, v_ref, o_ref, gr_ref = outs
        q_ref[...] = (_dot(xb, w_ref[:, :QK_W]) * (DQK ** -0.5)).astype(BF16)
        o_ref[...] = _dot(xb, w_ref[:, O_START:G_START])
    else:
        kt_ref, v_ref, gr_ref = outs
    kg = lax.dot_general(wkgt_ref[...], xb, (((1,), (1,)), ((), ())), preferred_element_type=F32)
    kt_ref[...] = kg[:QK_W].astype(BF16)
    gr_ref[...] = kg[QK_W:]
    v_ref[...] = _dot(xb, w_ref[:, V_START:O_START]).astype(BF16)


def _front(x2d, norm_g, sh, sc, w_in, wkgt, rows_per_mod, with_qo):
    n = x2d.shape[0]
    tm = min(ROW_TILE, n)
    tiles_per_mod = rows_per_mod // tm
    row = lambda i: (i, 0)
    mod = lambda i: (i // tiles_per_mod, 0, 0)
    out_shape, out_specs = [], []
    if with_qo:
        out_shape.append(jax.ShapeDtypeStruct((n, QK_W), BF16))
        out_specs.append(pl.BlockSpec((tm, QK_W), row))
    out_shape.append(jax.ShapeDtypeStruct((QK_W, n), BF16))
    out_specs.append(pl.BlockSpec((QK_W, tm), lambda i: (0, i)))
    out_shape.append(jax.ShapeDtypeStruct((n, V_W), BF16))
    out_specs.append(pl.BlockSpec((tm, V_W), row))
    if with_qo:
        out_shape.append(jax.ShapeDtypeStruct((n, V_W), F32))
        out_specs.append(pl.BlockSpec((tm, V_W), row))
    out_shape.append(jax.ShapeDtypeStruct((GATE_W, n), F32))
    out_specs.append(pl.BlockSpec((GATE_W, tm), lambda i: (0, i)))
    return pl.pallas_call(
        functools.partial(_front_kernel, with_qo),
        grid=(n // tm,),
        in_specs=[
            pl.BlockSpec((tm, D_MODEL), row),
            _resident((1, D_MODEL)),
            pl.BlockSpec((1, 1, D_MODEL), mod),
            pl.BlockSpec((1, 1, D_MODEL), mod),
            _resident(w_in.shape), _resident(wkgt.shape),
        ],
        out_specs=out_specs,
        out_shape=out_shape,
        compiler_params=_params(1),
        name="mlstm_front_qo" if with_qo else "mlstm_front_ctx",
    )(x2d, norm_g.reshape(1, D_MODEL), sh, sc, w_in, wkgt)


def _softcap(g):
    return GATE_SOFTCAP * jnp.tanh(g * (1.0 / GATE_SOFTCAP))


def _log_sigmoid(g):
    return jnp.minimum(g, 0.0) - jnp.log1p(jnp.exp(-jnp.abs(g)))


PAIRS = HEADS // 2
BCOL_W = 16
GCOL_W = PAIRS * BCOL_W


def _gate_kernel(gr_ref, br_ref, col_ref, row_ref):
    t = gr_ref.shape[1]
    r = lax.broadcasted_iota(jnp.int32, (CHUNK, CHUNK), 0)
    c = lax.broadcasted_iota(jnp.int32, (CHUNK, CHUNK), 1)
    lower = (c <= r).astype(BF16)
    upper = (c >= r).astype(BF16)
    ones = jnp.ones((CHUNK, CHUNK), BF16)
    row_rhs = jnp.concatenate([upper, lower, ones], axis=1)
    col_lhs = jnp.concatenate([lower, upper], axis=1)
    src = lax.broadcasted_iota(jnp.int32, (GATE_W, GCOL_W), 0)
    dst = lax.broadcasted_iota(jnp.int32, (GATE_W, GCOL_W), 1)
    place = [((src % 8 < 4) & (dst == (src // 8) * BCOL_W + 4 * i + src % 8)).astype(BF16)
             for i in range(3)]
    m = lax.broadcasted_iota(jnp.int32, (GATE_W, CHUNK), 0) % 8
    fwd_rows = m < 2
    bwd_rows = (m >= 2) & (m < 4)
    nt = (((1,), (1,)), ((), ()))

    gr = _softcap(gr_ref[...] + br_ref[...])
    li = pltpu.roll(gr, GATE_W - 4, axis=0)
    lf3 = _split3(_log_sigmoid(gr))

    chunks = [slice(k * CHUNK, (k + 1) * CHUNK) for k in range(t // CHUNK)]
    sums = []
    for sl in chunks:
        cum = jnp.zeros((CHUNK, GATE_W), F32)
        y = jnp.zeros((GATE_W, 3 * CHUNK), F32)
        for piece in lf3:
            x = piece[:, sl]
            zero = jnp.zeros_like(x)
            xd = jnp.concatenate([jnp.where(fwd_rows, x, zero), jnp.where(bwd_rows, x, zero)],
                                 axis=1)
            cum = cum + lax.dot_general(col_lhs, xd, nt, preferred_element_type=F32)
            y = y + _dot(x, row_rhs)
        sums.append((cum, y))
    for sl, (cum, y) in zip(chunks, sums):
        out = jnp.zeros((CHUNK, GCOL_W), F32)
        for piece, sel in zip(_split3(cum * LOG2E), place):
            out = out + _dot(piece, sel)
        col_ref[sl, :] = out.astype(BF16)
        total = y[:, 2 * CHUNK:]
        e = li[:, sl] - jnp.where(fwd_rows, y[:, :CHUNK], y[:, CHUNK:2 * CHUNK])
        used = fwd_rows | bwd_rows
        row_ref[0, :, sl] = jnp.where(used, e * LOG2E, 0.0)
        row_ref[1, :, sl] = jnp.where(used, jnp.exp(total + e), 0.0)
        row_ref[2, :, sl] = jnp.where(used, jnp.exp(total), 0.0)


def _gate_prep(gates_raw, bias):
    n = gates_raw.shape[1]
    t = min(1024, n)
    return pl.pallas_call(
        _gate_kernel,
        grid=(n // t,),
        in_specs=[
            pl.BlockSpec((GATE_W, t), lambda i: (0, i)),
            _resident((GATE_W, 1)),
        ],
        out_specs=[
            pl.BlockSpec((t, GCOL_W), lambda i: (i, 0)),
            pl.BlockSpec((3, GATE_W, t), lambda i: (0, 0, i)),
        ],
        out_shape=[
            jax.ShapeDtypeStruct((n, GCOL_W), BF16),
            jax.ShapeDtypeStruct((3, GATE_W, n), F32),
        ],
        compiler_params=_params(1),
        name="mlstm_gate_prep",
    )(gates_raw, bias)


SCAN_SPLITS = 2
STATE_UNROLL = 8
OUTPUT_UNROLL = 8


def _scan_kernel(q_ref, kt_ref, v_ref, gcol_ref, grow_ref, ktc_ref, vc_ref, growc_ref,
                 h_ref, cst_ref, cs_ref):
    seq = q_ref.shape[0]
    nc = seq // CHUNK
    ncc = ktc_ref.shape[1] // CHUNK
    nc_part = nc // SCAN_SPLITS
    part = pl.program_id(2)
    r = lax.broadcasted_iota(jnp.int32, (CHUNK, CHUNK), 0)
    c = lax.broadcasted_iota(jnp.int32, (CHUNK, CHUNK), 1)
    masks = (r >= c, r <= c)
    left = c < DQK
    ones_blk = jnp.ones((CHUNK, DV), BF16)
    zeros_k = jnp.zeros((DQK, CHUNK), BF16)
    zeros_q = jnp.zeros((CHUNK, DQK), BF16)
    pair = pl.program_id(1)
    src = lax.broadcasted_iota(jnp.int32, (GCOL_W, 4 * CHUNK), 0)
    block = lax.broadcasted_iota(jnp.int32, (GCOL_W, 4 * CHUNK), 1) // CHUNK
    piece = src % BCOL_W
    sel = ((src // BCOL_W == pair) & (piece % 4 == block) & (piece < 12)).astype(BF16)

    def gate_row(ref, quantity, d, j, lanes):
        return ref[quantity, 2 * d + j:2 * d + j + 1, lanes]

    def state_increment(kt_j, v_j, w_row):
        ktw = (kt_j.astype(F32) * w_row).astype(BF16)
        return _dot(ktw, jnp.concatenate([v_j, ones_blk], axis=1))

    def decay2(a_row):
        return jnp.concatenate([a_row, a_row], axis=1)

    @pl.when(part == 0)
    def _():
        cst_ref[...] = jnp.zeros(cst_ref.shape, F32)
        for d in range(2):
            for step in range(ncc):
                k = step if d == 0 else ncc - 1 - step
                sl = slice(k * CHUNK, (k + 1) * CHUNK)
                ktp = ktc_ref[:, sl]
                vp = vc_ref[sl, :]
                for j in range(2):
                    inc = state_increment(ktp[j * DQK:(j + 1) * DQK], vp[:, j * DV:(j + 1) * DV],
                                          gate_row(growc_ref, 1, d, j, sl))
                    cst_ref[d, j] = cst_ref[d, j] * decay2(gate_row(growc_ref, 2, d, j, sl)) + inc

        def state_step(i, carry):
            for d in range(2):
                k = i if d == 0 else nc - 1 - i
                rows = pl.ds(pl.multiple_of(k * CHUNK, CHUNK), CHUNK)
                ktp = kt_ref[:, rows]
                vp = v_ref[rows, :]
                for j in range(2):
                    cst = cst_ref[d, j]
                    cs_ref[d, k, j * DQK:(j + 1) * DQK, :] = cst.astype(BF16)
                    inc = state_increment(ktp[j * DQK:(j + 1) * DQK], vp[:, j * DV:(j + 1) * DV],
                                          gate_row(grow_ref, 1, d, j, rows))
                    cst_ref[d, j] = cst * decay2(gate_row(grow_ref, 2, d, j, rows)) + inc
            return carry

        lax.fori_loop(0, nc, state_step, 0, unroll=STATE_UNROLL)

    def decay_stage(k):
        rows = pl.ds(pl.multiple_of(k * CHUNK, CHUNK), CHUNK)
        qp = q_ref[rows, :]
        ktp = kt_ref[:, rows]
        bb_all = _dot(gcol_ref[rows, :], sel)
        kbd = jnp.concatenate([jnp.concatenate([ktp[:DQK], zeros_k], axis=1),
                               jnp.concatenate([zeros_k, ktp[DQK:]], axis=1)], axis=0)
        s2 = _dot(qp, kbd)
        qf = qp.astype(F32)
        lhs = []
        for d in range(2):
            bb = [bb_all[:, (2 * d + j) * CHUNK:(2 * d + j + 1) * CHUNK] for j in range(2)]
            qs = (qf * jnp.exp2(jnp.where(left, bb[0], bb[1]))).astype(BF16)
            for j in range(2):
                e_row = gate_row(grow_ref, 0, d, j, rows)
                dm = jnp.exp2(jnp.where(masks[d], bb[j] + e_row, NEG_BIG))
                p = (s2[:, j * CHUNK:(j + 1) * CHUNK] * dm).astype(BF16)
                qs_j = [qs[:, :DQK], zeros_q] if j == 0 else [zeros_q, qs[:, DQK:]]
                lhs.append(jnp.concatenate([p] + qs_j, axis=1))
        return lhs

    def value_stage(k, i, lhs):
        rows = pl.ds(pl.multiple_of(k * CHUNK, CHUNK), CHUNK)
        out_rows = pl.ds(pl.multiple_of(i * CHUNK, CHUNK), CHUNK)
        vp = v_ref[rows, :]
        for j in range(2):
            v1 = jnp.concatenate([vp[:, j * DV:(j + 1) * DV], ones_blk], axis=1)
            h = None
            for d in range(2):
                out = _dot(lhs[2 * d + j], jnp.concatenate([v1, cs_ref[d, k]], axis=0))
                hd = out[:, :DV] / jnp.maximum(jnp.abs(out[:, DV:]), 1.0)
                h = hd if d == 0 else h + hd
            h_ref[out_rows, j * DV:(j + 1) * DV] = h

    def output_trip(t, carry):
        i0 = t * OUTPUT_UNROLL
        staged = [decay_stage(part * nc_part + i0 + g) for g in range(OUTPUT_UNROLL)]
        for g in range(OUTPUT_UNROLL):
            value_stage(part * nc_part + i0 + g, i0 + g, staged[g])
        return carry

    lax.fori_loop(0, nc_part // OUTPUT_UNROLL, output_trip, 0)


def _scan(q, kt, v, gcol, grow, ktc, vc, growc, bsz, seq, ctx_len):
    nc = seq // CHUNK
    return pl.pallas_call(
        _scan_kernel,
        grid=(bsz, PAIRS, SCAN_SPLITS),
        in_specs=[
            pl.BlockSpec((seq, 2 * DQK), lambda b, p, s: (b, p)),
            pl.BlockSpec((2 * DQK, seq), lambda b, p, s: (p, b)),
            pl.BlockSpec((seq, 2 * DV), lambda b, p, s: (b, p)),
            pl.BlockSpec((seq, GCOL_W), lambda b, p, s: (b, 0)),
            pl.BlockSpec((3, 8, seq), lambda b, p, s: (0, p, b)),
            pl.BlockSpec((2 * DQK, ctx_len), lambda b, p, s: (p, b)),
            pl.BlockSpec((ctx_len, 2 * DV), lambda b, p, s: (b, p)),
            pl.BlockSpec((3, 8, ctx_len), lambda b, p, s: (0, p, b)),
        ],
        out_specs=pl.BlockSpec((seq // SCAN_SPLITS, 2 * DV),
                               lambda b, p, s: (b * SCAN_SPLITS + s, p)),
        out_shape=jax.ShapeDtypeStruct((bsz * seq, V_W), F32),
        scratch_shapes=[pltpu.VMEM((2, 2, DQK, 2 * DV), F32),
                        pltpu.VMEM((2, nc, 2 * DQK, 2 * DV), BF16)],
        compiler_params=_params(3),
        name="mlstm_scan",
    )(q, kt, v, gcol, grow, ktc, vc, growc)


MLP_TILE = 1024
POST_TILE = 512
SUB_ROWS = 256


def _mlp_pipeline(n_sub, mixer, finish, n2_ref, sh2_ref, sc2_ref, g2_ref, w1_ref, w2_ref):
    def normed(x1):
        return (_rmsnorm(x1, n2_ref[...]) * (1.0 + sc2_ref[0]) + sh2_ref[0]).astype(BF16)

    def slab(u, k):
        cols = slice(k * FF_TILE, (k + 1) * FF_TILE)
        hid = jnp.maximum(_dot(u, w1_ref[0, :, cols].astype(BF16)), 0.0)
        return _dot((hid * hid).astype(BF16), w2_ref[0, cols, :].astype(BF16))

    def advance(gen):
        try:
            next(gen)
            return gen, None
        except StopIteration as done:
            return None, done.value

    gen, x1 = mixer(0), None
    while gen is not None:
        gen, x1 = advance(gen)
    u = normed(x1)
    for s in range(n_sub):
        gen = mixer(s + 1) if s + 1 < n_sub else None
        x1_next = u_next = None
        acc = slab(u, 0)
        for k in range(1, D_FF // FF_TILE):
            if gen is not None:
                gen, x1_next = advance(gen)
                if gen is None:
                    u_next = normed(x1_next)
            acc = acc + slab(u, k)
        assert gen is None, "mixer has more phases than hidden slabs to hide them under"
        finish(s, x1 + g2_ref[0] * acc)
        x1, u = x1_next, u_next


def _mlstm_post_kernel(h_ref, o_ref, x_ref, ng_ref, wout_ref, g1_ref, n2_ref, sh2_ref, sc2_ref,
                       g2_ref, w1_ref, w2_ref, out_ref):
    def mixer(s):
        rows = slice(s * SUB_ROWS, (s + 1) * SUB_ROWS)
        gate = jax.nn.sigmoid(o_ref[rows, :])
        parts = []
        for k in range(HEADS):
            hk = h_ref[rows, k * DV:(k + 1) * DV]
            parts.append(hk * lax.rsqrt(jnp.mean(hk * hk, axis=-1, keepdims=True) + NORM_EPS))
        hn = jnp.concatenate(parts, axis=1) * ng_ref[...]
        y = (gate * hn).astype(BF16)
        return x_ref[rows, :] + g1_ref[0] * _dot(y, wout_ref[...])
        yield

    def finish(s, x2):
        out_ref[s * SUB_ROWS:(s + 1) * SUB_ROWS, :] = x2

    _mlp_pipeline(x_ref.shape[0] // SUB_ROWS, mixer, finish, n2_ref, sh2_ref, sc2_ref, g2_ref,
                  w1_ref, w2_ref)


def _mlstm_post(h, o, x2d, norm_g, w_out, g1, n2, sh2, sc2, g2, w1, w2, layer, rows_per_mod):
    n = x2d.shape[0]
    tm = POST_TILE
    tiles_per_mod = rows_per_mod // tm
    row = pl.BlockSpec((tm, D_MODEL), lambda i: (i, 0))
    mod = pl.BlockSpec((1, 1, D_MODEL), lambda i: (i // tiles_per_mod, 0, 0))
    return pl.pallas_call(
        _mlstm_post_kernel,
        grid=(n // tm,),
        in_specs=[row, row, row, _resident((1, V_W)), _resident(w_out.shape), mod,
                  _resident((1, D_MODEL)), mod, mod, mod, _layer_slab(w1.shape, layer), _layer_slab(w2.shape, layer)],
        out_specs=row,
        out_shape=jax.ShapeDtypeStruct((n, D_MODEL), F32),
        compiler_params=_params(1),
        name="mlstm_post_mlp",
    )(h, o, x2d, norm_g.reshape(1, V_W), w_out, g1, n2.reshape(1, D_MODEL), sh2, sc2, g2, w1, w2)


def _pool_constants():
    t = np.arange(SUB_ROWS)
    pos = t % GRID_W
    bands, inv = [], []
    for window in POOL_WINDOWS:
        lo = np.clip(pos - window // 2, 0, GRID_W)
        hi = np.clip(pos - window // 2 + window, 0, GRID_W)
        same_group = t[:, None] // GRID_W == t[None, :] // GRID_W
        bands.append(same_group & (pos[None, :] >= lo[:, None]) & (pos[None, :] < hi[:, None]))
        inv.append(np.broadcast_to((1.0 / (hi - lo))[:, None], (SUB_ROWS, POOL_GW)))
    return (jnp.asarray(np.stack(bands), dtype=BF16), jnp.asarray(np.stack(inv), dtype=F32))


def _pool_layer_kernel(x_ref, n1_ref, sh1_ref, sc1_ref, g1_ref, band_ref, inv_ref, pw_ref, ps_ref,
                       n2_ref, sh2_ref, sc2_ref, g2_ref, w1_ref, w2_ref, fg_ref, out_ref):
    def mixer(s):
        x = x_ref[s * SUB_ROWS:(s + 1) * SUB_ROWS, :]
        xn = _rmsnorm(x, n1_ref[...]) * (1.0 + sc1_ref[0]) + sh1_ref[0]
        hi = xn.astype(BF16)
        lo = (xn - hi.astype(F32)).astype(BF16)
        ps = []
        for gi in range(len(POOL_WINDOWS)):
            cols = slice(gi * POOL_GW, (gi + 1) * POOL_GW)
            band = band_ref[gi]
            total = _dot(band, hi[:, cols]) + _dot(band, lo[:, cols])
            ps.append((total * inv_ref[gi] - xn[:, cols]).astype(BF16))
        yield
        ys = [_dot(p, pw_ref[gi]) for gi, p in enumerate(ps)]
        return x + g1_ref[0] * (jnp.concatenate(ys, axis=1) * ps_ref[...])

    def finish(s, x2):
        out_ref[s * SUB_ROWS:(s + 1) * SUB_ROWS, :] = _rmsnorm(x2, fg_ref[...])

    _mlp_pipeline(x_ref.shape[0] // SUB_ROWS, mixer, finish, n2_ref, sh2_ref, sc2_ref, g2_ref,
                  w1_ref, w2_ref)


def _pool_layer(x2d, n1, sh1, sc1, g1, pool_w, pool_scale, n2, sh2, sc2, g2, w1, w2, layer,
                final_g, rows_per_mod):
    n = x2d.shape[0]
    tm = MLP_TILE
    tiles_per_mod = rows_per_mod // tm
    row = pl.BlockSpec((tm, D_MODEL), lambda i: (i, 0))
    mod = pl.BlockSpec((1, 1, D_MODEL), lambda i: (i // tiles_per_mod, 0, 0))
    vec = _resident((1, D_MODEL))
    bands, inv = _pool_constants()
    return pl.pallas_call(
        _pool_layer_kernel,
        grid=(n // tm,),
        in_specs=[row, vec, mod, mod, mod, _resident(bands.shape), _resident(inv.shape),
                  _resident(pool_w.shape), vec, vec, mod, mod, mod,
                  _layer_slab(w1.shape, layer), _layer_slab(w2.shape, layer), vec],
        out_specs=row,
        out_shape=jax.ShapeDtypeStruct((n, D_MODEL), F32),
        compiler_params=_params(1),
        name="pool_mlp_final",
    )(x2d, n1.reshape(1, D_MODEL), sh1, sc1, g1, bands, inv, pool_w,
      pool_scale.reshape(1, D_MODEL),
      n2.reshape(1, D_MODEL), sh2, sc2, g2, w1, w2, final_g.reshape(1, D_MODEL))


def kernel(x, c, ctx, c_ctx, ada_w, ada_b, norm1_g, norm2_g, mlstm_w_in, mlstm_gate_b,
           mlstm_norm_g, mlstm_w_out, pool_w, pool_scale, mlp_w1, mlp_w2, final_g):
    bsz, seq, dm = x.shape
    ctx_len = ctx.shape[1]
    assert dm == D_MODEL and seq % (2 * CHUNK) == 0 and ctx_len % CHUNK == 0
    assert seq % ROW_TILE == 0 and seq % MLP_TILE == 0 and bsz + 1 <= MOD_ROWS
    assert MLP_TILE % SUB_ROWS == 0 and SUB_ROWS % GRID_W == 0
    assert ada_w.shape[0] == 2 and mlstm_w_in.shape[0] == 1 and pool_w.shape[0] == 1

    cmat = jnp.concatenate(
        [c, c_ctx[None, :], jnp.zeros((MOD_ROWS - bsz - 1, dm), F32)], axis=0)
    mod = _modulation(cmat, ada_w, ada_b)

    def mod_rows(layer, which, rows):
        return mod[layer, rows, which * dm:(which + 1) * dm][:, None, :]

    lat = slice(0, bsz)
    cx = slice(bsz, bsz + 1)

    def pair_major(g):
        g4 = g.reshape(g.shape[:-1] + (4, PAIRS, 2))
        g4 = jnp.stack([g4[..., t, :, :] for t in (1, 3, 0, 2)], axis=-2)
        return g4.reshape(g.shape)

    w_in = mlstm_w_in[0].astype(BF16)
    wkgt = jnp.concatenate([mlstm_w_in[0, :, QK_W:V_START], pair_major(mlstm_w_in[0, :, G_START:])],
                           axis=1).T.astype(BF16)
    gate_b = pair_major(mlstm_gate_b[0].reshape(GATE_W)).reshape(GATE_W, 1)

    x2d = x.reshape(bsz * seq, dm)
    q, kt, v, o, gates = _front(
        x2d, norm1_g[0], mod_rows(0, 0, lat), mod_rows(0, 1, lat), w_in, wkgt, seq, True)
    ktc, vc, gates_c = _front(
        ctx.reshape(bsz * ctx_len, dm), norm1_g[0], mod_rows(0, 0, cx), mod_rows(0, 1, cx),
        w_in, wkgt, bsz * ctx_len, False)
    gcol, grow = _gate_prep(gates, gate_b)
    _, growc = _gate_prep(gates_c, gate_b)

    h = _scan(q, kt, v, gcol, grow, ktc, vc, growc, bsz, seq, ctx_len)
    x2d = _mlstm_post(
        h, o, x2d, mlstm_norm_g[0], mlstm_w_out[0].astype(BF16), mod_rows(0, 2, lat),
        norm2_g[0], mod_rows(0, 3, lat), mod_rows(0, 4, lat), mod_rows(0, 5, lat),
        mlp_w1, mlp_w2, 0, seq)

    out = _pool_layer(
        x2d, norm1_g[1], mod_rows(1, 0, lat), mod_rows(1, 1, lat), mod_rows(1, 2, lat),
        pool_w[0].astype(BF16), pool_scale[0], norm2_g[1], mod_rows(1, 3, lat),
        mod_rows(1, 4, lat), mod_rows(1, 5, lat), mlp_w1, mlp_w2, 1, final_g, seq)
    return out.reshape(bsz, seq, dm)
```

```python
import functools

import jax
import jax.numpy as jnp
import numpy as np
from jax import lax
from jax.experimental import pallas as pl
from jax.experimental.pallas import tpu as pltpu

F32 = jnp.float32
BF16 = jnp.bfloat16

D_MODEL = 1024
HEADS = 8
DV = 128
DQK = 64
QK_W = HEADS * DQK
V_W = HEADS * DV
GATE_W = 4 * HEADS
D_FF = 4 * D_MODEL
NORM_EPS = 1e-6
GATE_SOFTCAP = 15.0
GRID_W = 64
POOL_WINDOWS = (2, 4, 8, 16)
POOL_GW = D_MODEL // len(POOL_WINDOWS)

LANES = 128
CHUNK = LANES
MOD_ROWS = 8
VMEM_LIMIT = 62 * 1024 * 1024
ROW_TILE = 512
FF_TILE = 1024
NEG_BIG = -1e30
LOG2E = 1.4426950408889634


def _params(n_axes):
    return pltpu.CompilerParams(
        dimension_semantics=("arbitrary",) * n_axes, vmem_limit_bytes=VMEM_LIMIT)


def _resident(shape):
    return pl.BlockSpec(shape, lambda *_: (0,) * len(shape), pipeline_mode=pl.Buffered(1))


def _layer_slab(shape, layer):
    return pl.BlockSpec((1,) + tuple(shape[1:]), lambda *_: (layer,) + (0,) * (len(shape) - 1),
                        pipeline_mode=pl.Buffered(1))


def _rmsnorm(x, g):
    return x * lax.rsqrt(jnp.mean(x * x, axis=-1, keepdims=True) + NORM_EPS) * g


def _dot(a, b):
    return jnp.dot(a, b, preferred_element_type=F32)


def _split3(x):
    hi = x.astype(BF16)
    rest = x - hi.astype(F32)
    mid = rest.astype(BF16)
    return hi, mid, (rest - mid.astype(F32)).astype(BF16)


def _mod_kernel(c_ref, w_ref, b_ref, o_ref):
    c = c_ref[...]
    s = c * jax.nn.sigmoid(c)
    lhs = jnp.concatenate([p.astype(F32) for p in _split3(s)], axis=0).astype(BF16)
    acc = jnp.zeros((3 * MOD_ROWS, w_ref.shape[2]), F32)
    for piece in _split3(w_ref[0]):
        acc = acc + _dot(lhs, piece)
    o_ref[0] = acc[:MOD_ROWS] + acc[MOD_ROWS:2 * MOD_ROWS] + acc[2 * MOD_ROWS:] + b_ref[0]


def _modulation(cmat, ada_w, ada_b):
    depth, _, n = ada_w.shape
    tn = 1536
    return pl.pallas_call(
        _mod_kernel,
        grid=(depth, n // tn),
        in_specs=[
            pl.BlockSpec((MOD_ROWS, D_MODEL), lambda i, j: (0, 0)),
            pl.BlockSpec((1, D_MODEL, tn), lambda i, j: (i, 0, j)),
            pl.BlockSpec((1, 1, tn), lambda i, j: (i, 0, j)),
        ],
        out_specs=pl.BlockSpec((1, MOD_ROWS, tn), lambda i, j: (i, 0, j)),
        out_shape=jax.ShapeDtypeStruct((depth, MOD_ROWS, n), F32),
        compiler_params=_params(2),
        name="adaln_modulation",
    )(cmat, ada_w, ada_b.reshape(depth, 1, n))


V_START = 2 * QK_W
O_START = V_START + V_W
G_START = O_START + V_W


KG_ROW = QK_W + 2 * V_W


def _front_kernel(with_qo, x_ref, g_ref, sh_ref, sc_ref, wt32_ref, *rest):
    *outs, w_ref = rest
    nt = (((1,), (1,)), ((), ()))

    @pl.when(pl.program_id(0) == 0)
    def _():
        w_ref[:QK_W, :] = wt32_ref[:QK_W, :].astype(BF16)
        w_ref[QK_W:KG_ROW, :] = wt32_ref[V_START:G_START, :].astype(BF16)
        w_ref[KG_ROW:KG_ROW + QK_W, :] = wt32_ref[QK_W:V_START, :].astype(BF16)
        new = lax.broadcasted_iota(jnp.int32, (GATE_W, GATE_W), 0)
        old = lax.broadcasted_iota(jnp.int32, (GATE_W, GATE_W), 1)
        kind = new % 8 // 2
        gate_type = jnp.where(kind < 2, 2 * kind + 1, 2 * (kind - 2))
        perm = (old == gate_type * HEADS + 2 * (new // 8) + new % 2).astype(BF16)
        w_ref[KG_ROW + QK_W:, :] = _dot(perm, wt32_ref[G_START:, :].astype(BF16)).astype(BF16)

    xn = _rmsnorm(x_ref[...], g_ref[...]) * (1.0 + sc_ref[0]) + sh_ref[0]
    xb = xn.astype(BF16)

    def token_major(rows):
        return lax.dot_general(xb, w_ref[rows, :], nt, preferred_element_type=F32)

    if with_qo:
        q_ref, kt_ref, v_ref, o_ref, gr_ref = outs
        q_ref[...] = (token_major(slice(0, QK_W)) * (DQK ** -0.5)).astype(BF16)
        o_ref[...] = token_major(slice(QK_W + V_W, KG_ROW))
    else:
        kt_ref, v_ref, gr_ref = outs
    kg = lax.dot_general(w_ref[KG_ROW:, :], xb, nt, preferred_element_type=F32)
    kt_ref[...] = kg[:QK_W].astype(BF16)
    gr_ref[...] = kg[QK_W:]
    v_ref[...] = token_major(slice(QK_W, QK_W + V_W)).astype(BF16)


def _front(x2d, norm_g, sh, sc, w_in_t, rows_per_mod, with_qo):
    n = x2d.shape[0]
    tm = min(ROW_TILE, n)
    tiles_per_mod = rows_per_mod // tm
    row = lambda i: (i, 0)
    mod = lambda i: (i // tiles_per_mod, 0, 0)
    out_shape, out_specs = [], []
    if with_qo:
        out_shape.append(jax.ShapeDtypeStruct((n, QK_W), BF16))
        out_specs.append(pl.BlockSpec((tm, QK_W), row))
    out_shape.append(jax.ShapeDtypeStruct((QK_W, n), BF16))
    out_specs.append(pl.BlockSpec((QK_W, tm), lambda i: (0, i)))
    out_shape.append(jax.ShapeDtypeStruct((n, V_W), BF16))
    out_specs.append(pl.BlockSpec((tm, V_W), row))
    if with_qo:
        out_shape.append(jax.ShapeDtypeStruct((n, V_W), F32))
        out_specs.append(pl.BlockSpec((tm, V_W), row))
    out_shape.append(jax.ShapeDtypeStruct((GATE_W, n), F32))
    out_specs.append(pl.BlockSpec((GATE_W, tm), lambda i: (0, i)))
    return pl.pallas_call(
        functools.partial(_front_kernel, with_qo),
        grid=(n // tm,),
        in_specs=[
            pl.BlockSpec((tm, D_MODEL), row),
            _resident((1, D_MODEL)),
            pl.BlockSpec((1, 1, D_MODEL), mod),
            pl.BlockSpec((1, 1, D_MODEL), mod),
            _resident(w_in_t.shape),
        ],
        out_specs=out_specs,
        out_shape=out_shape,
        scratch_shapes=[pltpu.VMEM(w_in_t.shape, BF16)],
        compiler_params=_params(1),
        name="mlstm_front_qo" if with_qo else "mlstm_front_ctx",
    )(x2d, norm_g.reshape(1, D_MODEL), sh, sc, w_in_t)


def _softcap(g):
    return GATE_SOFTCAP * jnp.tanh(g * (1.0 / GATE_SOFTCAP))


def _log_sigmoid(g):
    return jnp.minimum(g, 0.0) - jnp.log1p(jnp.exp(-jnp.abs(g)))


PAIRS = HEADS // 2
BCOL_W = 16
GCOL_W = PAIRS * BCOL_W


def _gate_kernel(gr_ref, br_ref, col_ref, row_ref):
    t = gr_ref.shape[1]
    r = lax.broadcasted_iota(jnp.int32, (CHUNK, CHUNK), 0)
    c = lax.broadcasted_iota(jnp.int32, (CHUNK, CHUNK), 1)
    lower = (c <= r).astype(BF16)
    upper = (c >= r).astype(BF16)
    ones = jnp.ones((CHUNK, CHUNK), BF16)
    row_rhs = jnp.concatenate([upper, lower, ones], axis=1)
    col_lhs = jnp.concatenate([lower, upper], axis=1)
    src = lax.broadcasted_iota(jnp.int32, (GATE_W, GCOL_W), 0)
    dst = lax.broadcasted_iota(jnp.int32, (GATE_W, GCOL_W), 1)
    place = [((src % 8 < 4) & (dst == (src // 8) * BCOL_W + 4 * i + src % 8)).astype(BF16)
             for i in range(3)]
    m = lax.broadcasted_iota(jnp.int32, (GATE_W, CHUNK), 0) % 8
    fwd_rows = m < 2
    bwd_rows = (m >= 2) & (m < 4)
    nt = (((1,), (1,)), ((), ()))

    gr = _softcap(gr_ref[...] + br_ref[...])
    li = pltpu.roll(gr, GATE_W - 4, axis=0)
    lf3 = _split3(_log_sigmoid(gr))

    chunks = [slice(k * CHUNK, (k + 1) * CHUNK) for k in range(t // CHUNK)]
    sums = []
    for sl in chunks:
        cum = jnp.zeros((CHUNK, GATE_W), F32)
        y = jnp.zeros((GATE_W, 3 * CHUNK), F32)
        for piece in lf3:
            x = piece[:, sl]
            zero = jnp.zeros_like(x)
            xd = jnp.concatenate([jnp.where(fwd_rows, x, zero), jnp.where(bwd_rows, x, zero)],
                                 axis=1)
            cum = cum + lax.dot_general(col_lhs, xd, nt, preferred_element_type=F32)
            y = y + _dot(x, row_rhs)
        sums.append((cum, y))
    for sl, (cum, y) in zip(chunks, sums):
        out = jnp.zeros((CHUNK, GCOL_W), F32)
        for piece, sel in zip(_split3(cum * LOG2E), place):
            out = out + _dot(piece, sel)
        col_ref[sl, :] = out.astype(BF16)
        total = y[:, 2 * CHUNK:]
        e = li[:, sl] - jnp.where(fwd_rows, y[:, :CHUNK], y[:, CHUNK:2 * CHUNK])
        used = fwd_rows | bwd_rows
        row_ref[0, :, sl] = jnp.where(used, e * LOG2E, 0.0)
        row_ref[1, :, sl] = jnp.where(used, jnp.exp(total + e), 0.0)
        row_ref[2, :, sl] = jnp.where(used, jnp.exp(total), 0.0)


def _gate_prep(gates_raw, bias):
    n = gates_raw.shape[1]
    t = min(1024, n)
    return pl.pallas_call(
        _gate_kernel,
        grid=(n // t,),
        in_specs=[
            pl.BlockSpec((GATE_W, t), lambda i: (0, i)),
            _resident((GATE_W, 1)),
        ],
        out_specs=[
            pl.BlockSpec((t, GCOL_W), lambda i: (i, 0)),
            pl.BlockSpec((3, GATE_W, t), lambda i: (0, 0, i)),
        ],
        out_shape=[
            jax.ShapeDtypeStruct((n, GCOL_W), BF16),
            jax.ShapeDtypeStruct((3, GATE_W, n), F32),
        ],
        compiler_params=_params(1),
        name="mlstm_gate_prep",
    )(gates_raw, bias)


SCAN_SPLITS = 2
STATE_UNROLL = 8
OUTPUT_UNROLL = 8


def _scan_kernel(q_ref, kt_ref, v_ref, gcol_ref, grow_ref, ktc_ref, vc_ref, growc_ref,
                 h_ref, cst_ref, cs_ref):
    seq = q_ref.shape[0]
    nc = seq // CHUNK
    ncc = ktc_ref.shape[1] // CHUNK
    nc_part = nc // SCAN_SPLITS
    part = pl.program_id(2)
    r = lax.broadcasted_iota(jnp.int32, (CHUNK, CHUNK), 0)
    c = lax.broadcasted_iota(jnp.int32, (CHUNK, CHUNK), 1)
    masks = (r >= c, r <= c)
    left = c < DQK
    ones_blk = jnp.ones((CHUNK, DV), BF16)
    zeros_k = jnp.zeros((DQK, CHUNK), BF16)
    zeros_q = jnp.zeros((CHUNK, DQK), BF16)
    pair = pl.program_id(1)
    src = lax.broadcasted_iota(jnp.int32, (GCOL_W, 4 * CHUNK), 0)
    block = lax.broadcasted_iota(jnp.int32, (GCOL_W, 4 * CHUNK), 1) // CHUNK
    piece = src % BCOL_W
    sel = ((src // BCOL_W == pair) & (piece % 4 == block) & (piece < 12)).astype(BF16)

    def gate_row(ref, quantity, d, j, lanes):
        return ref[quantity, 2 * d + j:2 * d + j + 1, lanes]

    def state_increment(kt_j, v_j, w_row):
        ktw = (kt_j.astype(F32) * w_row).astype(BF16)
        return _dot(ktw, jnp.concatenate([v_j, ones_blk], axis=1))

    def decay2(a_row):
        return jnp.concatenate([a_row, a_row], axis=1)

    @pl.when(part == 0)
    def _():
        cst_ref[...] = jnp.zeros(cst_ref.shape, F32)
        for d in range(2):
            for step in range(ncc):
                k = step if d == 0 else ncc - 1 - step
                sl = slice(k * CHUNK, (k + 1) * CHUNK)
                ktp = ktc_ref[:, sl]
                vp = vc_ref[sl, :]
                for j in range(2):
                    inc = state_increment(ktp[j * DQK:(j + 1) * DQK], vp[:, j * DV:(j + 1) * DV],
                                          gate_row(growc_ref, 1, d, j, sl))
                    cst_ref[d, j] = cst_ref[d, j] * decay2(gate_row(growc_ref, 2, d, j, sl)) + inc

        def state_step(i, carry):
            for d in range(2):
                k = i if d == 0 else nc - 1 - i
                rows = pl.ds(pl.multiple_of(k * CHUNK, CHUNK), CHUNK)
                ktp = kt_ref[:, rows]
                vp = v_ref[rows, :]
                for j in range(2):
                    cst = cst_ref[d, j]
                    cs_ref[d, k, j * DQK:(j + 1) * DQK, :] = cst.astype(BF16)
                    inc = state_increment(ktp[j * DQK:(j + 1) * DQK], vp[:, j * DV:(j + 1) * DV],
                                          gate_row(grow_ref, 1, d, j, rows))
                    cst_ref[d, j] = cst * decay2(gate_row(grow_ref, 2, d, j, rows)) + inc
            return carry

        lax.fori_loop(0, nc, state_step, 0, unroll=STATE_UNROLL)

    def decay_stage(k):
        rows = pl.ds(pl.multiple_of(k * CHUNK, CHUNK), CHUNK)
        qp = q_ref[rows, :]
        ktp = kt_ref[:, rows]
        bb_all = _dot(gcol_ref[rows, :], sel)
        kbd = jnp.concatenate([jnp.concatenate([ktp[:DQK], zeros_k], axis=1),
                               jnp.concatenate([zeros_k, ktp[DQK:]], axis=1)], axis=0)
        s2 = _dot(qp, kbd)
        qf = qp.astype(F32)
        lhs = []
        for d in range(2):
            bb = [bb_all[:, (2 * d + j) * CHUNK:(2 * d + j + 1) * CHUNK] for j in range(2)]
            qs = (qf * jnp.exp2(jnp.where(left, bb[0], bb[1]))).astype(BF16)
            for j in range(2):
                e_row = gate_row(grow_ref, 0, d, j, rows)
                dm = jnp.exp2(jnp.where(masks[d], bb[j] + e_row, NEG_BIG))
                p = (s2[:, j * CHUNK:(j + 1) * CHUNK] * dm).astype(BF16)
                qs_j = [qs[:, :DQK], zeros_q] if j == 0 else [zeros_q, qs[:, DQK:]]
                lhs.append(jnp.concatenate([p] + qs_j, axis=1))
        return lhs

    def value_stage(k, i, lhs):
        rows = pl.ds(pl.multiple_of(k * CHUNK, CHUNK), CHUNK)
        out_rows = pl.ds(pl.multiple_of(i * CHUNK, CHUNK), CHUNK)
        vp = v_ref[rows, :]
        for j in range(2):
            v1 = jnp.concatenate([vp[:, j * DV:(j + 1) * DV], ones_blk], axis=1)
            h = None
            for d in range(2):
                out = _dot(lhs[2 * d + j], jnp.concatenate([v1, cs_ref[d, k]], axis=0))
                hd = out[:, :DV] / jnp.maximum(jnp.abs(out[:, DV:]), 1.0)
                h = hd if d == 0 else h + hd
            h_ref[out_rows, j * DV:(j + 1) * DV] = h

    def output_trip(t, carry):
        i0 = t * OUTPUT_UNROLL
        staged = [decay_stage(part * nc_part + i0 + g) for g in range(OUTPUT_UNROLL)]
        for g in range(OUTPUT_UNROLL):
            value_stage(part * nc_part + i0 + g, i0 + g, staged[g])
        return carry

    lax.fori_loop(0, nc_part // OUTPUT_UNROLL, output_trip, 0)


def _scan(q, kt, v, gcol, grow, ktc, vc, growc, bsz, seq, ctx_len):
    nc = seq // CHUNK
    return pl.pallas_call(
        _scan_kernel,
        grid=(bsz, PAIRS, SCAN_SPLITS),
        in_specs=[
            pl.BlockSpec((seq, 2 * DQK), lambda b, p, s: (b, p)),
            pl.BlockSpec((2 * DQK, seq), lambda b, p, s: (p, b)),
            pl.BlockSpec((seq, 2 * DV), lambda b, p, s: (b, p)),
            pl.BlockSpec((seq, GCOL_W), lambda b, p, s: (b, 0)),
            pl.BlockSpec((3, 8, seq), lambda b, p, s: (0, p, b)),
            pl.BlockSpec((2 * DQK, ctx_len), lambda b, p, s: (p, b)),
            pl.BlockSpec((ctx_len, 2 * DV), lambda b, p, s: (b, p)),
            pl.BlockSpec((3, 8, ctx_len), lambda b, p, s: (0, p, b)),
        ],
        out_specs=pl.BlockSpec((seq // SCAN_SPLITS, 2 * DV),
                               lambda b, p, s: (b * SCAN_SPLITS + s, p)),
        out_shape=jax.ShapeDtypeStruct((bsz * seq, V_W), F32),
        scratch_shapes=[pltpu.VMEM((2, 2, DQK, 2 * DV), F32),
                        pltpu.VMEM((2, nc, 2 * DQK, 2 * DV), BF16)],
        compiler_params=_params(3),
        name="mlstm_scan",
    )(q, kt, v, gcol, grow, ktc, vc, growc)


MLP_TILE = 1024
POST_TILE = 512
SUB_ROWS = 256


def _mlp_pipeline(n_sub, mixer, finish, n2_ref, sh2_ref, sc2_ref, g2_ref, w1_ref, w2_ref):
    def normed(x1):
        return (_rmsnorm(x1, n2_ref[...]) * (1.0 + sc2_ref[0]) + sh2_ref[0]).astype(BF16)

    def slab(u, k):
        cols = slice(k * FF_TILE, (k + 1) * FF_TILE)
        hid = jnp.maximum(_dot(u, w1_ref[0, :, cols].astype(BF16)), 0.0)
        return _dot((hid * hid).astype(BF16), w2_ref[0, cols, :].astype(BF16))

    def advance(gen):
        try:
            next(gen)
            return gen, None
        except StopIteration as done:
            return None, done.value

    gen, x1 = mixer(0), None
    while gen is not None:
        gen, x1 = advance(gen)
    u = normed(x1)
    for s in range(n_sub):
        gen = mixer(s + 1) if s + 1 < n_sub else None
        x1_next = u_next = None
        acc = slab(u, 0)
        for k in range(1, D_FF // FF_TILE):
            if gen is not None:
                gen, x1_next = advance(gen)
                if gen is None:
                    u_next = normed(x1_next)
            acc = acc + slab(u, k)
        assert gen is None, "mixer has more phases than hidden slabs to hide them under"
        finish(s, x1 + g2_ref[0] * acc)
        x1, u = x1_next, u_next


def _mlstm_post_kernel(h_ref, o_ref, x_ref, ng_ref, wout_ref, g1_ref, n2_ref, sh2_ref, sc2_ref,
                       g2_ref, w1_ref, w2_ref, out_ref):
    def mixer(s):
        rows = slice(s * SUB_ROWS, (s + 1) * SUB_ROWS)
        gate = jax.nn.sigmoid(o_ref[rows, :])
        parts = []
        for k in range(HEADS):
            hk = h_ref[rows, k * DV:(k + 1) * DV]
            parts.append(hk * lax.rsqrt(jnp.mean(hk * hk, axis=-1, keepdims=True) + NORM_EPS))
        hn = jnp.concatenate(parts, axis=1) * ng_ref[...]
        y = (gate * hn).astype(BF16)
        return x_ref[rows, :] + g1_ref[0] * _dot(y, wout_ref[...].astype(BF16))
        yield

    def finish(s, x2):
        out_ref[s * SUB_ROWS:(s + 1) * SUB_ROWS, :] = x2

    _mlp_pipeline(x_ref.shape[0] // SUB_ROWS, mixer, finish, n2_ref, sh2_ref, sc2_ref, g2_ref,
                  w1_ref, w2_ref)


def _mlstm_post(h, o, x2d, norm_g, w_out, g1, n2, sh2, sc2, g2, w1, w2, layer, rows_per_mod):
    n = x2d.shape[0]
    tm = POST_TILE
    tiles_per_mod = rows_per_mod // tm
    row = pl.BlockSpec((tm, D_MODEL), lambda i: (i, 0))
    mod = pl.BlockSpec((1, 1, D_MODEL), lambda i: (i // tiles_per_mod, 0, 0))
    return pl.pallas_call(
        _mlstm_post_kernel,
        grid=(n // tm,),
        in_specs=[row, row, row, _resident((1, V_W)), _resident(w_out.shape), mod,
                  _resident((1, D_MODEL)), mod, mod, mod, _layer_slab(w1.shape, layer), _layer_slab(w2.shape, layer)],
        out_specs=row,
        out_shape=jax.ShapeDtypeStruct((n, D_MODEL), F32),
        compiler_params=_params(1),
        name="mlstm_post_mlp",
    )(h, o, x2d, norm_g.reshape(1, V_W), w_out, g1, n2.reshape(1, D_MODEL), sh2, sc2, g2, w1, w2)


def _pool_constants():
    t = np.arange(SUB_ROWS)
    pos = t % GRID_W
    bands, inv = [], []
    for window in POOL_WINDOWS:
        lo = np.clip(pos - window // 2, 0, GRID_W)
        hi = np.clip(pos - window // 2 + window, 0, GRID_W)
        same_group = t[:, None] // GRID_W == t[None, :] // GRID_W
        bands.append(same_group & (pos[None, :] >= lo[:, None]) & (pos[None, :] < hi[:, None]))
        inv.append(np.broadcast_to((1.0 / (hi - lo))[:, None], (SUB_ROWS, POOL_GW)))
    return (jnp.asarray(np.stack(bands), dtype=BF16), jnp.asarray(np.stack(inv), dtype=F32))


def _pool_layer_kernel(x_ref, n1_ref, sh1_ref, sc1_ref, g1_ref, band_ref, inv_ref, pw_ref, ps_ref,
                       n2_ref, sh2_ref, sc2_ref, g2_ref, w1_ref, w2_ref, fg_ref, out_ref):
    def mixer(s):
        x = x_ref[s * SUB_ROWS:(s + 1) * SUB_ROWS, :]
        xn = _rmsnorm(x, n1_ref[...]) * (1.0 + sc1_ref[0]) + sh1_ref[0]
        hi = xn.astype(BF16)
        lo = (xn - hi.astype(F32)).astype(BF16)
        ps = []
        for gi in range(len(POOL_WINDOWS)):
            cols = slice(gi * POOL_GW, (gi + 1) * POOL_GW)
            band = band_ref[gi]
            total = _dot(band, hi[:, cols]) + _dot(band, lo[:, cols])
            ps.append((total * inv_ref[gi] - xn[:, cols]).astype(BF16))
        yield
        ys = [_dot(p, pw_ref[gi]) for gi, p in enumerate(ps)]
        return x + g1_ref[0] * (jnp.concatenate(ys, axis=1) * ps_ref[...])

    def finish(s, x2):
        out_ref[s * SUB_ROWS:(s + 1) * SUB_ROWS, :] = _rmsnorm(x2, fg_ref[...])

    _mlp_pipeline(x_ref.shape[0] // SUB_ROWS, mixer, finish, n2_ref, sh2_ref, sc2_ref, g2_ref,
                  w1_ref, w2_ref)


def _pool_layer(x2d, n1, sh1, sc1, g1, pool_w, pool_scale, n2, sh2, sc2, g2, w1, w2, layer,
                final_g, rows_per_mod):
    n = x2d.shape[0]
    tm = MLP_TILE
    tiles_per_mod = rows_per_mod // tm
    row = pl.BlockSpec((tm, D_MODEL), lambda i: (i, 0))
    mod = pl.BlockSpec((1, 1, D_MODEL), lambda i: (i // tiles_per_mod, 0, 0))
    vec = _resident((1, D_MODEL))
    bands, inv = _pool_constants()
    return pl.pallas_call(
        _pool_layer_kernel,
        grid=(n // tm,),
        in_specs=[row, vec, mod, mod, mod, _resident(bands.shape), _resident(inv.shape),
                  _resident(pool_w.shape), vec, vec, mod, mod, mod,
                  _layer_slab(w1.shape, layer), _layer_slab(w2.shape, layer), vec],
        out_specs=row,
        out_shape=jax.ShapeDtypeStruct((n, D_MODEL), F32),
        compiler_params=_params(1),
        name="pool_mlp_final",
    )(x2d, n1.reshape(1, D_MODEL), sh1, sc1, g1, bands, inv, pool_w,
      pool_scale.reshape(1, D_MODEL),
      n2.reshape(1, D_MODEL), sh2, sc2, g2, w1, w2, final_g.reshape(1, D_MODEL))


def kernel(x, c, ctx, c_ctx, ada_w, ada_b, norm1_g, norm2_g, mlstm_w_in, mlstm_gate_b,
           mlstm_norm_g, mlstm_w_out, pool_w, pool_scale, mlp_w1, mlp_w2, final_g):
    bsz, seq, dm = x.shape
    ctx_len = ctx.shape[1]
    assert dm == D_MODEL and seq % (2 * CHUNK) == 0 and ctx_len % CHUNK == 0
    assert seq % ROW_TILE == 0 and seq % MLP_TILE == 0 and bsz + 1 <= MOD_ROWS
    assert MLP_TILE % SUB_ROWS == 0 and SUB_ROWS % GRID_W == 0
    assert ada_w.shape[0] == 2 and mlstm_w_in.shape[0] == 1 and pool_w.shape[0] == 1

    cmat = jnp.concatenate(
        [c, c_ctx[None, :], jnp.zeros((MOD_ROWS - bsz - 1, dm), F32)], axis=0)
    mod = _modulation(cmat, ada_w, ada_b)

    def mod_rows(layer, which, rows):
        return mod[layer, rows, which * dm:(which + 1) * dm][:, None, :]

    lat = slice(0, bsz)
    cx = slice(bsz, bsz + 1)

    def pair_major(g):
        g4 = g.reshape(g.shape[:-1] + (4, PAIRS, 2))
        g4 = jnp.stack([g4[..., t, :, :] for t in (1, 3, 0, 2)], axis=-2)
        return g4.reshape(g.shape)

    w_in = mlstm_w_in[0].T
    gate_b = pair_major(mlstm_gate_b[0].reshape(GATE_W)).reshape(GATE_W, 1)

    x2d = x.reshape(bsz * seq, dm)
    q, kt, v, o, gates = _front(
        x2d, norm1_g[0], mod_rows(0, 0, lat), mod_rows(0, 1, lat), w_in, seq, True)
    ktc, vc, gates_c = _front(
        ctx.reshape(bsz * ctx_len, dm), norm1_g[0], mod_rows(0, 0, cx), mod_rows(0, 1, cx),
        w_in, bsz * ctx_len, False)
    gcol, grow = _gate_prep(gates, gate_b)
    _, growc = _gate_prep(gates_c, gate_b)

    h = _scan(q, kt, v, gcol, grow, ktc, vc, growc, bsz, seq, ctx_len)
    x2d = _mlstm_post(
        h, o, x2d, mlstm_norm_g[0], mlstm_w_out[0], mod_rows(0, 2, lat),
        norm2_g[0], mod_rows(0, 3, lat), mod_rows(0, 4, lat), mod_rows(0, 5, lat),
        mlp_w1, mlp_w2, 0, seq)

    out = _pool_layer(
        x2d, norm1_g[1], mod_rows(1, 0, lat), mod_rows(1, 1, lat), mod_rows(1, 2, lat),
        pool_w[0].astype(BF16), pool_scale[0], norm2_g[1], mod_rows(1, 3, lat),
        mod_rows(1, 4, lat), mod_rows(1, 5, lat), mlp_w1, mlp_w2, 1, final_g, seq)
    return out.reshape(bsz, seq, dm)
```

```python
import functools

import jax
import jax.numpy as jnp
import numpy as np
from jax import lax
from jax.experimental import pallas as pl
from jax.experimental.pallas import tpu as pltpu

F32 = jnp.float32
BF16 = jnp.bfloat16

D_MODEL = 1024
HEADS = 8
DV = 128
DQK = 64
QK_W = HEADS * DQK
V_W = HEADS * DV
GATE_W = 4 * HEADS
D_FF = 4 * D_MODEL
NORM_EPS = 1e-6
GATE_SOFTCAP = 15.0
GRID_W = 64
POOL_WINDOWS = (2, 4, 8, 16)
POOL_GW = D_MODEL // len(POOL_WINDOWS)

LANES = 128
CHUNK = LANES
MOD_ROWS = 8
VMEM_LIMIT = 62 * 1024 * 1024
ROW_TILE = 1024
FF_TILE = 1024
NEG_BIG = -1e30
LOG2E = 1.4426950408889634


def _params(n_axes):
    return pltpu.CompilerParams(
        dimension_semantics=("arbitrary",) * n_axes, vmem_limit_bytes=VMEM_LIMIT)


def _resident(shape):
    return pl.BlockSpec(shape, lambda *_: (0,) * len(shape), pipeline_mode=pl.Buffered(1))


def _layer_slab(shape, layer):
    return pl.BlockSpec((1,) + tuple(shape[1:]), lambda *_: (layer,) + (0,) * (len(shape) - 1),
                        pipeline_mode=pl.Buffered(1))


def _rmsnorm(x, g):
    return x * lax.rsqrt(jnp.mean(x * x, axis=-1, keepdims=True) + NORM_EPS) * g


def _dot(a, b):
    return jnp.dot(a, b, preferred_element_type=F32)


def _split3(x):
    hi = x.astype(BF16)
    rest = x - hi.astype(F32)
    mid = rest.astype(BF16)
    return hi, mid, (rest - mid.astype(F32)).astype(BF16)


def _mod_kernel(c_ref, w_ref, b_ref, o_ref):
    c = c_ref[...]
    s = c * jax.nn.sigmoid(c)
    lhs = jnp.concatenate([p.astype(F32) for p in _split3(s)], axis=0).astype(BF16)
    acc = jnp.zeros((3 * MOD_ROWS, w_ref.shape[2]), F32)
    for piece in _split3(w_ref[0]):
        acc = acc + _dot(lhs, piece)
    o_ref[0] = acc[:MOD_ROWS] + acc[MOD_ROWS:2 * MOD_ROWS] + acc[2 * MOD_ROWS:] + b_ref[0]


def _modulation(cmat, ada_w, ada_b):
    depth, _, n = ada_w.shape
    tn = 1536
    return pl.pallas_call(
        _mod_kernel,
        grid=(depth, n // tn),
        in_specs=[
            pl.BlockSpec((MOD_ROWS, D_MODEL), lambda i, j: (0, 0)),
            pl.BlockSpec((1, D_MODEL, tn), lambda i, j: (i, 0, j)),
            pl.BlockSpec((1, 1, tn), lambda i, j: (i, 0, j)),
        ],
        out_specs=pl.BlockSpec((1, MOD_ROWS, tn), lambda i, j: (i, 0, j)),
        out_shape=jax.ShapeDtypeStruct((depth, MOD_ROWS, n), F32),
        compiler_params=_params(2),
        name="adaln_modulation",
    )(cmat, ada_w, ada_b.reshape(depth, 1, n))


V_START = 2 * QK_W
O_START = V_START + V_W
G_START = O_START + V_W


KG_ROW = QK_W + 2 * V_W


def _front_kernel(with_qo, n_riders, x_ref, g_ref, sh_ref, sc_ref, wt32_ref, *rest):
    riders_in, rest = rest[:n_riders], rest[n_riders:]
    *outs, w_ref = rest
    outs, riders_out = outs[:len(outs) - n_riders], outs[len(outs) - n_riders:]
    for src, dst in zip(riders_in, riders_out):
        dst[...] = src[...].astype(BF16)
    nt = (((1,), (1,)), ((), ()))

    @pl.when(pl.program_id(0) == 0)
    def _():
        w_ref[:QK_W, :] = wt32_ref[:QK_W, :].astype(BF16)
        w_ref[QK_W:KG_ROW, :] = wt32_ref[V_START:G_START, :].astype(BF16)
        w_ref[KG_ROW:KG_ROW + QK_W, :] = wt32_ref[QK_W:V_START, :].astype(BF16)
        new = lax.broadcasted_iota(jnp.int32, (GATE_W, GATE_W), 0)
        old = lax.broadcasted_iota(jnp.int32, (GATE_W, GATE_W), 1)
        kind = new % 8 // 2
        gate_type = jnp.where(kind < 2, 2 * kind + 1, 2 * (kind - 2))
        perm = (old == gate_type * HEADS + 2 * (new // 8) + new % 2).astype(BF16)
        w_ref[KG_ROW + QK_W:, :] = _dot(perm, wt32_ref[G_START:, :].astype(BF16)).astype(BF16)

    def normed(s):
        x = x_ref[s * SUB_ROWS:(s + 1) * SUB_ROWS, :]
        return (_rmsnorm(x, g_ref[...]) * (1.0 + sc_ref[0]) + sh_ref[0]).astype(BF16)

    if with_qo:
        q_ref, kt_ref, v_ref, o_ref, gr_ref = outs
    else:
        kt_ref, v_ref, gr_ref = outs

    n_sub = x_ref.shape[0] // SUB_ROWS
    xb = normed(0)
    for s in range(n_sub):
        rows = slice(s * SUB_ROWS, (s + 1) * SUB_ROWS)

        def token_major(w_rows, xb=xb):
            return lax.dot_general(xb, w_ref[w_rows, :], nt, preferred_element_type=F32)

        if with_qo:
            q_ref[rows, :] = (token_major(slice(0, QK_W)) * (DQK ** -0.5)).astype(BF16)
        xb_next = normed(s + 1) if s + 1 < n_sub else None
        if with_qo:
            o_ref[rows, :] = token_major(slice(QK_W + V_W, KG_ROW))
        kg = lax.dot_general(w_ref[KG_ROW:, :], xb, nt, preferred_element_type=F32)
        kt_ref[:, rows] = kg[:QK_W].astype(BF16)
        gr_ref[:, rows] = kg[QK_W:]
        v_ref[rows, :] = token_major(slice(QK_W, QK_W + V_W)).astype(BF16)
        xb = xb_next


def _front(x2d, norm_g, sh, sc, w_in_t, rows_per_mod, with_qo, riders=()):
    n = x2d.shape[0]
    tm = min(ROW_TILE, n)
    steps = n // tm
    tiles_per_mod = rows_per_mod // tm
    row = lambda i: (i, 0)
    mod = lambda i: (i // tiles_per_mod, 0, 0)
    rider_specs = [pl.BlockSpec((r.shape[0] // steps, r.shape[1]), row) for r in riders]
    out_shape, out_specs = [], []
    if with_qo:
        out_shape.append(jax.ShapeDtypeStruct((n, QK_W), BF16))
        out_specs.append(pl.BlockSpec((tm, QK_W), row))
    out_shape.append(jax.ShapeDtypeStruct((QK_W, n), BF16))
    out_specs.append(pl.BlockSpec((QK_W, tm), lambda i: (0, i)))
    out_shape.append(jax.ShapeDtypeStruct((n, V_W), BF16))
    out_specs.append(pl.BlockSpec((tm, V_W), row))
    if with_qo:
        out_shape.append(jax.ShapeDtypeStruct((n, V_W), F32))
        out_specs.append(pl.BlockSpec((tm, V_W), row))
    out_shape.append(jax.ShapeDtypeStruct((GATE_W, n), F32))
    out_specs.append(pl.BlockSpec((GATE_W, tm), lambda i: (0, i)))
    out_shape += [jax.ShapeDtypeStruct(r.shape, BF16) for r in riders]
    out_specs += rider_specs
    return pl.pallas_call(
        functools.partial(_front_kernel, with_qo, len(riders)),
        grid=(steps,),
        in_specs=[
            pl.BlockSpec((tm, D_MODEL), row),
            _resident((1, D_MODEL)),
            pl.BlockSpec((1, 1, D_MODEL), mod),
            pl.BlockSpec((1, 1, D_MODEL), mod),
            _resident(w_in_t.shape),
        ] + rider_specs,
        out_specs=out_specs,
        out_shape=out_shape,
        scratch_shapes=[pltpu.VMEM(w_in_t.shape, BF16)],
        compiler_params=_params(1),
        name="mlstm_front_qo" if with_qo else "mlstm_front_ctx",
    )(x2d, norm_g.reshape(1, D_MODEL), sh, sc, w_in_t, *riders)


def _softcap(g):
    return GATE_SOFTCAP * jnp.tanh(g * (1.0 / GATE_SOFTCAP))


def _log_sigmoid(g):
    return jnp.minimum(g, 0.0) - jnp.log1p(jnp.exp(-jnp.abs(g)))


PAIRS = HEADS // 2
BCOL_W = 16
GCOL_W = PAIRS * BCOL_W


def _gate_kernel(gr_ref, br_ref, col_ref, row_ref):
    t = gr_ref.shape[1]
    r = lax.broadcasted_iota(jnp.int32, (CHUNK, CHUNK), 0)
    c = lax.broadcasted_iota(jnp.int32, (CHUNK, CHUNK), 1)
    lower = (c <= r).astype(BF16)
    upper = (c >= r).astype(BF16)
    ones = jnp.ones((CHUNK, CHUNK), BF16)
    row_rhs = jnp.concatenate([upper, lower, ones], axis=1)
    col_lhs = jnp.concatenate([lower, upper], axis=1)
    src = lax.broadcasted_iota(jnp.int32, (3 * GATE_W, GCOL_W), 0)
    dst = lax.broadcasted_iota(jnp.int32, (3 * GATE_W, GCOL_W), 1)
    gate = src % GATE_W
    place = ((gate % 8 < 4)
             & (dst == (gate // 8) * BCOL_W + 4 * (src // GATE_W) + gate % 8)).astype(BF16)
    lane_piece = lax.broadcasted_iota(jnp.int32, (CHUNK, 3 * GATE_W), 1) // GATE_W
    m = lax.broadcasted_iota(jnp.int32, (GATE_W, CHUNK), 0) % 8
    fwd_rows = m < 2
    bwd_rows = (m >= 2) & (m < 4)
    nt = (((1,), (1,)), ((), ()))

    gr = _softcap(gr_ref[...] + br_ref[...])
    li = pltpu.roll(gr, GATE_W - 4, axis=0)
    lf3 = _split3(_log_sigmoid(gr))

    chunks = [slice(k * CHUNK, (k + 1) * CHUNK) for k in range(t // CHUNK)]
    sums = []
    for sl in chunks:
        cum = jnp.zeros((CHUNK, 3 * GATE_W), F32)
        y = jnp.zeros((GATE_W, 3 * CHUNK), F32)
        for piece in lf3:
            x = piece[:, sl]
            zero = jnp.zeros_like(x)
            xd = jnp.concatenate([jnp.where(fwd_rows, x, zero), jnp.where(bwd_rows, x, zero)],
                                 axis=1)
            cum = cum + lax.dot_general(col_lhs, jnp.concatenate([xd, xd, xd], axis=0), nt,
                                        preferred_element_type=F32)
            y = y + _dot(x, row_rhs)
        sums.append((cum, y))
    for sl, (cum, y) in zip(chunks, sums):
        hi, mid, lo = [p.astype(F32) for p in _split3(cum * LOG2E)]
        pieces = jnp.where(lane_piece == 0, hi, jnp.where(lane_piece == 1, mid, lo))
        col_ref[sl, :] = _dot(pieces.astype(BF16), place).astype(BF16)
        total = y[:, 2 * CHUNK:]
        e = li[:, sl] - jnp.where(fwd_rows, y[:, :CHUNK], y[:, CHUNK:2 * CHUNK])
        used = fwd_rows | bwd_rows
        row_ref[0, :, sl] = jnp.where(used, e * LOG2E, 0.0)
        row_ref[1, :, sl] = jnp.where(used, jnp.exp(total + e), 0.0)
        row_ref[2, :, sl] = jnp.where(used, jnp.exp(total), 0.0)


def _gate_prep(gates_raw, bias):
    n = gates_raw.shape[1]
    t = min(1024, n)
    return pl.pallas_call(
        _gate_kernel,
        grid=(n // t,),
        in_specs=[
            pl.BlockSpec((GATE_W, t), lambda i: (0, i)),
            _resident((GATE_W, 1)),
        ],
        out_specs=[
            pl.BlockSpec((t, GCOL_W), lambda i: (i, 0)),
            pl.BlockSpec((3, GATE_W, t), lambda i: (0, 0, i)),
        ],
        out_shape=[
            jax.ShapeDtypeStruct((n, GCOL_W), BF16),
            jax.ShapeDtypeStruct((3, GATE_W, n), F32),
        ],
        compiler_params=_params(1),
        name="mlstm_gate_prep",
    )(gates_raw, bias)


SCAN_SPLITS = 2
STATE_UNROLL = 8
OUTPUT_UNROLL = 8


def _scan_kernel(q_ref, kt_ref, v_ref, gcol_ref, grow_ref, ktc_ref, vc_ref, growc_ref,
                 h_ref, cst_ref, cs_ref):
    seq = q_ref.shape[0]
    nc = seq // CHUNK
    ncc = ktc_ref.shape[1] // CHUNK
    nc_part = nc // SCAN_SPLITS
    part = pl.program_id(2)
    r = lax.broadcasted_iota(jnp.int32, (CHUNK, CHUNK), 0)
    c = lax.broadcasted_iota(jnp.int32, (CHUNK, CHUNK), 1)
    masks = (r >= c, r <= c)
    left = c < DQK
    ones_blk = jnp.ones((CHUNK, DV), BF16)
    zeros_k = jnp.zeros((DQK, CHUNK), BF16)
    zeros_q = jnp.zeros((CHUNK, DQK), BF16)
    pair = pl.program_id(1)
    src = lax.broadcasted_iota(jnp.int32, (GCOL_W, 4 * CHUNK), 0)
    block = lax.broadcasted_iota(jnp.int32, (GCOL_W, 4 * CHUNK), 1) // CHUNK
    piece = src % BCOL_W
    sel = ((src // BCOL_W == pair) & (piece % 4 == block) & (piece < 12)).astype(BF16)

    def gate_row(ref, quantity, d, j, lanes):
        return ref[quantity, 2 * d + j:2 * d + j + 1, lanes]

    def state_increment(kt_j, v_j, w_row):
        ktw = (kt_j.astype(F32) * w_row).astype(BF16)
        return _dot(ktw, jnp.concatenate([v_j, ones_blk], axis=1))

    def decay2(a_row):
        return jnp.concatenate([a_row, a_row], axis=1)

    @pl.when(part == 0)
    def _():
        cst_ref[...] = jnp.zeros(cst_ref.shape, F32)
        for d in range(2):
            for step in range(ncc):
                k = step if d == 0 else ncc - 1 - step
                sl = slice(k * CHUNK, (k + 1) * CHUNK)
                ktp = ktc_ref[:, sl]
                vp = vc_ref[sl, :]
                for j in range(2):
                    inc = state_increment(ktp[j * DQK:(j + 1) * DQK], vp[:, j * DV:(j + 1) * DV],
                                          gate_row(growc_ref, 1, d, j, sl))
                    cst_ref[d, j] = cst_ref[d, j] * decay2(gate_row(growc_ref, 2, d, j, sl)) + inc

        def state_step(i, carry):
            for d in range(2):
                k = i if d == 0 else nc - 1 - i
                rows = pl.ds(pl.multiple_of(k * CHUNK, CHUNK), CHUNK)
                ktp = kt_ref[:, rows]
                vp = v_ref[rows, :]
                for j in range(2):
                    cst = cst_ref[d, j]
                    cs_ref[d, k, j * DQK:(j + 1) * DQK, :] = cst.astype(BF16)
                    inc = state_increment(ktp[j * DQK:(j + 1) * DQK], vp[:, j * DV:(j + 1) * DV],
                                          gate_row(grow_ref, 1, d, j, rows))
                    cst_ref[d, j] = cst * decay2(gate_row(grow_ref, 2, d, j, rows)) + inc
            return carry

        lax.fori_loop(0, nc, state_step, 0, unroll=STATE_UNROLL)

    def decay_stage(k):
        rows = pl.ds(pl.multiple_of(k * CHUNK, CHUNK), CHUNK)
        qp = q_ref[rows, :]
        ktp = kt_ref[:, rows]
        bb_all = _dot(gcol_ref[rows, :], sel)
        kbd = jnp.concatenate([jnp.concatenate([ktp[:DQK], zeros_k], axis=1),
                               jnp.concatenate([zeros_k, ktp[DQK:]], axis=1)], axis=0)
        s2 = _dot(qp, kbd)
        qf = qp.astype(F32)
        lhs = []
        for d in range(2):
            bb = [bb_all[:, (2 * d + j) * CHUNK:(2 * d + j + 1) * CHUNK] for j in range(2)]
            qs = (qf * jnp.exp2(jnp.where(left, bb[0], bb[1]))).astype(BF16)
            for j in range(2):
                e_row = gate_row(grow_ref, 0, d, j, rows)
                dm = jnp.exp2(jnp.where(masks[d], bb[j] + e_row, NEG_BIG))
                p = (s2[:, j * CHUNK:(j + 1) * CHUNK] * dm).astype(BF16)
                qs_j = [qs[:, :DQK], zeros_q] if j == 0 else [zeros_q, qs[:, DQK:]]
                lhs.append(jnp.concatenate([p] + qs_j, axis=1))
        return lhs

    def value_stage(k, i, lhs):
        rows = pl.ds(pl.multiple_of(k * CHUNK, CHUNK), CHUNK)
        out_rows = pl.ds(pl.multiple_of(i * CHUNK, CHUNK), CHUNK)
        vp = v_ref[rows, :]
        for j in range(2):
            v1 = jnp.concatenate([vp[:, j * DV:(j + 1) * DV], ones_blk], axis=1)
            h = None
            for d in range(2):
                out = _dot(lhs[2 * d + j], jnp.concatenate([v1, cs_ref[d, k]], axis=0))
                hd = out[:, :DV] / jnp.maximum(jnp.abs(out[:, DV:]), 1.0)
                h = hd if d == 0 else h + hd
            h_ref[out_rows, j * DV:(j + 1) * DV] = h

    def output_trip(t, carry):
        i0 = t * OUTPUT_UNROLL
        staged = [decay_stage(part * nc_part + i0 + g) for g in range(OUTPUT_UNROLL)]
        for g in range(OUTPUT_UNROLL):
            value_stage(part * nc_part + i0 + g, i0 + g, staged[g])
        return carry

    lax.fori_loop(0, nc_part // OUTPUT_UNROLL, output_trip, 0)


def _scan(q, kt, v, gcol, grow, ktc, vc, growc, bsz, seq, ctx_len):
    nc = seq // CHUNK
    return pl.pallas_call(
        _scan_kernel,
        grid=(bsz, PAIRS, SCAN_SPLITS),
        in_specs=[
            pl.BlockSpec((seq, 2 * DQK), lambda b, p, s: (b, p)),
            pl.BlockSpec((2 * DQK, seq), lambda b, p, s: (p, b)),
            pl.BlockSpec((seq, 2 * DV), lambda b, p, s: (b, p)),
            pl.BlockSpec((seq, GCOL_W), lambda b, p, s: (b, 0)),
            pl.BlockSpec((3, 8, seq), lambda b, p, s: (0, p, b)),
            pl.BlockSpec((2 * DQK, ctx_len), lambda b, p, s: (p, b)),
            pl.BlockSpec((ctx_len, 2 * DV), lambda b, p, s: (b, p)),
            pl.BlockSpec((3, 8, ctx_len), lambda b, p, s: (0, p, b)),
        ],
        out_specs=pl.BlockSpec((seq // SCAN_SPLITS, 2 * DV),
                               lambda b, p, s: (b * SCAN_SPLITS + s, p)),
        out_shape=jax.ShapeDtypeStruct((bsz * seq, V_W), F32),
        scratch_shapes=[pltpu.VMEM((2, 2, DQK, 2 * DV), F32),
                        pltpu.VMEM((2, nc, 2 * DQK, 2 * DV), BF16)],
        compiler_params=_params(3),
        name="mlstm_scan",
    )(q, kt, v, gcol, grow, ktc, vc, growc)


MLP_TILE = 1024
POST_TILE = 1024
SUB_ROWS = 256


def _mlp_pipeline(n_sub, mixer, finish, n2_ref, sh2_ref, sc2_ref, g2_ref, w1_ref, w2_ref):
    def normed(x1):
        return (_rmsnorm(x1, n2_ref[...]) * (1.0 + sc2_ref[0]) + sh2_ref[0]).astype(BF16)

    def slab(u, k):
        cols = slice(k * FF_TILE, (k + 1) * FF_TILE)
        hid = jnp.maximum(_dot(u, w1_ref[0, :, cols]), 0.0)
        return _dot((hid * hid).astype(BF16), w2_ref[0, cols, :])

    def advance(gen):
        try:
            next(gen)
            return gen, None
        except StopIteration as done:
            return None, done.value

    gen, x1 = mixer(0), None
    while gen is not None:
        gen, x1 = advance(gen)
    u = normed(x1)
    for s in range(n_sub):
        gen = mixer(s + 1) if s + 1 < n_sub else None
        x1_next = u_next = None
        acc = slab(u, 0)
        for k in range(1, D_FF // FF_TILE):
            if gen is not None:
                gen, x1_next = advance(gen)
                if gen is None:
                    u_next = normed(x1_next)
            acc = acc + slab(u, k)
        assert gen is None, "mixer has more phases than hidden slabs to hide them under"
        finish(s, x1 + g2_ref[0] * acc)
        x1, u = x1_next, u_next


def _mlstm_post_kernel(h_ref, o_ref, x_ref, ng_ref, wout_ref, g1_ref, n2_ref, sh2_ref, sc2_ref,
                       g2_ref, w1_ref, w2_ref, out_ref):
    def mixer(s):
        rows = slice(s * SUB_ROWS, (s + 1) * SUB_ROWS)
        gate = jax.nn.sigmoid(o_ref[rows, :])
        parts = []
        for k in range(HEADS):
            hk = h_ref[rows, k * DV:(k + 1) * DV]
            parts.append(hk * lax.rsqrt(jnp.mean(hk * hk, axis=-1, keepdims=True) + NORM_EPS))
        hn = jnp.concatenate(parts, axis=1) * ng_ref[...]
        y = (gate * hn).astype(BF16)
        return x_ref[rows, :] + g1_ref[0] * _dot(y, wout_ref[...])
        yield

    def finish(s, x2):
        out_ref[s * SUB_ROWS:(s + 1) * SUB_ROWS, :] = x2

    _mlp_pipeline(x_ref.shape[0] // SUB_ROWS, mixer, finish, n2_ref, sh2_ref, sc2_ref, g2_ref,
                  w1_ref, w2_ref)


def _mlstm_post(h, o, x2d, norm_g, w_out, g1, n2, sh2, sc2, g2, w1, w2, layer, rows_per_mod):
    n = x2d.shape[0]
    tm = POST_TILE
    tiles_per_mod = rows_per_mod // tm
    row = pl.BlockSpec((tm, D_MODEL), lambda i: (i, 0))
    mod = pl.BlockSpec((1, 1, D_MODEL), lambda i: (i // tiles_per_mod, 0, 0))
    return pl.pallas_call(
        _mlstm_post_kernel,
        grid=(n // tm,),
        in_specs=[row, row, row, _resident((1, V_W)), _resident(w_out.shape), mod,
                  _resident((1, D_MODEL)), mod, mod, mod, _layer_slab(w1.shape, layer), _layer_slab(w2.shape, layer)],
        out_specs=row,
        out_shape=jax.ShapeDtypeStruct((n, D_MODEL), F32),
        compiler_params=_params(1),
        name="mlstm_post_mlp",
    )(h, o, x2d, norm_g.reshape(1, V_W), w_out, g1, n2.reshape(1, D_MODEL), sh2, sc2, g2, w1, w2)


def _pool_constants():
    t = np.arange(SUB_ROWS)
    pos = t % GRID_W
    bands, inv = [], []
    for window in POOL_WINDOWS:
        lo = np.clip(pos - window // 2, 0, GRID_W)
        hi = np.clip(pos - window // 2 + window, 0, GRID_W)
        same_group = t[:, None] // GRID_W == t[None, :] // GRID_W
        bands.append(same_group & (pos[None, :] >= lo[:, None]) & (pos[None, :] < hi[:, None]))
        inv.append(np.broadcast_to((1.0 / (hi - lo))[:, None], (SUB_ROWS, POOL_GW)))
    return (jnp.asarray(np.stack(bands), dtype=BF16), jnp.asarray(np.stack(inv), dtype=F32))


def _pool_layer_kernel(x_ref, n1_ref, sh1_ref, sc1_ref, g1_ref, band_ref, inv_ref, pw_ref, ps_ref,
                       n2_ref, sh2_ref, sc2_ref, g2_ref, w1_ref, w2_ref, fg_ref, out_ref):
    def mixer(s):
        x = x_ref[s * SUB_ROWS:(s + 1) * SUB_ROWS, :]
        xn = _rmsnorm(x, n1_ref[...]) * (1.0 + sc1_ref[0]) + sh1_ref[0]
        hi = xn.astype(BF16)
        lo = (xn - hi.astype(F32)).astype(BF16)
        ps = []
        for gi in range(len(POOL_WINDOWS)):
            cols = slice(gi * POOL_GW, (gi + 1) * POOL_GW)
            band = band_ref[gi]
            total = _dot(band, hi[:, cols]) + _dot(band, lo[:, cols])
            ps.append((total * inv_ref[gi] - xn[:, cols]).astype(BF16))
        yield
        ys = [_dot(p, pw_ref[gi]) for gi, p in enumerate(ps)]
        return x + g1_ref[0] * (jnp.concatenate(ys, axis=1) * ps_ref[...])

    def finish(s, x2):
        out_ref[s * SUB_ROWS:(s + 1) * SUB_ROWS, :] = _rmsnorm(x2, fg_ref[...])

    _mlp_pipeline(x_ref.shape[0] // SUB_ROWS, mixer, finish, n2_ref, sh2_ref, sc2_ref, g2_ref,
                  w1_ref, w2_ref)


def _pool_layer(x2d, n1, sh1, sc1, g1, pool_w, pool_scale, n2, sh2, sc2, g2, w1, w2, layer,
                final_g, rows_per_mod):
    n = x2d.shape[0]
    tm = MLP_TILE
    tiles_per_mod = rows_per_mod // tm
    row = pl.BlockSpec((tm, D_MODEL), lambda i: (i, 0))
    mod = pl.BlockSpec((1, 1, D_MODEL), lambda i: (i // tiles_per_mod, 0, 0))
    vec = _resident((1, D_MODEL))
    bands, inv = _pool_constants()
    return pl.pallas_call(
        _pool_layer_kernel,
        grid=(n // tm,),
        in_specs=[row, vec, mod, mod, mod, _resident(bands.shape), _resident(inv.shape),
                  _resident(pool_w.shape), vec, vec, mod, mod, mod,
                  _layer_slab(w1.shape, layer), _layer_slab(w2.shape, layer), vec],
        out_specs=row,
        out_shape=jax.ShapeDtypeStruct((n, D_MODEL), F32),
        compiler_params=_params(1),
        name="pool_mlp_final",
    )(x2d, n1.reshape(1, D_MODEL), sh1, sc1, g1, bands, inv, pool_w,
      pool_scale.reshape(1, D_MODEL),
      n2.reshape(1, D_MODEL), sh2, sc2, g2, w1, w2, final_g.reshape(1, D_MODEL))


def kernel(x, c, ctx, c_ctx, ada_w, ada_b, norm1_g, norm2_g, mlstm_w_in, mlstm_gate_b,
           mlstm_norm_g, mlstm_w_out, pool_w, pool_scale, mlp_w1, mlp_w2, final_g):
    bsz, seq, dm = x.shape
    ctx_len = ctx.shape[1]
    assert dm == D_MODEL and seq % (2 * CHUNK) == 0 and ctx_len % CHUNK == 0
    assert seq % ROW_TILE == 0 and seq % MLP_TILE == 0 and seq % POST_TILE == 0
    assert (bsz * ctx_len) % SUB_ROWS == 0 and ROW_TILE % SUB_ROWS == 0 and bsz + 1 <= MOD_ROWS
    assert MLP_TILE % SUB_ROWS == 0 and SUB_ROWS % GRID_W == 0
    assert ada_w.shape[0] == 2 and mlstm_w_in.shape[0] == 1 and pool_w.shape[0] == 1

    cmat = jnp.concatenate(
        [c, c_ctx[None, :], jnp.zeros((MOD_ROWS - bsz - 1, dm), F32)], axis=0)
    mod = _modulation(cmat, ada_w, ada_b)

    def mod_rows(layer, which, rows):
        return mod[layer, rows, which * dm:(which + 1) * dm][:, None, :]

    lat = slice(0, bsz)
    cx = slice(bsz, bsz + 1)

    def pair_major(g):
        g4 = g.reshape(g.shape[:-1] + (4, PAIRS, 2))
        g4 = jnp.stack([g4[..., t, :, :] for t in (1, 3, 0, 2)], axis=-2)
        return g4.reshape(g.shape)

    w_in = mlstm_w_in[0].T
    gate_b = pair_major(mlstm_gate_b[0].reshape(GATE_W)).reshape(GATE_W, 1)

    x2d = x.reshape(bsz * seq, dm)
    depth = mlp_w1.shape[0]
    riders = (mlp_w1.reshape(depth * dm, D_FF), mlp_w2.reshape(depth * D_FF, dm),
              mlstm_w_out[0])
    q, kt, v, o, gates, w1, w2, w_out = _front(
        x2d, norm1_g[0], mod_rows(0, 0, lat), mod_rows(0, 1, lat), w_in, seq, True, riders)
    w1 = w1.reshape(depth, dm, D_FF)
    w2 = w2.reshape(depth, D_FF, dm)
    ktc, vc, gates_c = _front(
        ctx.reshape(bsz * ctx_len, dm), norm1_g[0], mod_rows(0, 0, cx), mod_rows(0, 1, cx),
        w_in, bsz * ctx_len, False)
    gcol, grow = _gate_prep(gates, gate_b)
    _, growc = _gate_prep(gates_c, gate_b)

    h = _scan(q, kt, v, gcol, grow, ktc, vc, growc, bsz, seq, ctx_len)
    x2d = _mlstm_post(
        h, o, x2d, mlstm_norm_g[0], w_out, mod_rows(0, 2, lat),
        norm2_g[0], mod_rows(0, 3, lat), mod_rows(0, 4, lat), mod_rows(0, 5, lat),
        w1, w2, 0, seq)

    out = _pool_layer(
        x2d, norm1_g[1], mod_rows(1, 0, lat), mod_rows(1, 1, lat), mod_rows(1, 2, lat),
        pool_w[0].astype(BF16), pool_scale[0], norm2_g[1], mod_rows(1, 3, lat),
        mod_rows(1, 4, lat), mod_rows(1, 5, lat), w1, w2, 1, final_g, seq)
    return out.reshape(bsz, seq, dm)
```

```python
import functools

import jax
import jax.numpy as jnp
import numpy as np
from jax import lax
from jax.experimental import pallas as pl
from jax.experimental.pallas import tpu as pltpu

F32 = jnp.float32
BF16 = jnp.bfloat16

D_MODEL = 1024
HEADS = 8
DV = 128
DQK = 64
QK_W = HEADS * DQK
V_W = HEADS * DV
GATE_W = 4 * HEADS
D_FF = 4 * D_MODEL
NORM_EPS = 1e-6
GATE_SOFTCAP = 15.0
GRID_W = 64
POOL_WINDOWS = (2, 4, 8, 16)
POOL_GW = D_MODEL // len(POOL_WINDOWS)

LANES = 128
CHUNK = LANES
MOD_ROWS = 8
MOD_TILE = 1536
VMEM_LIMIT = 62 * 1024 * 1024
ROW_TILE = 1024
FF_TILE = 1024
NEG_BIG = -1e30
LOG2E = 1.4426950408889634


def _params(n_axes):
    return pltpu.CompilerParams(
        dimension_semantics=("arbitrary",) * n_axes, vmem_limit_bytes=VMEM_LIMIT)


def _resident(shape):
    return pl.BlockSpec(shape, lambda *_: (0,) * len(shape), pipeline_mode=pl.Buffered(1))


def _rmsnorm(x, g):
    return x * lax.rsqrt(jnp.mean(x * x, axis=-1, keepdims=True) + NORM_EPS) * g


def _dot(a, b):
    return jnp.dot(a, b, preferred_element_type=F32)


def _split3(x):
    hi = x.astype(BF16)
    rest = x - hi.astype(F32)
    mid = rest.astype(BF16)
    return hi, mid, (rest - mid.astype(F32)).astype(BF16)


def _mod_kernel(c_ref, w_ref, b_ref, o_ref):
    c = c_ref[...]
    s = c * jax.nn.sigmoid(c)
    lhs = jnp.concatenate([p.astype(F32) for p in _split3(s)], axis=0).astype(BF16)
    acc = jnp.zeros((3 * MOD_ROWS, w_ref.shape[2]), F32)
    for piece in _split3(w_ref[0]):
        acc = acc + _dot(lhs, piece)
    o_ref[0] = acc[:MOD_ROWS] + acc[MOD_ROWS:2 * MOD_ROWS] + acc[2 * MOD_ROWS:] + b_ref[0]


def _modulation(cmat, ada_w, ada_b):
    depth, _, n = ada_w.shape
    tn = MOD_TILE
    return pl.pallas_call(
        _mod_kernel,
        grid=(depth, n // tn),
        in_specs=[
            pl.BlockSpec((MOD_ROWS, D_MODEL), lambda i, j: (0, 0)),
            pl.BlockSpec((1, D_MODEL, tn), lambda i, j: (i, 0, j)),
            pl.BlockSpec((1, 1, tn), lambda i, j: (i, 0, j)),
        ],
        out_specs=pl.BlockSpec((1, MOD_ROWS, tn), lambda i, j: (i, 0, j)),
        out_shape=jax.ShapeDtypeStruct((depth, MOD_ROWS, n), F32),
        compiler_params=_params(2),
        name="adaln_modulation",
    )(cmat, ada_w, ada_b.reshape(depth, 1, n))


V_START = 2 * QK_W
O_START = V_START + V_W
G_START = O_START + V_W


KG_ROW = QK_W + 2 * V_W


def _front_kernel(with_qo, n_riders, x_ref, g_ref, sh_ref, sc_ref, wt32_ref, *rest):
    riders_in, rest = rest[:n_riders], rest[n_riders:]
    *outs, w_ref = rest
    outs, riders_out = outs[:len(outs) - n_riders], outs[len(outs) - n_riders:]
    for src, dst in zip(riders_in, riders_out):
        dst[...] = src[...].astype(BF16)
    nt = (((1,), (1,)), ((), ()))

    @pl.when(pl.program_id(0) == 0)
    def _():
        w_ref[:QK_W, :] = wt32_ref[:QK_W, :].astype(BF16)
        w_ref[QK_W:KG_ROW, :] = wt32_ref[V_START:G_START, :].astype(BF16)
        w_ref[KG_ROW:KG_ROW + QK_W, :] = wt32_ref[QK_W:V_START, :].astype(BF16)
        new = lax.broadcasted_iota(jnp.int32, (GATE_W, GATE_W), 0)
        old = lax.broadcasted_iota(jnp.int32, (GATE_W, GATE_W), 1)
        kind = new % 8 // 2
        gate_type = jnp.where(kind < 2, 2 * kind + 1, 2 * (kind - 2))
        perm = (old == gate_type * HEADS + 2 * (new // 8) + new % 2).astype(BF16)
        w_ref[KG_ROW + QK_W:, :] = _dot(perm, wt32_ref[G_START:, :].astype(BF16)).astype(BF16)

    def normed(s):
        x = x_ref[s * SUB_ROWS:(s + 1) * SUB_ROWS, :]
        return (_rmsnorm(x, g_ref[...]) * (1.0 + sc_ref[0]) + sh_ref[0]).astype(BF16)

    if with_qo:
        q_ref, kt_ref, v_ref, o_ref, gr_ref = outs
    else:
        kt_ref, v_ref, gr_ref = outs

    n_sub = x_ref.shape[0] // SUB_ROWS
    xb = normed(0)
    for s in range(n_sub):
        rows = slice(s * SUB_ROWS, (s + 1) * SUB_ROWS)

        def token_major(w_rows, xb=xb):
            return lax.dot_general(xb, w_ref[w_rows, :], nt, preferred_element_type=F32)

        if with_qo:
            q_ref[rows, :] = (token_major(slice(0, QK_W)) * (DQK ** -0.5)).astype(BF16)
        xb_next = normed(s + 1) if s + 1 < n_sub else None
        if with_qo:
            o_ref[rows, :] = token_major(slice(QK_W + V_W, KG_ROW))
        kg = lax.dot_general(w_ref[KG_ROW:, :], xb, nt, preferred_element_type=F32)
        kt_ref[:, rows] = kg[:QK_W].astype(BF16)
        gr_ref[:, rows] = kg[QK_W:]
        v_ref[rows, :] = token_major(slice(QK_W, QK_W + V_W)).astype(BF16)
        xb = xb_next


def _front(x2d, norm_g, sh, sc, w_in_t, rows_per_mod, with_qo, riders=()):
    n = x2d.shape[0]
    tm = min(ROW_TILE, n)
    steps = n // tm
    assert all(r.shape[0] % steps == 0 for r in riders)
    tiles_per_mod = rows_per_mod // tm
    row = lambda i: (i, 0)
    mod = lambda i: (i // tiles_per_mod, 0, 0)
    rider_specs = [pl.BlockSpec((r.shape[0] // steps, r.shape[1]), row) for r in riders]
    out_shape, out_specs = [], []
    if with_qo:
        out_shape.append(jax.ShapeDtypeStruct((n, QK_W), BF16))
        out_specs.append(pl.BlockSpec((tm, QK_W), row))
    out_shape.append(jax.ShapeDtypeStruct((QK_W, n), BF16))
    out_specs.append(pl.BlockSpec((QK_W, tm), lambda i: (0, i)))
    out_shape.append(jax.ShapeDtypeStruct((n, V_W), BF16))
    out_specs.append(pl.BlockSpec((tm, V_W), row))
    if with_qo:
        out_shape.append(jax.ShapeDtypeStruct((n, V_W), F32))
        out_specs.append(pl.BlockSpec((tm, V_W), row))
    out_shape.append(jax.ShapeDtypeStruct((GATE_W, n), F32))
    out_specs.append(pl.BlockSpec((GATE_W, tm), lambda i: (0, i)))
    out_shape += [jax.ShapeDtypeStruct(r.shape, BF16) for r in riders]
    out_specs += rider_specs
    return pl.pallas_call(
        functools.partial(_front_kernel, with_qo, len(riders)),
        grid=(steps,),
        in_specs=[
            pl.BlockSpec((tm, D_MODEL), row),
            _resident((1, D_MODEL)),
            pl.BlockSpec((1, 1, D_MODEL), mod),
            pl.BlockSpec((1, 1, D_MODEL), mod),
            _resident(w_in_t.shape),
        ] + rider_specs,
        out_specs=out_specs,
        out_shape=out_shape,
        scratch_shapes=[pltpu.VMEM(w_in_t.shape, BF16)],
        compiler_params=_params(1),
        name="mlstm_front_qo" if with_qo else "mlstm_front_ctx",
    )(x2d, norm_g.reshape(1, D_MODEL), sh, sc, w_in_t, *riders)


def _softcap(g):
    return GATE_SOFTCAP * jnp.tanh(g * (1.0 / GATE_SOFTCAP))


def _log_sigmoid(g):
    return jnp.minimum(g, 0.0) - jnp.log1p(jnp.exp(-jnp.abs(g)))


PAIRS = HEADS // 2
BCOL_W = 16
GCOL_W = PAIRS * BCOL_W


def _gate_kernel(gr_ref, br_ref, col_ref, row_ref):
    t = gr_ref.shape[1]
    r = lax.broadcasted_iota(jnp.int32, (CHUNK, CHUNK), 0)
    c = lax.broadcasted_iota(jnp.int32, (CHUNK, CHUNK), 1)
    lower = (c <= r).astype(BF16)
    upper = (c >= r).astype(BF16)
    ones = jnp.ones((CHUNK, CHUNK), BF16)
    row_rhs = jnp.concatenate([upper, lower, ones], axis=1)
    col_lhs = jnp.concatenate([lower, upper], axis=1)
    src = lax.broadcasted_iota(jnp.int32, (3 * GATE_W, GCOL_W), 0)
    dst = lax.broadcasted_iota(jnp.int32, (3 * GATE_W, GCOL_W), 1)
    gate = src % GATE_W
    place = ((gate % 8 < 4)
             & (dst == (gate // 8) * BCOL_W + 4 * (src // GATE_W) + gate % 8)).astype(BF16)
    lane_piece = lax.broadcasted_iota(jnp.int32, (CHUNK, 3 * GATE_W), 1) // GATE_W
    m = lax.broadcasted_iota(jnp.int32, (GATE_W, CHUNK), 0) % 8
    fwd_rows = m < 2
    bwd_rows = (m >= 2) & (m < 4)
    nt = (((1,), (1,)), ((), ()))

    gr = _softcap(gr_ref[...] + br_ref[...])
    li = pltpu.roll(gr, GATE_W - 4, axis=0)
    lf3 = _split3(_log_sigmoid(gr))

    chunks = [slice(k * CHUNK, (k + 1) * CHUNK) for k in range(t // CHUNK)]
    sums = []
    for sl in chunks:
        cum = jnp.zeros((CHUNK, 3 * GATE_W), F32)
        y = jnp.zeros((GATE_W, 3 * CHUNK), F32)
        for piece in lf3:
            x = piece[:, sl]
            zero = jnp.zeros_like(x)
            xd = jnp.concatenate([jnp.where(fwd_rows, x, zero), jnp.where(bwd_rows, x, zero)],
                                 axis=1)
            cum = cum + lax.dot_general(col_lhs, jnp.concatenate([xd, xd, xd], axis=0), nt,
                                        preferred_element_type=F32)
            y = y + _dot(x, row_rhs)
        sums.append((cum, y))
    for sl, (cum, y) in zip(chunks, sums):
        hi, mid, lo = [p.astype(F32) for p in _split3(cum * LOG2E)]
        pieces = jnp.where(lane_piece == 0, hi, jnp.where(lane_piece == 1, mid, lo))
        col_ref[sl, :] = _dot(pieces.astype(BF16), place).astype(BF16)
        total = y[:, 2 * CHUNK:]
        e = li[:, sl] - jnp.where(fwd_rows, y[:, :CHUNK], y[:, CHUNK:2 * CHUNK])
        used = fwd_rows | bwd_rows
        row_ref[0, :, sl] = jnp.where(used, e * LOG2E, 0.0)
        row_ref[1, :, sl] = jnp.where(used, jnp.exp(total + e), 0.0)
        row_ref[2, :, sl] = jnp.where(used, jnp.exp(total), 0.0)


def _gate_prep(gates_raw, bias):
    n = gates_raw.shape[1]
    t = min(1024, n)
    return pl.pallas_call(
        _gate_kernel,
        grid=(n // t,),
        in_specs=[
            pl.BlockSpec((GATE_W, t), lambda i: (0, i)),
            _resident((GATE_W, 1)),
        ],
        out_specs=[
            pl.BlockSpec((t, GCOL_W), lambda i: (i, 0)),
            pl.BlockSpec((3, GATE_W, t), lambda i: (0, 0, i)),
        ],
        out_shape=[
            jax.ShapeDtypeStruct((n, GCOL_W), BF16),
            jax.ShapeDtypeStruct((3, GATE_W, n), F32),
        ],
        compiler_params=_params(1),
        name="mlstm_gate_prep",
    )(gates_raw, bias)


SCAN_SPLITS = 2
STATE_UNROLL = 8
OUTPUT_UNROLL = 8


def _scan_kernel(n_riders, q_ref, kt_ref, v_ref, gcol_ref, grow_ref, ktc_ref, vc_ref, growc_ref,
                 *rest):
    riders_in, rest = rest[:n_riders], rest[n_riders:]
    h_ref, *riders_out = rest[:1 + n_riders]
    cst_ref, cs_ref = rest[1 + n_riders:]
    for src, dst in zip(riders_in, riders_out):
        dst[...] = src[...].astype(BF16)
    seq = q_ref.shape[0]
    nc = seq // CHUNK
    ncc = ktc_ref.shape[1] // CHUNK
    nc_part = nc // SCAN_SPLITS
    part = pl.program_id(2)
    r = lax.broadcasted_iota(jnp.int32, (CHUNK, CHUNK), 0)
    c = lax.broadcasted_iota(jnp.int32, (CHUNK, CHUNK), 1)
    masks = (r >= c, r <= c)
    left = c < DQK
    ones_blk = jnp.ones((CHUNK, DV), BF16)
    zeros_k = jnp.zeros((DQK, CHUNK), BF16)
    zeros_q = jnp.zeros((CHUNK, DQK), BF16)
    pair = pl.program_id(1)
    src = lax.broadcasted_iota(jnp.int32, (GCOL_W, 4 * CHUNK), 0)
    block = lax.broadcasted_iota(jnp.int32, (GCOL_W, 4 * CHUNK), 1) // CHUNK
    piece = src % BCOL_W
    sel = ((src // BCOL_W == pair) & (piece % 4 == block) & (piece < 12)).astype(BF16)

    def gate_row(ref, quantity, d, j, lanes):
        return ref[quantity, 2 * d + j:2 * d + j + 1, lanes]

    def state_increment(kt_j, v_j, w_row):
        ktw = (kt_j.astype(F32) * w_row).astype(BF16)
        return _dot(ktw, jnp.concatenate([v_j, ones_blk], axis=1))

    def decay2(a_row):
        return jnp.concatenate([a_row, a_row], axis=1)

    @pl.when(part == 0)
    def _():
        cst_ref[...] = jnp.zeros(cst_ref.shape, F32)
        for d in range(2):
            for step in range(ncc):
                k = step if d == 0 else ncc - 1 - step
                sl = slice(k * CHUNK, (k + 1) * CHUNK)
                ktp = ktc_ref[:, sl]
                vp = vc_ref[sl, :]
                for j in range(2):
                    inc = state_increment(ktp[j * DQK:(j + 1) * DQK], vp[:, j * DV:(j + 1) * DV],
                                          gate_row(growc_ref, 1, d, j, sl))
                    cst_ref[d, j] = cst_ref[d, j] * decay2(gate_row(growc_ref, 2, d, j, sl)) + inc

        def state_step(i, carry):
            for d in range(2):
                k = i if d == 0 else nc - 1 - i
                rows = pl.ds(pl.multiple_of(k * CHUNK, CHUNK), CHUNK)
                ktp = kt_ref[:, rows]
                vp = v_ref[rows, :]
                for j in range(2):
                    cst = cst_ref[d, j]
                    cs_ref[d, k, j * DQK:(j + 1) * DQK, :] = cst.astype(BF16)
                    inc = state_increment(ktp[j * DQK:(j + 1) * DQK], vp[:, j * DV:(j + 1) * DV],
                                          gate_row(grow_ref, 1, d, j, rows))
                    cst_ref[d, j] = cst * decay2(gate_row(grow_ref, 2, d, j, rows)) + inc
            return carry

        lax.fori_loop(0, nc, state_step, 0, unroll=STATE_UNROLL)

    def decay_stage(k):
        rows = pl.ds(pl.multiple_of(k * CHUNK, CHUNK), CHUNK)
        qp = q_ref[rows, :]
        ktp = kt_ref[:, rows]
        bb_all = _dot(gcol_ref[rows, :], sel)
        kbd = jnp.concatenate([jnp.concatenate([ktp[:DQK], zeros_k], axis=1),
                               jnp.concatenate([zeros_k, ktp[DQK:]], axis=1)], axis=0)
        s2 = _dot(qp, kbd)
        qf = qp.astype(F32)
        lhs = []
        for d in range(2):
            bb = [bb_all[:, (2 * d + j) * CHUNK:(2 * d + j + 1) * CHUNK] for j in range(2)]
            qs = (qf * jnp.exp2(jnp.where(left, bb[0], bb[1]))).astype(BF16)
            for j in range(2):
                e_row = gate_row(grow_ref, 0, d, j, rows)
                dm = jnp.exp2(jnp.where(masks[d], bb[j] + e_row, NEG_BIG))
                p = (s2[:, j * CHUNK:(j + 1) * CHUNK] * dm).astype(BF16)
                qs_j = [qs[:, :DQK], zeros_q] if j == 0 else [zeros_q, qs[:, DQK:]]
                lhs.append(jnp.concatenate([p] + qs_j, axis=1))
        return lhs

    def value_stage(k, i, lhs):
        rows = pl.ds(pl.multiple_of(k * CHUNK, CHUNK), CHUNK)
        out_rows = pl.ds(pl.multiple_of(i * CHUNK, CHUNK), CHUNK)
        vp = v_ref[rows, :]
        for j in range(2):
            v1 = jnp.concatenate([vp[:, j * DV:(j + 1) * DV], ones_blk], axis=1)
            h = None
            for d in range(2):
                out = _dot(lhs[2 * d + j], jnp.concatenate([v1, cs_ref[d, k]], axis=0))
                hd = out[:, :DV] / jnp.maximum(jnp.abs(out[:, DV:]), 1.0)
                h = hd if d == 0 else h + hd
            h_ref[out_rows, j * DV:(j + 1) * DV] = h

    def output_trip(t, carry):
        i0 = t * OUTPUT_UNROLL
        staged = [decay_stage(part * nc_part + i0 + g) for g in range(OUTPUT_UNROLL)]
        for g in range(OUTPUT_UNROLL):
            value_stage(part * nc_part + i0 + g, i0 + g, staged[g])
        return carry

    lax.fori_loop(0, nc_part // OUTPUT_UNROLL, output_trip, 0)


def _scan(q, kt, v, gcol, grow, ktc, vc, growc, bsz, seq, ctx_len, riders=()):
    nc = seq // CHUNK
    steps = bsz * PAIRS * SCAN_SPLITS
    assert all(r.shape[0] % steps == 0 for r in riders)
    rider_specs = [pl.BlockSpec((r.shape[0] // steps, r.shape[1]),
                                lambda b, p, s: ((b * PAIRS + p) * SCAN_SPLITS + s, 0))
                   for r in riders]
    return pl.pallas_call(
        functools.partial(_scan_kernel, len(riders)),
        grid=(bsz, PAIRS, SCAN_SPLITS),
        in_specs=[
            pl.BlockSpec((seq, 2 * DQK), lambda b, p, s: (b, p)),
            pl.BlockSpec((2 * DQK, seq), lambda b, p, s: (p, b)),
            pl.BlockSpec((seq, 2 * DV), lambda b, p, s: (b, p)),
            pl.BlockSpec((seq, GCOL_W), lambda b, p, s: (b, 0)),
            pl.BlockSpec((3, 8, seq), lambda b, p, s: (0, p, b)),
            pl.BlockSpec((2 * DQK, ctx_len), lambda b, p, s: (p, b)),
            pl.BlockSpec((ctx_len, 2 * DV), lambda b, p, s: (b, p)),
            pl.BlockSpec((3, 8, ctx_len), lambda b, p, s: (0, p, b)),
        ] + rider_specs,
        out_specs=[pl.BlockSpec((seq // SCAN_SPLITS, 2 * DV),
                                lambda b, p, s: (b * SCAN_SPLITS + s, p))] + rider_specs,
        out_shape=[jax.ShapeDtypeStruct((bsz * seq, V_W), F32)]
        + [jax.ShapeDtypeStruct(r.shape, BF16) for r in riders],
        scratch_shapes=[pltpu.VMEM((2, 2, DQK, 2 * DV), F32),
                        pltpu.VMEM((2, nc, 2 * DQK, 2 * DV), BF16)],
        compiler_params=_params(3),
        name="mlstm_scan",
    )(q, kt, v, gcol, grow, ktc, vc, growc, *riders)


MLP_TILE = 1024
POST_TILE = 1024
SUB_ROWS = 256


def _mlp_pipeline(n_sub, mixer, finish, n2_ref, sh2_ref, sc2_ref, g2_ref, w1_ref, w2_ref):
    def normed(x1):
        return (_rmsnorm(x1, n2_ref[...]) * (1.0 + sc2_ref[0]) + sh2_ref[0]).astype(BF16)

    def slab(u, k):
        cols = slice(k * FF_TILE, (k + 1) * FF_TILE)
        hid = jnp.maximum(_dot(u, w1_ref[:, cols]), 0.0)
        return _dot((hid * hid).astype(BF16), w2_ref[cols, :])

    def advance(gen):
        try:
            next(gen)
            return gen, None
        except StopIteration as done:
            return None, done.value

    gen, x1 = mixer(0), None
    while gen is not None:
        gen, x1 = advance(gen)
    u = normed(x1)
    for s in range(n_sub):
        gen = mixer(s + 1) if s + 1 < n_sub else None
        x1_next = u_next = None
        acc = slab(u, 0)
        for k in range(1, D_FF // FF_TILE):
            if gen is not None:
                gen, x1_next = advance(gen)
                if gen is None:
                    u_next = normed(x1_next)
            acc = acc + slab(u, k)
        assert gen is None, "mixer has more phases than hidden slabs to hide them under"
        finish(s, x1 + g2_ref[0] * acc)
        x1, u = x1_next, u_next


def _mlstm_post_kernel(h_ref, o_ref, x_ref, ng_ref, wout_ref, g1_ref, n2_ref, sh2_ref, sc2_ref,
                       g2_ref, w1_ref, w2_ref, out_ref):
    def mixer(s):
        rows = slice(s * SUB_ROWS, (s + 1) * SUB_ROWS)
        gate = jax.nn.sigmoid(o_ref[rows, :])
        parts = []
        for k in range(HEADS):
            hk = h_ref[rows, k * DV:(k + 1) * DV]
            parts.append(hk * lax.rsqrt(jnp.mean(hk * hk, axis=-1, keepdims=True) + NORM_EPS))
        hn = jnp.concatenate(parts, axis=1) * ng_ref[...]
        y = (gate * hn).astype(BF16)
        return x_ref[rows, :] + g1_ref[0] * _dot(y, wout_ref[...])
        yield

    def finish(s, x2):
        out_ref[s * SUB_ROWS:(s + 1) * SUB_ROWS, :] = x2

    _mlp_pipeline(x_ref.shape[0] // SUB_ROWS, mixer, finish, n2_ref, sh2_ref, sc2_ref, g2_ref,
                  w1_ref, w2_ref)


def _mlstm_post(h, o, x2d, norm_g, w_out, g1, n2, sh2, sc2, g2, w1, w2, rows_per_mod):
    n = x2d.shape[0]
    tm = POST_TILE
    tiles_per_mod = rows_per_mod // tm
    row = pl.BlockSpec((tm, D_MODEL), lambda i: (i, 0))
    mod = pl.BlockSpec((1, 1, D_MODEL), lambda i: (i // tiles_per_mod, 0, 0))
    return pl.pallas_call(
        _mlstm_post_kernel,
        grid=(n // tm,),
        in_specs=[row, row, row, _resident((1, V_W)), _resident(w_out.shape), mod,
                  _resident((1, D_MODEL)), mod, mod, mod, _resident(w1.shape), _resident(w2.shape)],
        out_specs=row,
        out_shape=jax.ShapeDtypeStruct((n, D_MODEL), F32),
        compiler_params=_params(1),
        name="mlstm_post_mlp",
    )(h, o, x2d, norm_g.reshape(1, V_W), w_out, g1, n2.reshape(1, D_MODEL), sh2, sc2, g2, w1, w2)


def _pool_constants():
    t = np.arange(SUB_ROWS)
    pos = t % GRID_W
    bands, inv = [], []
    for window in POOL_WINDOWS:
        lo = np.clip(pos - window // 2, 0, GRID_W)
        hi = np.clip(pos - window // 2 + window, 0, GRID_W)
        same_group = t[:, None] // GRID_W == t[None, :] // GRID_W
        bands.append(same_group & (pos[None, :] >= lo[:, None]) & (pos[None, :] < hi[:, None]))
        inv.append(np.broadcast_to((1.0 / (hi - lo))[:, None], (SUB_ROWS, POOL_GW)))
    return (jnp.asarray(np.stack(bands), dtype=BF16), jnp.asarray(np.stack(inv), dtype=F32))


def _pool_layer_kernel(x_ref, n1_ref, sh1_ref, sc1_ref, g1_ref, band_ref, inv_ref, pw_ref, ps_ref,
                       n2_ref, sh2_ref, sc2_ref, g2_ref, w1_ref, w2_ref, fg_ref, out_ref):
    def mixer(s):
        x = x_ref[s * SUB_ROWS:(s + 1) * SUB_ROWS, :]
        xn = _rmsnorm(x, n1_ref[...]) * (1.0 + sc1_ref[0]) + sh1_ref[0]
        hi = xn.astype(BF16)
        lo = (xn - hi.astype(F32)).astype(BF16)
        ps = []
        for gi in range(len(POOL_WINDOWS)):
            cols = slice(gi * POOL_GW, (gi + 1) * POOL_GW)
            band = band_ref[gi]
            total = _dot(band, hi[:, cols]) + _dot(band, lo[:, cols])
            ps.append((total * inv_ref[gi] - xn[:, cols]).astype(BF16))
        yield
        ys = [_dot(p, pw_ref[gi]) for gi, p in enumerate(ps)]
        return x + g1_ref[0] * (jnp.concatenate(ys, axis=1) * ps_ref[...])

    def finish(s, x2):
        out_ref[s * SUB_ROWS:(s + 1) * SUB_ROWS, :] = _rmsnorm(x2, fg_ref[...])

    _mlp_pipeline(x_ref.shape[0] // SUB_ROWS, mixer, finish, n2_ref, sh2_ref, sc2_ref, g2_ref,
                  w1_ref, w2_ref)


def _pool_layer(x2d, n1, sh1, sc1, g1, pool_w, pool_scale, n2, sh2, sc2, g2, w1, w2,
                final_g, rows_per_mod):
    n = x2d.shape[0]
    tm = MLP_TILE
    tiles_per_mod = rows_per_mod // tm
    row = pl.BlockSpec((tm, D_MODEL), lambda i: (i, 0))
    mod = pl.BlockSpec((1, 1, D_MODEL), lambda i: (i // tiles_per_mod, 0, 0))
    vec = _resident((1, D_MODEL))
    bands, inv = _pool_constants()
    return pl.pallas_call(
        _pool_layer_kernel,
        grid=(n // tm,),
        in_specs=[row, vec, mod, mod, mod, _resident(bands.shape), _resident(inv.shape),
                  _resident(pool_w.shape), vec, vec, mod, mod, mod,
                  _resident(w1.shape), _resident(w2.shape), vec],
        out_specs=row,
        out_shape=jax.ShapeDtypeStruct((n, D_MODEL), F32),
        compiler_params=_params(1),
        name="pool_mlp_final",
    )(x2d, n1.reshape(1, D_MODEL), sh1, sc1, g1, bands, inv, pool_w,
      pool_scale.reshape(1, D_MODEL),
      n2.reshape(1, D_MODEL), sh2, sc2, g2, w1, w2, final_g.reshape(1, D_MODEL))


def kernel(x, c, ctx, c_ctx, ada_w, ada_b, norm1_g, norm2_g, mlstm_w_in, mlstm_gate_b,
           mlstm_norm_g, mlstm_w_out, pool_w, pool_scale, mlp_w1, mlp_w2, final_g):
    bsz, seq, dm = x.shape
    ctx_len = ctx.shape[1]
    assert dm == D_MODEL and seq % (2 * CHUNK) == 0 and ctx_len % CHUNK == 0
    assert seq % ROW_TILE == 0 and seq % MLP_TILE == 0 and seq % POST_TILE == 0
    assert (bsz * ctx_len) % SUB_ROWS == 0 and ROW_TILE % SUB_ROWS == 0 and bsz + 1 <= MOD_ROWS
    assert MLP_TILE % SUB_ROWS == 0 and SUB_ROWS % GRID_W == 0
    assert ada_w.shape[0] == 2 and mlstm_w_in.shape[0] == 1 and pool_w.shape[0] == 1

    cmat = jnp.concatenate(
        [c, c_ctx[None, :], jnp.zeros((MOD_ROWS - bsz - 1, dm), F32)], axis=0)
    mod = _modulation(cmat, ada_w, ada_b)

    def mod_rows(layer, which, rows):
        return mod[layer, rows, which * dm:(which + 1) * dm][:, None, :]

    lat = slice(0, bsz)
    cx = slice(bsz, bsz + 1)

    def pair_major(g):
        g4 = g.reshape(g.shape[:-1] + (4, PAIRS, 2))
        g4 = jnp.stack([g4[..., t, :, :] for t in (1, 3, 0, 2)], axis=-2)
        return g4.reshape(g.shape)

    w_in = mlstm_w_in[0].T
    gate_b = pair_major(mlstm_gate_b[0].reshape(GATE_W)).reshape(GATE_W, 1)

    x2d = x.reshape(bsz * seq, dm)
    q, kt, v, o, gates, w1_a, w2_a, w_out = _front(
        x2d, norm1_g[0], mod_rows(0, 0, lat), mod_rows(0, 1, lat), w_in, seq, True,
        (mlp_w1[0], mlp_w2[0], mlstm_w_out[0]))
    ktc, vc, gates_c = _front(
        ctx.reshape(bsz * ctx_len, dm), norm1_g[0], mod_rows(0, 0, cx), mod_rows(0, 1, cx),
        w_in, bsz * ctx_len, False)
    gcol, grow = _gate_prep(gates, gate_b)
    _, growc = _gate_prep(gates_c, gate_b)

    h, w1_b, w2_b = _scan(q, kt, v, gcol, grow, ktc, vc, growc, bsz, seq, ctx_len,
                          (mlp_w1[1], mlp_w2[1]))
    x2d = _mlstm_post(
        h, o, x2d, mlstm_norm_g[0], w_out, mod_rows(0, 2, lat),
        norm2_g[0], mod_rows(0, 3, lat), mod_rows(0, 4, lat), mod_rows(0, 5, lat),
        w1_a, w2_a, seq)

    out = _pool_layer(
        x2d, norm1_g[1], mod_rows(1, 0, lat), mod_rows(1, 1, lat), mod_rows(1, 2, lat),
        pool_w[0].astype(BF16), pool_scale[0], norm2_g[1], mod_rows(1, 3, lat),
        mod_rows(1, 4, lat), mod_rows(1, 5, lat), w1_b, w2_b, final_g, seq)
    return out.reshape(bsz, seq, dm)
```

```python
import functools

import jax
import jax.numpy as jnp
import numpy as np
from jax import lax
from jax.experimental import pallas as pl
from jax.experimental.pallas import tpu as pltpu

F32 = jnp.float32
BF16 = jnp.bfloat16

D_MODEL = 1024
HEADS = 8
DV = 128
DQK = 64
QK_W = HEADS * DQK
V_W = HEADS * DV
GATE_W = 4 * HEADS
D_FF = 4 * D_MODEL
NORM_EPS = 1e-6
GATE_SOFTCAP = 15.0
GRID_W = 64
POOL_WINDOWS = (2, 4, 8, 16)
POOL_GW = D_MODEL // len(POOL_WINDOWS)

LANES = 128
CHUNK = LANES
MOD_ROWS = 8
MOD_TILE = 1536
VMEM_LIMIT = 62 * 1024 * 1024
ROW_TILE = 1024
FF_TILE = 1024
NEG_BIG = -1e30
LOG2E = 1.4426950408889634


def _params(n_axes):
    return pltpu.CompilerParams(
        dimension_semantics=("arbitrary",) * n_axes, vmem_limit_bytes=VMEM_LIMIT)


def _resident(shape):
    return pl.BlockSpec(shape, lambda *_: (0,) * len(shape), pipeline_mode=pl.Buffered(1))


def _rider_specs(riders, steps, step_of):
    in_specs, out_specs, out_shapes = [], [], []
    for w, layer, depth in riders:
        rows = w.shape[0] // depth
        assert rows % steps == 0
        block = (rows // steps, w.shape[1])
        in_specs.append(pl.BlockSpec(
            block, lambda *g, layer=layer: (layer * steps + step_of(*g), 0)))
        out_specs.append(pl.BlockSpec(block, lambda *g: (step_of(*g), 0)))
        out_shapes.append(jax.ShapeDtypeStruct((rows, w.shape[1]), BF16))
    return in_specs, out_specs, out_shapes


def _rmsnorm(x, g):
    return x * lax.rsqrt(jnp.mean(x * x, axis=-1, keepdims=True) + NORM_EPS) * g


def _dot(a, b):
    return jnp.dot(a, b, preferred_element_type=F32)


def _split3(x):
    hi = x.astype(BF16)
    rest = x - hi.astype(F32)
    mid = rest.astype(BF16)
    return hi, mid, (rest - mid.astype(F32)).astype(BF16)


def _mod_kernel(c_ref, w_ref, b_ref, o_ref):
    c = c_ref[...]
    s = c * jax.nn.sigmoid(c)
    lhs = jnp.concatenate([p.astype(F32) for p in _split3(s)], axis=0).astype(BF16)
    acc = jnp.zeros((3 * MOD_ROWS, w_ref.shape[2]), F32)
    for piece in _split3(w_ref[0]):
        acc = acc + _dot(lhs, piece)
    o_ref[0] = acc[:MOD_ROWS] + acc[MOD_ROWS:2 * MOD_ROWS] + acc[2 * MOD_ROWS:] + b_ref[0]


def _modulation(cmat, ada_w, ada_b):
    depth, _, n = ada_w.shape
    tn = MOD_TILE
    return pl.pallas_call(
        _mod_kernel,
        grid=(depth, n // tn),
        in_specs=[
            pl.BlockSpec((MOD_ROWS, D_MODEL), lambda i, j: (0, 0)),
            pl.BlockSpec((1, D_MODEL, tn), lambda i, j: (i, 0, j)),
            pl.BlockSpec((1, 1, tn), lambda i, j: (i, 0, j)),
        ],
        out_specs=pl.BlockSpec((1, MOD_ROWS, tn), lambda i, j: (i, 0, j)),
        out_shape=jax.ShapeDtypeStruct((depth, MOD_ROWS, n), F32),
        compiler_params=_params(2),
        name="adaln_modulation",
    )(cmat, ada_w, ada_b.reshape(depth, 1, n))


V_START = 2 * QK_W
O_START = V_START + V_W
G_START = O_START + V_W


KG_ROW = QK_W + 2 * V_W


def _front_kernel(with_qo, n_riders, x_ref, g_ref, sh_ref, sc_ref, wt32_ref, *rest):
    riders_in, rest = rest[:n_riders], rest[n_riders:]
    *outs, w_ref = rest
    outs, riders_out = outs[:len(outs) - n_riders], outs[len(outs) - n_riders:]
    for src, dst in zip(riders_in, riders_out):
        dst[...] = src[...].astype(BF16)
    nt = (((1,), (1,)), ((), ()))

    @pl.when(pl.program_id(0) == 0)
    def _():
        w_ref[:QK_W, :] = wt32_ref[:QK_W, :].astype(BF16)
        w_ref[QK_W:KG_ROW, :] = wt32_ref[V_START:G_START, :].astype(BF16)
        w_ref[KG_ROW:KG_ROW + QK_W, :] = wt32_ref[QK_W:V_START, :].astype(BF16)
        new = lax.broadcasted_iota(jnp.int32, (GATE_W, GATE_W), 0)
        old = lax.broadcasted_iota(jnp.int32, (GATE_W, GATE_W), 1)
        kind = new % 8 // 2
        gate_type = jnp.where(kind < 2, 2 * kind + 1, 2 * (kind - 2))
        perm = (old == gate_type * HEADS + 2 * (new // 8) + new % 2).astype(BF16)
        w_ref[KG_ROW + QK_W:, :] = _dot(perm, wt32_ref[G_START:, :].astype(BF16)).astype(BF16)

    def normed(s):
        x = x_ref[s * SUB_ROWS:(s + 1) * SUB_ROWS, :]
        return (_rmsnorm(x, g_ref[...]) * (1.0 + sc_ref[0]) + sh_ref[0]).astype(BF16)

    if with_qo:
        q_ref, kt_ref, v_ref, o_ref, gr_ref = outs
    else:
        kt_ref, v_ref, gr_ref = outs

    n_sub = x_ref.shape[0] // SUB_ROWS
    xb = normed(0)
    for s in range(n_sub):
        rows = slice(s * SUB_ROWS, (s + 1) * SUB_ROWS)

        def token_major(w_rows, xb=xb):
            return lax.dot_general(xb, w_ref[w_rows, :], nt, preferred_element_type=F32)

        if with_qo:
            q_ref[rows, :] = (token_major(slice(0, QK_W)) * (DQK ** -0.5)).astype(BF16)
        xb_next = normed(s + 1) if s + 1 < n_sub else None
        if with_qo:
            o_ref[rows, :] = token_major(slice(QK_W + V_W, KG_ROW))
        kg = lax.dot_general(w_ref[KG_ROW:, :], xb, nt, preferred_element_type=F32)
        kt_ref[:, rows] = kg[:QK_W].astype(BF16)
        gr_ref[:, rows] = kg[QK_W:]
        v_ref[rows, :] = token_major(slice(QK_W, QK_W + V_W)).astype(BF16)
        xb = xb_next


def _front(x2d, norm_g, sh, sc, w_in_t, rows_per_mod, with_qo, riders=()):
    n = x2d.shape[0]
    tm = min(ROW_TILE, n)
    steps = n // tm
    tiles_per_mod = rows_per_mod // tm
    row = lambda i: (i, 0)
    mod = lambda i: (i // tiles_per_mod, 0, 0)
    rider_in, rider_out, rider_shapes = _rider_specs(riders, steps, lambda i: i)
    out_shape, out_specs = [], []
    if with_qo:
        out_shape.append(jax.ShapeDtypeStruct((n, QK_W), BF16))
        out_specs.append(pl.BlockSpec((tm, QK_W), row))
    out_shape.append(jax.ShapeDtypeStruct((QK_W, n), BF16))
    out_specs.append(pl.BlockSpec((QK_W, tm), lambda i: (0, i)))
    out_shape.append(jax.ShapeDtypeStruct((n, V_W), BF16))
    out_specs.append(pl.BlockSpec((tm, V_W), row))
    if with_qo:
        out_shape.append(jax.ShapeDtypeStruct((n, V_W), F32))
        out_specs.append(pl.BlockSpec((tm, V_W), row))
    out_shape.append(jax.ShapeDtypeStruct((GATE_W, n), F32))
    out_specs.append(pl.BlockSpec((GATE_W, tm), lambda i: (0, i)))
    out_shape += rider_shapes
    out_specs += rider_out
    return pl.pallas_call(
        functools.partial(_front_kernel, with_qo, len(riders)),
        grid=(steps,),
        in_specs=[
            pl.BlockSpec((tm, D_MODEL), row),
            _resident((1, D_MODEL)),
            pl.BlockSpec((1, 1, D_MODEL), mod),
            pl.BlockSpec((1, 1, D_MODEL), mod),
            _resident(w_in_t.shape),
        ] + rider_in,
        out_specs=out_specs,
        out_shape=out_shape,
        scratch_shapes=[pltpu.VMEM(w_in_t.shape, BF16)],
        compiler_params=_params(1),
        name="mlstm_front_qo" if with_qo else "mlstm_front_ctx",
    )(x2d, norm_g.reshape(1, D_MODEL), sh, sc, w_in_t, *[r[0] for r in riders])


def _softcap(g):
    return GATE_SOFTCAP * jnp.tanh(g * (1.0 / GATE_SOFTCAP))


def _log_sigmoid(g):
    return jnp.minimum(g, 0.0) - jnp.log1p(jnp.exp(-jnp.abs(g)))


PAIRS = HEADS // 2
BCOL_W = 16
GCOL_W = PAIRS * BCOL_W


def _gate_kernel(gr_ref, br_ref, col_ref, row_ref):
    t = gr_ref.shape[1]
    r = lax.broadcasted_iota(jnp.int32, (CHUNK, CHUNK), 0)
    c = lax.broadcasted_iota(jnp.int32, (CHUNK, CHUNK), 1)
    lower = (c <= r).astype(BF16)
    upper = (c >= r).astype(BF16)
    ones = jnp.ones((CHUNK, CHUNK), BF16)
    row_rhs = jnp.concatenate([upper, lower, ones], axis=1)
    col_lhs = jnp.concatenate([lower, upper], axis=1)
    src = lax.broadcasted_iota(jnp.int32, (3 * GATE_W, GCOL_W), 0)
    dst = lax.broadcasted_iota(jnp.int32, (3 * GATE_W, GCOL_W), 1)
    gate = src % GATE_W
    place = ((gate % 8 < 4)
             & (dst == (gate // 8) * BCOL_W + 4 * (src // GATE_W) + gate % 8)).astype(BF16)
    lane_piece = lax.broadcasted_iota(jnp.int32, (CHUNK, 3 * GATE_W), 1) // GATE_W
    m = lax.broadcasted_iota(jnp.int32, (GATE_W, CHUNK), 0) % 8
    fwd_rows = m < 2
    bwd_rows = (m >= 2) & (m < 4)
    nt = (((1,), (1,)), ((), ()))

    gr = _softcap(gr_ref[...] + br_ref[...])
    li = pltpu.roll(gr, GATE_W - 4, axis=0)
    lf3 = _split3(_log_sigmoid(gr))

    chunks = [slice(k * CHUNK, (k + 1) * CHUNK) for k in range(t // CHUNK)]
    sums = []
    for sl in chunks:
        cum = jnp.zeros((CHUNK, 3 * GATE_W), F32)
        y = jnp.zeros((GATE_W, 3 * CHUNK), F32)
        for piece in lf3:
            x = piece[:, sl]
            zero = jnp.zeros_like(x)
            xd = jnp.concatenate([jnp.where(fwd_rows, x, zero), jnp.where(bwd_rows, x, zero)],
                                 axis=1)
            cum = cum + lax.dot_general(col_lhs, jnp.concatenate([xd, xd, xd], axis=0), nt,
                                        preferred_element_type=F32)
            y = y + _dot(x, row_rhs)
        sums.append((cum, y))
    for sl, (cum, y) in zip(chunks, sums):
        hi, mid, lo = [p.astype(F32) for p in _split3(cum * LOG2E)]
        pieces = jnp.where(lane_piece == 0, hi, jnp.where(lane_piece == 1, mid, lo))
        col_ref[sl, :] = _dot(pieces.astype(BF16), place).astype(BF16)
        total = y[:, 2 * CHUNK:]
        e = li[:, sl] - jnp.where(fwd_rows, y[:, :CHUNK], y[:, CHUNK:2 * CHUNK])
        used = fwd_rows | bwd_rows
        row_ref[0, :, sl] = jnp.where(used, e * LOG2E, 0.0)
        row_ref[1, :, sl] = jnp.where(used, jnp.exp(total + e), 0.0)
        row_ref[2, :, sl] = jnp.where(used, jnp.exp(total), 0.0)


def _gate_prep(gates_raw, bias):
    n = gates_raw.shape[1]
    t = min(1024, n)
    return pl.pallas_call(
        _gate_kernel,
        grid=(n // t,),
        in_specs=[
            pl.BlockSpec((GATE_W, t), lambda i: (0, i)),
            _resident((GATE_W, 1)),
        ],
        out_specs=[
            pl.BlockSpec((t, GCOL_W), lambda i: (i, 0)),
            pl.BlockSpec((3, GATE_W, t), lambda i: (0, 0, i)),
        ],
        out_shape=[
            jax.ShapeDtypeStruct((n, GCOL_W), BF16),
            jax.ShapeDtypeStruct((3, GATE_W, n), F32),
        ],
        compiler_params=_params(1),
        name="mlstm_gate_prep",
    )(gates_raw, bias)


SCAN_SPLITS = 2
STATE_UNROLL = 8
OUTPUT_UNROLL = 8


def _scan_kernel(n_riders, q_ref, kt_ref, v_ref, gcol_ref, grow_ref, ktc_ref, vc_ref, growc_ref,
                 *rest):
    riders_in, rest = rest[:n_riders], rest[n_riders:]
    h_ref, *riders_out = rest[:1 + n_riders]
    cst_ref, cs_ref = rest[1 + n_riders:]
    for src, dst in zip(riders_in, riders_out):
        dst[...] = src[...].astype(BF16)
    seq = q_ref.shape[0]
    nc = seq // CHUNK
    ncc = ktc_ref.shape[1] // CHUNK
    nc_part = nc // SCAN_SPLITS
    part = pl.program_id(2)
    r = lax.broadcasted_iota(jnp.int32, (CHUNK, CHUNK), 0)
    c = lax.broadcasted_iota(jnp.int32, (CHUNK, CHUNK), 1)
    masks = (r >= c, r <= c)
    left = c < DQK
    ones_blk = jnp.ones((CHUNK, DV), BF16)
    zeros_k = jnp.zeros((DQK, CHUNK), BF16)
    zeros_q = jnp.zeros((CHUNK, DQK), BF16)
    pair = pl.program_id(1)
    src = lax.broadcasted_iota(jnp.int32, (GCOL_W, 4 * CHUNK), 0)
    block = lax.broadcasted_iota(jnp.int32, (GCOL_W, 4 * CHUNK), 1) // CHUNK
    piece = src % BCOL_W
    sel = ((src // BCOL_W == pair) & (piece % 4 == block) & (piece < 12)).astype(BF16)

    def gate_row(ref, quantity, d, j, lanes):
        return ref[quantity, 2 * d + j:2 * d + j + 1, lanes]

    def state_increment(kt_j, v_j, w_row):
        ktw = (kt_j.astype(F32) * w_row).astype(BF16)
        return _dot(ktw, jnp.concatenate([v_j, ones_blk], axis=1))

    def decay2(a_row):
        return jnp.concatenate([a_row, a_row], axis=1)

    @pl.when(part == 0)
    def _():
        cst_ref[...] = jnp.zeros(cst_ref.shape, F32)
        for d in range(2):
            for step in range(ncc):
                k = step if d == 0 else ncc - 1 - step
                sl = slice(k * CHUNK, (k + 1) * CHUNK)
                ktp = ktc_ref[:, sl]
                vp = vc_ref[sl, :]
                for j in range(2):
                    inc = state_increment(ktp[j * DQK:(j + 1) * DQK], vp[:, j * DV:(j + 1) * DV],
                                          gate_row(growc_ref, 1, d, j, sl))
                    cst_ref[d, j] = cst_ref[d, j] * decay2(gate_row(growc_ref, 2, d, j, sl)) + inc

        def state_step(i, carry):
            for d in range(2):
                k = i if d == 0 else nc - 1 - i
                rows = pl.ds(pl.multiple_of(k * CHUNK, CHUNK), CHUNK)
                ktp = kt_ref[:, rows]
                vp = v_ref[rows, :]
                for j in range(2):
                    cst = cst_ref[d, j]
                    cs_ref[d, k, j * DQK:(j + 1) * DQK, :] = cst.astype(BF16)
                    inc = state_increment(ktp[j * DQK:(j + 1) * DQK], vp[:, j * DV:(j + 1) * DV],
                                          gate_row(grow_ref, 1, d, j, rows))
                    cst_ref[d, j] = cst * decay2(gate_row(grow_ref, 2, d, j, rows)) + inc
            return carry

        lax.fori_loop(0, nc, state_step, 0, unroll=STATE_UNROLL)

    def decay_stage(k):
        rows = pl.ds(pl.multiple_of(k * CHUNK, CHUNK), CHUNK)
        qp = q_ref[rows, :]
        ktp = kt_ref[:, rows]
        bb_all = _dot(gcol_ref[rows, :], sel)
        kbd = jnp.concatenate([jnp.concatenate([ktp[:DQK], zeros_k], axis=1),
                               jnp.concatenate([zeros_k, ktp[DQK:]], axis=1)], axis=0)
        s2 = _dot(qp, kbd)
        qf = qp.astype(F32)
        lhs = []
        for d in range(2):
            bb = [bb_all[:, (2 * d + j) * CHUNK:(2 * d + j + 1) * CHUNK] for j in range(2)]
            qs = (qf * jnp.exp2(jnp.where(left, bb[0], bb[1]))).astype(BF16)
            for j in range(2):
                e_row = gate_row(grow_ref, 0, d, j, rows)
                dm = jnp.exp2(jnp.where(masks[d], bb[j] + e_row, NEG_BIG))
                p = (s2[:, j * CHUNK:(j + 1) * CHUNK] * dm).astype(BF16)
                qs_j = [qs[:, :DQK], zeros_q] if j == 0 else [zeros_q, qs[:, DQK:]]
                lhs.append(jnp.concatenate([p] + qs_j, axis=1))
        return lhs

    def value_stage(k, i, lhs):
        rows = pl.ds(pl.multiple_of(k * CHUNK, CHUNK), CHUNK)
        out_rows = pl.ds(pl.multiple_of(i * CHUNK, CHUNK), CHUNK)
        vp = v_ref[rows, :]
        for j in range(2):
            v1 = jnp.concatenate([vp[:, j * DV:(j + 1) * DV], ones_blk], axis=1)
            h = None
            for d in range(2):
                out = _dot(lhs[2 * d + j], jnp.concatenate([v1, cs_ref[d, k]], axis=0))
                hd = out[:, :DV] / jnp.maximum(jnp.abs(out[:, DV:]), 1.0)
                h = hd if d == 0 else h + hd
            h_ref[out_rows, j * DV:(j + 1) * DV] = h

    def output_trip(t, carry):
        i0 = t * OUTPUT_UNROLL
        staged = [decay_stage(part * nc_part + i0 + g) for g in range(OUTPUT_UNROLL)]
        for g in range(OUTPUT_UNROLL):
            value_stage(part * nc_part + i0 + g, i0 + g, staged[g])
        return carry

    lax.fori_loop(0, nc_part // OUTPUT_UNROLL, output_trip, 0)


def _scan(q, kt, v, gcol, grow, ktc, vc, growc, bsz, seq, ctx_len, riders=()):
    nc = seq // CHUNK
    rider_in, rider_out, rider_shapes = _rider_specs(
        riders, bsz * PAIRS * SCAN_SPLITS, lambda b, p, s: (b * PAIRS + p) * SCAN_SPLITS + s)
    return pl.pallas_call(
        functools.partial(_scan_kernel, len(riders)),
        grid=(bsz, PAIRS, SCAN_SPLITS),
        in_specs=[
            pl.BlockSpec((seq, 2 * DQK), lambda b, p, s: (b, p)),
            pl.BlockSpec((2 * DQK, seq), lambda b, p, s: (p, b)),
            pl.BlockSpec((seq, 2 * DV), lambda b, p, s: (b, p)),
            pl.BlockSpec((seq, GCOL_W), lambda b, p, s: (b, 0)),
            pl.BlockSpec((3, 8, seq), lambda b, p, s: (0, p, b)),
            pl.BlockSpec((2 * DQK, ctx_len), lambda b, p, s: (p, b)),
            pl.BlockSpec((ctx_len, 2 * DV), lambda b, p, s: (b, p)),
            pl.BlockSpec((3, 8, ctx_len), lambda b, p, s: (0, p, b)),
        ] + rider_in,
        out_specs=[pl.BlockSpec((seq // SCAN_SPLITS, 2 * DV),
                                lambda b, p, s: (b * SCAN_SPLITS + s, p))] + rider_out,
        out_shape=[jax.ShapeDtypeStruct((bsz * seq, V_W), F32)] + rider_shapes,
        scratch_shapes=[pltpu.VMEM((2, 2, DQK, 2 * DV), F32),
                        pltpu.VMEM((2, nc, 2 * DQK, 2 * DV), BF16)],
        compiler_params=_params(3),
        name="mlstm_scan",
    )(q, kt, v, gcol, grow, ktc, vc, growc, *[r[0] for r in riders])


MLP_TILE = 1024
POST_TILE = 1024
SUB_ROWS = 256


def _mlp_pipeline(n_sub, mixer, finish, n2_ref, sh2_ref, sc2_ref, g2_ref, w1_ref, w2_ref):
    def normed(x1):
        return (_rmsnorm(x1, n2_ref[...]) * (1.0 + sc2_ref[0]) + sh2_ref[0]).astype(BF16)

    def slab(u, k):
        cols = slice(k * FF_TILE, (k + 1) * FF_TILE)
        hid = jnp.maximum(_dot(u, w1_ref[:, cols]), 0.0)
        return _dot((hid * hid).astype(BF16), w2_ref[cols, :])

    def advance(gen):
        try:
            next(gen)
            return gen, None
        except StopIteration as done:
            return None, done.value

    gen, x1 = mixer(0), None
    while gen is not None:
        gen, x1 = advance(gen)
    u = normed(x1)
    for s in range(n_sub):
        gen = mixer(s + 1) if s + 1 < n_sub else None
        x1_next = u_next = None
        acc = slab(u, 0)
        for k in range(1, D_FF // FF_TILE):
            if gen is not None:
                gen, x1_next = advance(gen)
                if gen is None:
                    u_next = normed(x1_next)
            acc = acc + slab(u, k)
        assert gen is None, "mixer has more phases than hidden slabs to hide them under"
        finish(s, x1 + g2_ref[0] * acc)
        x1, u = x1_next, u_next


def _mlstm_post_kernel(h_ref, o_ref, x_ref, ng_ref, wout_ref, g1_ref, n2_ref, sh2_ref, sc2_ref,
                       g2_ref, w1_ref, w2_ref, out_ref):
    def mixer(s):
        rows = slice(s * SUB_ROWS, (s + 1) * SUB_ROWS)
        gate = jax.nn.sigmoid(o_ref[rows, :])
        parts = []
        for k in range(HEADS):
            hk = h_ref[rows, k * DV:(k + 1) * DV]
            parts.append(hk * lax.rsqrt(jnp.mean(hk * hk, axis=-1, keepdims=True) + NORM_EPS))
        hn = jnp.concatenate(parts, axis=1) * ng_ref[...]
        y = (gate * hn).astype(BF16)
        return x_ref[rows, :] + g1_ref[0] * _dot(y, wout_ref[...])
        yield

    def finish(s, x2):
        out_ref[s * SUB_ROWS:(s + 1) * SUB_ROWS, :] = x2

    _mlp_pipeline(x_ref.shape[0] // SUB_ROWS, mixer, finish, n2_ref, sh2_ref, sc2_ref, g2_ref,
                  w1_ref, w2_ref)


def _mlstm_post(h, o, x2d, norm_g, w_out, g1, n2, sh2, sc2, g2, w1, w2, rows_per_mod):
    n = x2d.shape[0]
    tm = POST_TILE
    tiles_per_mod = rows_per_mod // tm
    row = pl.BlockSpec((tm, D_MODEL), lambda i: (i, 0))
    mod = pl.BlockSpec((1, 1, D_MODEL), lambda i: (i // tiles_per_mod, 0, 0))
    return pl.pallas_call(
        _mlstm_post_kernel,
        grid=(n // tm,),
        in_specs=[row, row, row, _resident((1, V_W)), _resident(w_out.shape), mod,
                  _resident((1, D_MODEL)), mod, mod, mod, _resident(w1.shape), _resident(w2.shape)],
        out_specs=row,
        out_shape=jax.ShapeDtypeStruct((n, D_MODEL), F32),
        compiler_params=_params(1),
        name="mlstm_post_mlp",
    )(h, o, x2d, norm_g.reshape(1, V_W), w_out, g1, n2.reshape(1, D_MODEL), sh2, sc2, g2, w1, w2)


def _pool_constants():
    t = np.arange(SUB_ROWS)
    pos = t % GRID_W
    bands, inv = [], []
    for window in POOL_WINDOWS:
        lo = np.clip(pos - window // 2, 0, GRID_W)
        hi = np.clip(pos - window // 2 + window, 0, GRID_W)
        same_group = t[:, None] // GRID_W == t[None, :] // GRID_W
        bands.append(same_group & (pos[None, :] >= lo[:, None]) & (pos[None, :] < hi[:, None]))
        inv.append(np.broadcast_to((1.0 / (hi - lo))[:, None], (SUB_ROWS, POOL_GW)))
    return (jnp.asarray(np.stack(bands), dtype=BF16), jnp.asarray(np.stack(inv), dtype=F32))


def _pool_layer_kernel(x_ref, n1_ref, sh1_ref, sc1_ref, g1_ref, band_ref, inv_ref, pw_ref, ps_ref,
                       n2_ref, sh2_ref, sc2_ref, g2_ref, w1_ref, w2_ref, fg_ref, out_ref):
    def mixer(s):
        x = x_ref[s * SUB_ROWS:(s + 1) * SUB_ROWS, :]
        xn = _rmsnorm(x, n1_ref[...]) * (1.0 + sc1_ref[0]) + sh1_ref[0]
        hi = xn.astype(BF16)
        lo = (xn - hi.astype(F32)).astype(BF16)
        ps = []
        for gi in range(len(POOL_WINDOWS)):
            cols = slice(gi * POOL_GW, (gi + 1) * POOL_GW)
            band = band_ref[gi]
            total = _dot(band, hi[:, cols]) + _dot(band, lo[:, cols])
            ps.append((total * inv_ref[gi] - xn[:, cols]).astype(BF16))
        yield
        ys = [_dot(p, pw_ref[gi]) for gi, p in enumerate(ps)]
        return x + g1_ref[0] * (jnp.concatenate(ys, axis=1) * ps_ref[...])

    def finish(s, x2):
        out_ref[s * SUB_ROWS:(s + 1) * SUB_ROWS, :] = _rmsnorm(x2, fg_ref[...])

    _mlp_pipeline(x_ref.shape[0] // SUB_ROWS, mixer, finish, n2_ref, sh2_ref, sc2_ref, g2_ref,
                  w1_ref, w2_ref)


def _pool_layer(x2d, n1, sh1, sc1, g1, pool_w, pool_scale, n2, sh2, sc2, g2, w1, w2,
                final_g, rows_per_mod):
    n = x2d.shape[0]
    tm = MLP_TILE
    tiles_per_mod = rows_per_mod // tm
    row = pl.BlockSpec((tm, D_MODEL), lambda i: (i, 0))
    mod = pl.BlockSpec((1, 1, D_MODEL), lambda i: (i // tiles_per_mod, 0, 0))
    vec = _resident((1, D_MODEL))
    bands, inv = _pool_constants()
    return pl.pallas_call(
        _pool_layer_kernel,
        grid=(n // tm,),
        in_specs=[row, vec, mod, mod, mod, _resident(bands.shape), _resident(inv.shape),
                  _resident(pool_w.shape), vec, vec, mod, mod, mod,
                  _resident(w1.shape), _resident(w2.shape), vec],
        out_specs=row,
        out_shape=jax.ShapeDtypeStruct((n, D_MODEL), F32),
        compiler_params=_params(1),
        name="pool_mlp_final",
    )(x2d, n1.reshape(1, D_MODEL), sh1, sc1, g1, bands, inv, pool_w,
      pool_scale.reshape(1, D_MODEL),
      n2.reshape(1, D_MODEL), sh2, sc2, g2, w1, w2, final_g.reshape(1, D_MODEL))


def kernel(x, c, ctx, c_ctx, ada_w, ada_b, norm1_g, norm2_g, mlstm_w_in, mlstm_gate_b,
           mlstm_norm_g, mlstm_w_out, pool_w, pool_scale, mlp_w1, mlp_w2, final_g):
    bsz, seq, dm = x.shape
    ctx_len = ctx.shape[1]
    assert dm == D_MODEL and seq % (2 * CHUNK) == 0 and ctx_len % CHUNK == 0
    assert seq % ROW_TILE == 0 and seq % MLP_TILE == 0 and seq % POST_TILE == 0
    assert (bsz * ctx_len) % SUB_ROWS == 0 and ROW_TILE % SUB_ROWS == 0 and bsz + 1 <= MOD_ROWS
    assert MLP_TILE % SUB_ROWS == 0 and SUB_ROWS % GRID_W == 0
    assert ada_w.shape[0] == 2 and mlstm_w_in.shape[0] == 1 and pool_w.shape[0] == 1

    cmat = jnp.concatenate(
        [c, c_ctx[None, :], jnp.zeros((MOD_ROWS - bsz - 1, dm), F32)], axis=0)
    mod = _modulation(cmat, ada_w, ada_b)

    def mod_rows(layer, which, rows):
        return mod[layer, rows, which * dm:(which + 1) * dm][:, None, :]

    lat = slice(0, bsz)
    cx = slice(bsz, bsz + 1)

    def pair_major(g):
        g4 = g.reshape(g.shape[:-1] + (4, PAIRS, 2))
        g4 = jnp.stack([g4[..., t, :, :] for t in (1, 3, 0, 2)], axis=-2)
        return g4.reshape(g.shape)

    w_in = mlstm_w_in[0].T
    gate_b = pair_major(mlstm_gate_b[0].reshape(GATE_W)).reshape(GATE_W, 1)

    x2d = x.reshape(bsz * seq, dm)
    depth = mlp_w1.shape[0]
    w1_all = mlp_w1.reshape(depth * dm, D_FF)
    w2_all = mlp_w2.reshape(depth * D_FF, dm)
    q, kt, v, o, gates, w1_a, w2_a, w_out = _front(
        x2d, norm1_g[0], mod_rows(0, 0, lat), mod_rows(0, 1, lat), w_in, seq, True,
        ((w1_all, 0, depth), (w2_all, 0, depth), (mlstm_w_out.reshape(dm, dm), 0, 1)))
    ktc, vc, gates_c = _front(
        ctx.reshape(bsz * ctx_len, dm), norm1_g[0], mod_rows(0, 0, cx), mod_rows(0, 1, cx),
        w_in, bsz * ctx_len, False)
    gcol, grow = _gate_prep(gates, gate_b)
    _, growc = _gate_prep(gates_c, gate_b)

    h, w1_b, w2_b = _scan(q, kt, v, gcol, grow, ktc, vc, growc, bsz, seq, ctx_len,
                          ((w1_all, 1, depth), (w2_all, 1, depth)))
    x2d = _mlstm_post(
        h, o, x2d, mlstm_norm_g[0], w_out, mod_rows(0, 2, lat),
        norm2_g[0], mod_rows(0, 3, lat), mod_rows(0, 4, lat), mod_rows(0, 5, lat),
        w1_a, w2_a, seq)

    out = _pool_layer(
        x2d, norm1_g[1], mod_rows(1, 0, lat), mod_rows(1, 1, lat), mod_rows(1, 2, lat),
        pool_w[0].astype(BF16), pool_scale[0], norm2_g[1], mod_rows(1, 3, lat),
        mod_rows(1, 4, lat), mod_rows(1, 5, lat), w1_b, w2_b, final_g, seq)
    return out.reshape(bsz, seq, dm)
```

```python
import functools

import jax
import jax.numpy as jnp
import numpy as np
from jax import lax
from jax.experimental import pallas as pl
from jax.experimental.pallas import tpu as pltpu

F32 = jnp.float32
BF16 = jnp.bfloat16

D_MODEL = 1024
HEADS = 8
DV = 128
DQK = 64
QK_W = HEADS * DQK
V_W = HEADS * DV
GATE_W = 4 * HEADS
D_FF = 4 * D_MODEL
NORM_EPS = 1e-6
GATE_SOFTCAP = 15.0
GRID_W = 64
POOL_WINDOWS = (2, 4, 8, 16)
POOL_GW = D_MODEL // len(POOL_WINDOWS)

LANES = 128
CHUNK = LANES
MOD_ROWS = 8
MOD_TILE = 1536
VMEM_LIMIT = 62 * 1024 * 1024
ROW_TILE = 1024
FF_TILE = 1024
NEG_BIG = -1e30
LOG2E = 1.4426950408889634


def _params(n_axes):
    return pltpu.CompilerParams(
        dimension_semantics=("arbitrary",) * n_axes, vmem_limit_bytes=VMEM_LIMIT)


def _resident(shape):
    return pl.BlockSpec(shape, lambda *_: (0,) * len(shape), pipeline_mode=pl.Buffered(1))


def _rider_specs(riders, steps):
    in_specs, out_specs, out_shapes = [], [], []
    for w, layer, depth in riders:
        rows = w.shape[0] // depth
        assert rows % steps == 0
        block = (rows // steps, w.shape[1])
        in_specs.append(pl.BlockSpec(block, lambda i, layer=layer: (layer * steps + i, 0)))
        out_specs.append(pl.BlockSpec(block, lambda i: (i, 0)))
        out_shapes.append(jax.ShapeDtypeStruct((rows, w.shape[1]), BF16))
    return in_specs, out_specs, out_shapes


def _rmsnorm(x, g):
    return x * lax.rsqrt(jnp.mean(x * x, axis=-1, keepdims=True) + NORM_EPS) * g


def _dot(a, b):
    return jnp.dot(a, b, preferred_element_type=F32)


def _split3(x):
    hi = x.astype(BF16)
    rest = x - hi.astype(F32)
    mid = rest.astype(BF16)
    return hi, mid, (rest - mid.astype(F32)).astype(BF16)


def _mod_kernel(c_ref, w_ref, b_ref, o_ref):
    c = c_ref[...]
    s = c * jax.nn.sigmoid(c)
    lhs = jnp.concatenate([p.astype(F32) for p in _split3(s)], axis=0).astype(BF16)
    acc = jnp.zeros((3 * MOD_ROWS, w_ref.shape[2]), F32)
    for piece in _split3(w_ref[0]):
        acc = acc + _dot(lhs, piece)
    o_ref[0] = acc[:MOD_ROWS] + acc[MOD_ROWS:2 * MOD_ROWS] + acc[2 * MOD_ROWS:] + b_ref[0]


def _modulation(cmat, ada_w, ada_b):
    depth, _, n = ada_w.shape
    tn = MOD_TILE
    return pl.pallas_call(
        _mod_kernel,
        grid=(depth, n // tn),
        in_specs=[
            pl.BlockSpec((MOD_ROWS, D_MODEL), lambda i, j: (0, 0)),
            pl.BlockSpec((1, D_MODEL, tn), lambda i, j: (i, 0, j)),
            pl.BlockSpec((1, 1, tn), lambda i, j: (i, 0, j)),
        ],
        out_specs=pl.BlockSpec((1, MOD_ROWS, tn), lambda i, j: (i, 0, j)),
        out_shape=jax.ShapeDtypeStruct((depth, MOD_ROWS, n), F32),
        compiler_params=_params(2),
        name="adaln_modulation",
    )(cmat, ada_w, ada_b.reshape(depth, 1, n))


V_START = 2 * QK_W
O_START = V_START + V_W
G_START = O_START + V_W


KG_ROW = QK_W + 2 * V_W


def _front_kernel(with_qo, n_riders, x_ref, g_ref, sh_ref, sc_ref, wt32_ref, *rest):
    riders_in, rest = rest[:n_riders], rest[n_riders:]
    *outs, w_ref = rest
    outs, riders_out = outs[:len(outs) - n_riders], outs[len(outs) - n_riders:]
    for src, dst in zip(riders_in, riders_out):
        dst[...] = src[...].astype(BF16)
    nt = (((1,), (1,)), ((), ()))

    @pl.when(pl.program_id(0) == 0)
    def _():
        w_ref[:QK_W, :] = wt32_ref[:QK_W, :].astype(BF16)
        w_ref[QK_W:KG_ROW, :] = wt32_ref[V_START:G_START, :].astype(BF16)
        w_ref[KG_ROW:KG_ROW + QK_W, :] = wt32_ref[QK_W:V_START, :].astype(BF16)
        new = lax.broadcasted_iota(jnp.int32, (GATE_W, GATE_W), 0)
        old = lax.broadcasted_iota(jnp.int32, (GATE_W, GATE_W), 1)
        kind = new % 8 // 2
        gate_type = jnp.where(kind < 2, 2 * kind + 1, 2 * (kind - 2))
        perm = (old == gate_type * HEADS + 2 * (new // 8) + new % 2).astype(BF16)
        w_ref[KG_ROW + QK_W:, :] = _dot(perm, wt32_ref[G_START:, :].astype(BF16)).astype(BF16)

    def normed(s):
        x = x_ref[s * SUB_ROWS:(s + 1) * SUB_ROWS, :]
        return (_rmsnorm(x, g_ref[...]) * (1.0 + sc_ref[0]) + sh_ref[0]).astype(BF16)

    if with_qo:
        q_ref, kt_ref, v_ref, o_ref, gr_ref = outs
    else:
        kt_ref, v_ref, gr_ref = outs

    n_sub = x_ref.shape[0] // SUB_ROWS
    xb = normed(0)
    for s in range(n_sub):
        rows = slice(s * SUB_ROWS, (s + 1) * SUB_ROWS)

        def token_major(w_rows, xb=xb):
            return lax.dot_general(xb, w_ref[w_rows, :], nt, preferred_element_type=F32)

        if with_qo:
            q_ref[rows, :] = (token_major(slice(0, QK_W)) * (DQK ** -0.5)).astype(BF16)
        xb_next = normed(s + 1) if s + 1 < n_sub else None
        if with_qo:
            o_ref[rows, :] = token_major(slice(QK_W + V_W, KG_ROW))
        kg = lax.dot_general(w_ref[KG_ROW:, :], xb, nt, preferred_element_type=F32)
        kt_ref[:, rows] = kg[:QK_W].astype(BF16)
        gr_ref[:, rows] = kg[QK_W:]
        v_ref[rows, :] = token_major(slice(QK_W, QK_W + V_W)).astype(BF16)
        xb = xb_next


def _front(x2d, norm_g, sh, sc, w_in_t, rows_per_mod, with_qo, riders=()):
    n = x2d.shape[0]
    tm = min(ROW_TILE, n)
    steps = n // tm
    tiles_per_mod = rows_per_mod // tm
    row = lambda i: (i, 0)
    mod = lambda i: (i // tiles_per_mod, 0, 0)
    rider_in, rider_out, rider_shapes = _rider_specs(riders, steps)
    out_shape, out_specs = [], []
    if with_qo:
        out_shape.append(jax.ShapeDtypeStruct((n, QK_W), BF16))
        out_specs.append(pl.BlockSpec((tm, QK_W), row))
    out_shape.append(jax.ShapeDtypeStruct((QK_W, n), BF16))
    out_specs.append(pl.BlockSpec((QK_W, tm), lambda i: (0, i)))
    out_shape.append(jax.ShapeDtypeStruct((n, V_W), BF16))
    out_specs.append(pl.BlockSpec((tm, V_W), row))
    if with_qo:
        out_shape.append(jax.ShapeDtypeStruct((n, V_W), F32))
        out_specs.append(pl.BlockSpec((tm, V_W), row))
    out_shape.append(jax.ShapeDtypeStruct((GATE_W, n), F32))
    out_specs.append(pl.BlockSpec((GATE_W, tm), lambda i: (0, i)))
    out_shape += rider_shapes
    out_specs += rider_out
    return pl.pallas_call(
        functools.partial(_front_kernel, with_qo, len(riders)),
        grid=(steps,),
        in_specs=[
            pl.BlockSpec((tm, D_MODEL), row),
            _resident((1, D_MODEL)),
            pl.BlockSpec((1, 1, D_MODEL), mod),
            pl.BlockSpec((1, 1, D_MODEL), mod),
            _resident(w_in_t.shape),
        ] + rider_in,
        out_specs=out_specs,
        out_shape=out_shape,
        scratch_shapes=[pltpu.VMEM(w_in_t.shape, BF16)],
        compiler_params=_params(1),
        name="mlstm_front_qo" if with_qo else "mlstm_front_ctx",
    )(x2d, norm_g.reshape(1, D_MODEL), sh, sc, w_in_t, *[r[0] for r in riders])


def _softcap(g):
    return GATE_SOFTCAP * jnp.tanh(g * (1.0 / GATE_SOFTCAP))


def _log_sigmoid(g):
    return jnp.minimum(g, 0.0) - jnp.log1p(jnp.exp(-jnp.abs(g)))


PAIRS = HEADS // 2
BCOL_W = 16
GCOL_W = PAIRS * BCOL_W


def _gate_kernel(gr_ref, br_ref, col_ref, row_ref):
    t = gr_ref.shape[1]
    r = lax.broadcasted_iota(jnp.int32, (CHUNK, CHUNK), 0)
    c = lax.broadcasted_iota(jnp.int32, (CHUNK, CHUNK), 1)
    lower = (c <= r).astype(BF16)
    upper = (c >= r).astype(BF16)
    ones = jnp.ones((CHUNK, CHUNK), BF16)
    row_rhs = jnp.concatenate([upper, lower, ones], axis=1)
    col_lhs = jnp.concatenate([lower, upper], axis=1)
    src = lax.broadcasted_iota(jnp.int32, (3 * GATE_W, GCOL_W), 0)
    dst = lax.broadcasted_iota(jnp.int32, (3 * GATE_W, GCOL_W), 1)
    gate = src % GATE_W
    place = ((gate % 8 < 4)
             & (dst == (gate // 8) * BCOL_W + 4 * (src // GATE_W) + gate % 8)).astype(BF16)
    lane_piece = lax.broadcasted_iota(jnp.int32, (CHUNK, 3 * GATE_W), 1) // GATE_W
    m = lax.broadcasted_iota(jnp.int32, (GATE_W, CHUNK), 0) % 8
    fwd_rows = m < 2
    bwd_rows = (m >= 2) & (m < 4)
    nt = (((1,), (1,)), ((), ()))

    gr = _softcap(gr_ref[...] + br_ref[...])
    li = pltpu.roll(gr, GATE_W - 4, axis=0)
    lf3 = _split3(_log_sigmoid(gr))

    chunks = [slice(k * CHUNK, (k + 1) * CHUNK) for k in range(t // CHUNK)]
    sums = []
    for sl in chunks:
        cum = jnp.zeros((CHUNK, 3 * GATE_W), F32)
        y = jnp.zeros((GATE_W, 3 * CHUNK), F32)
        for piece in lf3:
            x = piece[:, sl]
            zero = jnp.zeros_like(x)
            xd = jnp.concatenate([jnp.where(fwd_rows, x, zero), jnp.where(bwd_rows, x, zero)],
                                 axis=1)
            cum = cum + lax.dot_general(col_lhs, jnp.concatenate([xd, xd, xd], axis=0), nt,
                                        preferred_element_type=F32)
            y = y + _dot(x, row_rhs)
        sums.append((cum, y))
    for sl, (cum, y) in zip(chunks, sums):
        hi, mid, lo = [p.astype(F32) for p in _split3(cum * LOG2E)]
        pieces = jnp.where(lane_piece == 0, hi, jnp.where(lane_piece == 1, mid, lo))
        col_ref[sl, :] = _dot(pieces.astype(BF16), place).astype(BF16)
        total = y[:, 2 * CHUNK:]
        e = li[:, sl] - jnp.where(fwd_rows, y[:, :CHUNK], y[:, CHUNK:2 * CHUNK])
        used = fwd_rows | bwd_rows
        row_ref[0, :, sl] = jnp.where(used, e * LOG2E, 0.0)
        row_ref[1, :, sl] = jnp.where(used, jnp.exp(total + e), 0.0)
        row_ref[2, :, sl] = jnp.where(used, jnp.exp(total), 0.0)


def _gate_prep(gates_raw, bias):
    n = gates_raw.shape[1]
    t = min(1024, n)
    return pl.pallas_call(
        _gate_kernel,
        grid=(n // t,),
        in_specs=[
            pl.BlockSpec((GATE_W, t), lambda i: (0, i)),
            _resident((GATE_W, 1)),
        ],
        out_specs=[
            pl.BlockSpec((t, GCOL_W), lambda i: (i, 0)),
            pl.BlockSpec((3, GATE_W, t), lambda i: (0, 0, i)),
        ],
        out_shape=[
            jax.ShapeDtypeStruct((n, GCOL_W), BF16),
            jax.ShapeDtypeStruct((3, GATE_W, n), F32),
        ],
        compiler_params=_params(1),
        name="mlstm_gate_prep",
    )(gates_raw, bias)


SCAN_SPLITS = 2
STATE_UNROLL = 8
OUTPUT_UNROLL = 8


def _scan_kernel(q_ref, kt_ref, v_ref, gcol_ref, grow_ref, ktc_ref, vc_ref, growc_ref,
                 h_ref, cst_ref, cs_ref):
    seq = q_ref.shape[0]
    nc = seq // CHUNK
    ncc = ktc_ref.shape[1] // CHUNK
    nc_part = nc // SCAN_SPLITS
    part = pl.program_id(2)
    r = lax.broadcasted_iota(jnp.int32, (CHUNK, CHUNK), 0)
    c = lax.broadcasted_iota(jnp.int32, (CHUNK, CHUNK), 1)
    masks = (r >= c, r <= c)
    left = c < DQK
    ones_blk = jnp.ones((CHUNK, DV), BF16)
    zeros_k = jnp.zeros((DQK, CHUNK), BF16)
    zeros_q = jnp.zeros((CHUNK, DQK), BF16)
    pair = pl.program_id(1)
    src = lax.broadcasted_iota(jnp.int32, (GCOL_W, 4 * CHUNK), 0)
    block = lax.broadcasted_iota(jnp.int32, (GCOL_W, 4 * CHUNK), 1) // CHUNK
    piece = src % BCOL_W
    sel = ((src // BCOL_W == pair) & (piece % 4 == block) & (piece < 12)).astype(BF16)

    def gate_row(ref, quantity, d, j, lanes):
        return ref[quantity, 2 * d + j:2 * d + j + 1, lanes]

    def state_increment(kt_j, v_j, w_row):
        ktw = (kt_j.astype(F32) * w_row).astype(BF16)
        return _dot(ktw, jnp.concatenate([v_j, ones_blk], axis=1))

    def decay2(a_row):
        return jnp.concatenate([a_row, a_row], axis=1)

    @pl.when(part == 0)
    def _():
        cst_ref[...] = jnp.zeros(cst_ref.shape, F32)
        for d in range(2):
            for step in range(ncc):
                k = step if d == 0 else ncc - 1 - step
                sl = slice(k * CHUNK, (k + 1) * CHUNK)
                ktp = ktc_ref[:, sl]
                vp = vc_ref[sl, :]
                for j in range(2):
                    inc = state_increment(ktp[j * DQK:(j + 1) * DQK], vp[:, j * DV:(j + 1) * DV],
                                          gate_row(growc_ref, 1, d, j, sl))
                    cst_ref[d, j] = cst_ref[d, j] * decay2(gate_row(growc_ref, 2, d, j, sl)) + inc

        def state_step(i, carry):
            for d in range(2):
                k = i if d == 0 else nc - 1 - i
                rows = pl.ds(pl.multiple_of(k * CHUNK, CHUNK), CHUNK)
                ktp = kt_ref[:, rows]
                vp = v_ref[rows, :]
                for j in range(2):
                    cst = cst_ref[d, j]
                    cs_ref[d, k, j * DQK:(j + 1) * DQK, :] = cst.astype(BF16)
                    inc = state_increment(ktp[j * DQK:(j + 1) * DQK], vp[:, j * DV:(j + 1) * DV],
                                          gate_row(grow_ref, 1, d, j, rows))
                    cst_ref[d, j] = cst * decay2(gate_row(grow_ref, 2, d, j, rows)) + inc
            return carry

        lax.fori_loop(0, nc, state_step, 0, unroll=STATE_UNROLL)

    def decay_stage(k):
        rows = pl.ds(pl.multiple_of(k * CHUNK, CHUNK), CHUNK)
        qp = q_ref[rows, :]
        ktp = kt_ref[:, rows]
        bb_all = _dot(gcol_ref[rows, :], sel)
        kbd = jnp.concatenate([jnp.concatenate([ktp[:DQK], zeros_k], axis=1),
                               jnp.concatenate([zeros_k, ktp[DQK:]], axis=1)], axis=0)
        s2 = _dot(qp, kbd)
        qf = qp.astype(F32)
        lhs = []
        for d in range(2):
            bb = [bb_all[:, (2 * d + j) * CHUNK:(2 * d + j + 1) * CHUNK] for j in range(2)]
            qs = (qf * jnp.exp2(jnp.where(left, bb[0], bb[1]))).astype(BF16)
            for j in range(2):
                e_row = gate_row(grow_ref, 0, d, j, rows)
                dm = jnp.exp2(jnp.where(masks[d], bb[j] + e_row, NEG_BIG))
                p = (s2[:, j * CHUNK:(j + 1) * CHUNK] * dm).astype(BF16)
                qs_j = [qs[:, :DQK], zeros_q] if j == 0 else [zeros_q, qs[:, DQK:]]
                lhs.append(jnp.concatenate([p] + qs_j, axis=1))
        return lhs

    def value_stage(k, i, lhs):
        rows = pl.ds(pl.multiple_of(k * CHUNK, CHUNK), CHUNK)
        out_rows = pl.ds(pl.multiple_of(i * CHUNK, CHUNK), CHUNK)
        vp = v_ref[rows, :]
        for j in range(2):
            v1 = jnp.concatenate([vp[:, j * DV:(j + 1) * DV], ones_blk], axis=1)
            h = None
            for d in range(2):
                out = _dot(lhs[2 * d + j], jnp.concatenate([v1, cs_ref[d, k]], axis=0))
                hd = out[:, :DV] / jnp.maximum(jnp.abs(out[:, DV:]), 1.0)
                h = hd if d == 0 else h + hd
            h_ref[out_rows, j * DV:(j + 1) * DV] = h

    def output_trip(t, carry):
        i0 = t * OUTPUT_UNROLL
        staged = [decay_stage(part * nc_part + i0 + g) for g in range(OUTPUT_UNROLL)]
        for g in range(OUTPUT_UNROLL):
            value_stage(part * nc_part + i0 + g, i0 + g, staged[g])
        return carry

    lax.fori_loop(0, nc_part // OUTPUT_UNROLL, output_trip, 0)


def _scan(q, kt, v, gcol, grow, ktc, vc, growc, bsz, seq, ctx_len):
    nc = seq // CHUNK
    return pl.pallas_call(
        _scan_kernel,
        grid=(bsz, PAIRS, SCAN_SPLITS),
        in_specs=[
            pl.BlockSpec((seq, 2 * DQK), lambda b, p, s: (b, p)),
            pl.BlockSpec((2 * DQK, seq), lambda b, p, s: (p, b)),
            pl.BlockSpec((seq, 2 * DV), lambda b, p, s: (b, p)),
            pl.BlockSpec((seq, GCOL_W), lambda b, p, s: (b, 0)),
            pl.BlockSpec((3, 8, seq), lambda b, p, s: (0, p, b)),
            pl.BlockSpec((2 * DQK, ctx_len), lambda b, p, s: (p, b)),
            pl.BlockSpec((ctx_len, 2 * DV), lambda b, p, s: (b, p)),
            pl.BlockSpec((3, 8, ctx_len), lambda b, p, s: (0, p, b)),
        ],
        out_specs=pl.BlockSpec((seq // SCAN_SPLITS, 2 * DV),
                               lambda b, p, s: (b * SCAN_SPLITS + s, p)),
        out_shape=jax.ShapeDtypeStruct((bsz * seq, V_W), F32),
        scratch_shapes=[pltpu.VMEM((2, 2, DQK, 2 * DV), F32),
                        pltpu.VMEM((2, nc, 2 * DQK, 2 * DV), BF16)],
        compiler_params=_params(3),
        name="mlstm_scan",
    )(q, kt, v, gcol, grow, ktc, vc, growc)


MLP_TILE = 1024
POST_TILE = 1024
SUB_ROWS = 256


def _mlp_pipeline(n_sub, mixer, finish, n2_ref, sh2_ref, sc2_ref, g2_ref, w1_ref, w2_ref):
    def normed(x1):
        return (_rmsnorm(x1, n2_ref[...]) * (1.0 + sc2_ref[0]) + sh2_ref[0]).astype(BF16)

    def slab(u, k):
        cols = slice(k * FF_TILE, (k + 1) * FF_TILE)
        hid = jnp.maximum(_dot(u, w1_ref[:, cols]), 0.0)
        return _dot((hid * hid).astype(BF16), w2_ref[cols, :])

    def advance(gen):
        try:
            next(gen)
            return gen, None
        except StopIteration as done:
            return None, done.value

    gen, x1 = mixer(0), None
    while gen is not None:
        gen, x1 = advance(gen)
    u = normed(x1)
    for s in range(n_sub):
        gen = mixer(s + 1) if s + 1 < n_sub else None
        x1_next = u_next = None
        acc = slab(u, 0)
        for k in range(1, D_FF // FF_TILE):
            if gen is not None:
                gen, x1_next = advance(gen)
                if gen is None:
                    u_next = normed(x1_next)
            acc = acc + slab(u, k)
        assert gen is None, "mixer has more phases than hidden slabs to hide them under"
        finish(s, x1 + g2_ref[0] * acc)
        x1, u = x1_next, u_next


def _mlstm_post_kernel(h_ref, o_ref, x_ref, ng_ref, wout_ref, g1_ref, n2_ref, sh2_ref, sc2_ref,
                       g2_ref, w1_ref, w2_ref, out_ref):
    def mixer(s):
        rows = slice(s * SUB_ROWS, (s + 1) * SUB_ROWS)
        gate = jax.nn.sigmoid(o_ref[rows, :])
        parts = []
        for k in range(HEADS):
            hk = h_ref[rows, k * DV:(k + 1) * DV]
            parts.append(hk * lax.rsqrt(jnp.mean(hk * hk, axis=-1, keepdims=True) + NORM_EPS))
        hn = jnp.concatenate(parts, axis=1) * ng_ref[...]
        y = (gate * hn).astype(BF16)
        return x_ref[rows, :] + g1_ref[0] * _dot(y, wout_ref[...])
        yield

    def finish(s, x2):
        out_ref[s * SUB_ROWS:(s + 1) * SUB_ROWS, :] = x2

    _mlp_pipeline(x_ref.shape[0] // SUB_ROWS, mixer, finish, n2_ref, sh2_ref, sc2_ref, g2_ref,
                  w1_ref, w2_ref)


def _mlstm_post(h, o, x2d, norm_g, w_out, g1, n2, sh2, sc2, g2, w1, w2, rows_per_mod):
    n = x2d.shape[0]
    tm = POST_TILE
    tiles_per_mod = rows_per_mod // tm
    row = pl.BlockSpec((tm, D_MODEL), lambda i: (i, 0))
    mod = pl.BlockSpec((1, 1, D_MODEL), lambda i: (i // tiles_per_mod, 0, 0))
    return pl.pallas_call(
        _mlstm_post_kernel,
        grid=(n // tm,),
        in_specs=[row, row, row, _resident((1, V_W)), _resident(w_out.shape), mod,
                  _resident((1, D_MODEL)), mod, mod, mod, _resident(w1.shape), _resident(w2.shape)],
        out_specs=row,
        out_shape=jax.ShapeDtypeStruct((n, D_MODEL), F32),
        compiler_params=_params(1),
        name="mlstm_post_mlp",
    )(h, o, x2d, norm_g.reshape(1, V_W), w_out, g1, n2.reshape(1, D_MODEL), sh2, sc2, g2, w1, w2)


def _pool_constants():
    t = np.arange(SUB_ROWS)
    pos = t % GRID_W
    bands, inv = [], []
    for window in POOL_WINDOWS:
        lo = np.clip(pos - window // 2, 0, GRID_W)
        hi = np.clip(pos - window // 2 + window, 0, GRID_W)
        same_group = t[:, None] // GRID_W == t[None, :] // GRID_W
        bands.append(same_group & (pos[None, :] >= lo[:, None]) & (pos[None, :] < hi[:, None]))
        inv.append(np.broadcast_to((1.0 / (hi - lo))[:, None], (SUB_ROWS, POOL_GW)))
    return (jnp.asarray(np.stack(bands), dtype=BF16), jnp.asarray(np.stack(inv), dtype=F32))


def _pool_layer_kernel(x_ref, n1_ref, sh1_ref, sc1_ref, g1_ref, band_ref, inv_ref, pw_ref, ps_ref,
                       n2_ref, sh2_ref, sc2_ref, g2_ref, w1_ref, w2_ref, fg_ref, out_ref):
    def mixer(s):
        x = x_ref[s * SUB_ROWS:(s + 1) * SUB_ROWS, :]
        xn = _rmsnorm(x, n1_ref[...]) * (1.0 + sc1_ref[0]) + sh1_ref[0]
        hi = xn.astype(BF16)
        lo = (xn - hi.astype(F32)).astype(BF16)
        ps = []
        for gi in range(len(POOL_WINDOWS)):
            cols = slice(gi * POOL_GW, (gi + 1) * POOL_GW)
            band = band_ref[gi]
            total = _dot(band, hi[:, cols]) + _dot(band, lo[:, cols])
            ps.append((total * inv_ref[gi] - xn[:, cols]).astype(BF16))
        yield
        ys = [_dot(p, pw_ref[gi]) for gi, p in enumerate(ps)]
        return x + g1_ref[0] * (jnp.concatenate(ys, axis=1) * ps_ref[...])

    def finish(s, x2):
        out_ref[s * SUB_ROWS:(s + 1) * SUB_ROWS, :] = _rmsnorm(x2, fg_ref[...])

    _mlp_pipeline(x_ref.shape[0] // SUB_ROWS, mixer, finish, n2_ref, sh2_ref, sc2_ref, g2_ref,
                  w1_ref, w2_ref)


def _pool_layer(x2d, n1, sh1, sc1, g1, pool_w, pool_scale, n2, sh2, sc2, g2, w1, w2,
                final_g, rows_per_mod):
    n = x2d.shape[0]
    tm = MLP_TILE
    tiles_per_mod = rows_per_mod // tm
    row = pl.BlockSpec((tm, D_MODEL), lambda i: (i, 0))
    mod = pl.BlockSpec((1, 1, D_MODEL), lambda i: (i // tiles_per_mod, 0, 0))
    vec = _resident((1, D_MODEL))
    bands, inv = _pool_constants()
    return pl.pallas_call(
        _pool_layer_kernel,
        grid=(n // tm,),
        in_specs=[row, vec, mod, mod, mod, _resident(bands.shape), _resident(inv.shape),
                  _resident(pool_w.shape), vec, vec, mod, mod, mod,
                  _resident(w1.shape), _resident(w2.shape), vec],
        out_specs=row,
        out_shape=jax.ShapeDtypeStruct((n, D_MODEL), F32),
        compiler_params=_params(1),
        name="pool_mlp_final",
    )(x2d, n1.reshape(1, D_MODEL), sh1, sc1, g1, bands, inv, pool_w,
      pool_scale.reshape(1, D_MODEL),
      n2.reshape(1, D_MODEL), sh2, sc2, g2, w1, w2, final_g.reshape(1, D_MODEL))


def kernel(x, c, ctx, c_ctx, ada_w, ada_b, norm1_g, norm2_g, mlstm_w_in, mlstm_gate_b,
           mlstm_norm_g, mlstm_w_out, pool_w, pool_scale, mlp_w1, mlp_w2, final_g):
    bsz, seq, dm = x.shape
    ctx_len = ctx.shape[1]
    assert dm == D_MODEL and seq % (2 * CHUNK) == 0 and ctx_len % CHUNK == 0
    assert seq % ROW_TILE == 0 and seq % MLP_TILE == 0 and seq % POST_TILE == 0
    assert (bsz * ctx_len) % SUB_ROWS == 0 and ROW_TILE % SUB_ROWS == 0 and bsz + 1 <= MOD_ROWS
    assert MLP_TILE % SUB_ROWS == 0 and SUB_ROWS % GRID_W == 0
    assert ada_w.shape[0] == 2 and mlstm_w_in.shape[0] == 1 and pool_w.shape[0] == 1

    cmat = jnp.concatenate(
        [c, c_ctx[None, :], jnp.zeros((MOD_ROWS - bsz - 1, dm), F32)], axis=0)
    mod = _modulation(cmat, ada_w, ada_b)

    def mod_rows(layer, which, rows):
        return mod[layer, rows, which * dm:(which + 1) * dm][:, None, :]

    lat = slice(0, bsz)
    cx = slice(bsz, bsz + 1)

    def pair_major(g):
        g4 = g.reshape(g.shape[:-1] + (4, PAIRS, 2))
        g4 = jnp.stack([g4[..., t, :, :] for t in (1, 3, 0, 2)], axis=-2)
        return g4.reshape(g.shape)

    w_in = mlstm_w_in[0].T
    gate_b = pair_major(mlstm_gate_b[0].reshape(GATE_W)).reshape(GATE_W, 1)

    x2d = x.reshape(bsz * seq, dm)
    depth = mlp_w1.shape[0]
    w1_all = mlp_w1.reshape(depth * dm, D_FF)
    w2_all = mlp_w2.reshape(depth * D_FF, dm)
    q, kt, v, o, gates, w1_a, w2_a, w1_b, w2_b, w_out = _front(
        x2d, norm1_g[0], mod_rows(0, 0, lat), mod_rows(0, 1, lat), w_in, seq, True,
        ((w1_all, 0, depth), (w2_all, 0, depth), (w1_all, 1, depth), (w2_all, 1, depth),
         (mlstm_w_out.reshape(dm, dm), 0, 1)))
    ktc, vc, gates_c = _front(
        ctx.reshape(bsz * ctx_len, dm), norm1_g[0], mod_rows(0, 0, cx), mod_rows(0, 1, cx),
        w_in, bsz * ctx_len, False)
    gcol, grow = _gate_prep(gates, gate_b)
    _, growc = _gate_prep(gates_c, gate_b)

    h = _scan(q, kt, v, gcol, grow, ktc, vc, growc, bsz, seq, ctx_len)
    x2d = _mlstm_post(
        h, o, x2d, mlstm_norm_g[0], w_out, mod_rows(0, 2, lat),
        norm2_g[0], mod_rows(0, 3, lat), mod_rows(0, 4, lat), mod_rows(0, 5, lat),
        w1_a, w2_a, seq)

    out = _pool_layer(
        x2d, norm1_g[1], mod_rows(1, 0, lat), mod_rows(1, 1, lat), mod_rows(1, 2, lat),
        pool_w[0].astype(BF16), pool_scale[0], norm2_g[1], mod_rows(1, 3, lat),
        mod_rows(1, 4, lat), mod_rows(1, 5, lat), w1_b, w2_b, final_g, seq)
    return out.reshape(bsz, seq, dm)
```

```python
import functools

import jax
import jax.numpy as jnp
import numpy as np
from jax import lax
from jax.experimental import pallas as pl
from jax.experimental.pallas import tpu as pltpu

F32 = jnp.float32
BF16 = jnp.bfloat16

D_MODEL = 1024
HEADS = 8
DV = 128
DQK = 64
QK_W = HEADS * DQK
V_W = HEADS * DV
GATE_W = 4 * HEADS
D_FF = 4 * D_MODEL
NORM_EPS = 1e-6
GATE_SOFTCAP = 15.0
GRID_W = 64
POOL_WINDOWS = (2, 4, 8, 16)
POOL_GW = D_MODEL // len(POOL_WINDOWS)

LANES = 128
CHUNK = LANES
MOD_ROWS = 8
MOD_TILE = 1536
VMEM_LIMIT = 62 * 1024 * 1024
ROW_TILE = 1024
FF_TILE = 1024
NEG_BIG = -1e30
LOG2E = 1.4426950408889634


def _params(n_axes):
    return pltpu.CompilerParams(
        dimension_semantics=("arbitrary",) * n_axes, vmem_limit_bytes=VMEM_LIMIT)


def _resident(shape):
    return pl.BlockSpec(shape, lambda *_: (0,) * len(shape), pipeline_mode=pl.Buffered(1))


def _rider_specs(riders, steps):
    in_specs, out_specs, out_shapes = [], [], []
    for w, layer, depth in riders:
        rows = w.shape[0] // depth
        assert rows % steps == 0
        block = (rows // steps, w.shape[1])
        in_specs.append(pl.BlockSpec(block, lambda i, layer=layer: (layer * steps + i, 0)))
        out_specs.append(pl.BlockSpec(block, lambda i: (i, 0)))
        out_shapes.append(jax.ShapeDtypeStruct((rows, w.shape[1]), BF16))
    return in_specs, out_specs, out_shapes


def _rmsnorm(x, g):
    return x * lax.rsqrt(jnp.mean(x * x, axis=-1, keepdims=True) + NORM_EPS) * g


def _dot(a, b):
    return jnp.dot(a, b, preferred_element_type=F32)


def _split3(x):
    hi = x.astype(BF16)
    rest = x - hi.astype(F32)
    mid = rest.astype(BF16)
    return hi, mid, (rest - mid.astype(F32)).astype(BF16)


def _mod_kernel(c_ref, w_ref, b_ref, o_ref):
    c = c_ref[...]
    s = c * jax.nn.sigmoid(c)
    lhs = jnp.concatenate([p.astype(F32) for p in _split3(s)], axis=0).astype(BF16)
    acc = jnp.zeros((3 * MOD_ROWS, w_ref.shape[2]), F32)
    for piece in _split3(w_ref[0]):
        acc = acc + _dot(lhs, piece)
    o_ref[0] = acc[:MOD_ROWS] + acc[MOD_ROWS:2 * MOD_ROWS] + acc[2 * MOD_ROWS:] + b_ref[0]


def _modulation(cmat, ada_w, ada_b):
    depth, _, n = ada_w.shape
    tn = MOD_TILE
    return pl.pallas_call(
        _mod_kernel,
        grid=(depth, n // tn),
        in_specs=[
            pl.BlockSpec((MOD_ROWS, D_MODEL), lambda i, j: (0, 0)),
            pl.BlockSpec((1, D_MODEL, tn), lambda i, j: (i, 0, j)),
            pl.BlockSpec((1, 1, tn), lambda i, j: (i, 0, j)),
        ],
        out_specs=pl.BlockSpec((1, MOD_ROWS, tn), lambda i, j: (i, 0, j)),
        out_shape=jax.ShapeDtypeStruct((depth, MOD_ROWS, n), F32),
        compiler_params=_params(2),
        name="adaln_modulation",
    )(cmat, ada_w, ada_b.reshape(depth, 1, n))


V_START = 2 * QK_W
O_START = V_START + V_W
G_START = O_START + V_W


KG_ROW = QK_W + 2 * V_W


def _front_kernel(with_qo, n_riders, x_ref, g_ref, sh_ref, sc_ref, wt32_ref, *rest):
    riders_in, rest = rest[:n_riders], rest[n_riders:]
    *outs, w_ref = rest
    outs, riders_out = outs[:len(outs) - n_riders], outs[len(outs) - n_riders:]
    for src, dst in zip(riders_in, riders_out):
        dst[...] = src[...].astype(BF16)
    nt = (((1,), (1,)), ((), ()))

    @pl.when(pl.program_id(0) == 0)
    def _():
        w_ref[:QK_W, :] = wt32_ref[:QK_W, :].astype(BF16)
        w_ref[QK_W:KG_ROW, :] = wt32_ref[V_START:G_START, :].astype(BF16)
        w_ref[KG_ROW:KG_ROW + QK_W, :] = wt32_ref[QK_W:V_START, :].astype(BF16)
        new = lax.broadcasted_iota(jnp.int32, (GATE_W, GATE_W), 0)
        old = lax.broadcasted_iota(jnp.int32, (GATE_W, GATE_W), 1)
        kind = new % 8 // 2
        gate_type = jnp.where(kind < 2, 2 * kind + 1, 2 * (kind - 2))
        perm = (old == gate_type * HEADS + 2 * (new // 8) + new % 2).astype(BF16)
        w_ref[KG_ROW + QK_W:, :] = _dot(perm, wt32_ref[G_START:, :].astype(BF16)).astype(BF16)

    def normed(s):
        x = x_ref[s * SUB_ROWS:(s + 1) * SUB_ROWS, :]
        return (_rmsnorm(x, g_ref[...]) * (1.0 + sc_ref[0]) + sh_ref[0]).astype(BF16)

    if with_qo:
        q_ref, kt_ref, v_ref, o_ref, gr_ref = outs
    else:
        kt_ref, v_ref, gr_ref = outs

    n_sub = x_ref.shape[0] // SUB_ROWS
    xb = normed(0)
    for s in range(n_sub):
        rows = slice(s * SUB_ROWS, (s + 1) * SUB_ROWS)

        def token_major(w_rows, xb=xb):
            return lax.dot_general(xb, w_ref[w_rows, :], nt, preferred_element_type=F32)

        if with_qo:
            q_ref[rows, :] = (token_major(slice(0, QK_W)) * (DQK ** -0.5)).astype(BF16)
        xb_next = normed(s + 1) if s + 1 < n_sub else None
        if with_qo:
            o_ref[rows, :] = token_major(slice(QK_W + V_W, KG_ROW))
        kg = lax.dot_general(w_ref[KG_ROW:, :], xb, nt, preferred_element_type=F32)
        kt_ref[:, rows] = kg[:QK_W].astype(BF16)
        gr_ref[:, rows] = kg[QK_W:]
        v_ref[rows, :] = token_major(slice(QK_W, QK_W + V_W)).astype(BF16)
        xb = xb_next


def _front(x2d, norm_g, sh, sc, w_in_t, rows_per_mod, with_qo, riders=()):
    n = x2d.shape[0]
    tm = min(ROW_TILE, n)
    steps = n // tm
    tiles_per_mod = rows_per_mod // tm
    row = lambda i: (i, 0)
    mod = lambda i: (i // tiles_per_mod, 0, 0)
    rider_in, rider_out, rider_shapes = _rider_specs(riders, steps)
    out_shape, out_specs = [], []
    if with_qo:
        out_shape.append(jax.ShapeDtypeStruct((n, QK_W), BF16))
        out_specs.append(pl.BlockSpec((tm, QK_W), row))
    out_shape.append(jax.ShapeDtypeStruct((QK_W, n), BF16))
    out_specs.append(pl.BlockSpec((QK_W, tm), lambda i: (0, i)))
    out_shape.append(jax.ShapeDtypeStruct((n, V_W), BF16))
    out_specs.append(pl.BlockSpec((tm, V_W), row))
    if with_qo:
        out_shape.append(jax.ShapeDtypeStruct((n, V_W), F32))
        out_specs.append(pl.BlockSpec((tm, V_W), row))
    out_shape.append(jax.ShapeDtypeStruct((GATE_W, n), F32))
    out_specs.append(pl.BlockSpec((GATE_W, tm), lambda i: (0, i)))
    out_shape += rider_shapes
    out_specs += rider_out
    return pl.pallas_call(
        functools.partial(_front_kernel, with_qo, len(riders)),
        grid=(steps,),
        in_specs=[
            pl.BlockSpec((tm, D_MODEL), row),
            _resident((1, D_MODEL)),
            pl.BlockSpec((1, 1, D_MODEL), mod),
            pl.BlockSpec((1, 1, D_MODEL), mod),
            _resident(w_in_t.shape),
        ] + rider_in,
        out_specs=out_specs,
        out_shape=out_shape,
        scratch_shapes=[pltpu.VMEM(w_in_t.shape, BF16)],
        compiler_params=_params(1),
        name="mlstm_front_qo" if with_qo else "mlstm_front_ctx",
    )(x2d, norm_g.reshape(1, D_MODEL), sh, sc, w_in_t, *[r[0] for r in riders])


def _softcap(g):
    return GATE_SOFTCAP * jnp.tanh(g * (1.0 / GATE_SOFTCAP))


def _log_sigmoid(g):
    return jnp.minimum(g, 0.0) - jnp.log1p(jnp.exp(-jnp.abs(g)))


PAIRS = HEADS // 2
BCOL_W = 16
GCOL_W = PAIRS * BCOL_W
GATE_TILE = 2048


def _gate_kernel(gr_ref, br_ref, col_ref, row_ref):
    t = gr_ref.shape[1]
    r = lax.broadcasted_iota(jnp.int32, (CHUNK, CHUNK), 0)
    c = lax.broadcasted_iota(jnp.int32, (CHUNK, CHUNK), 1)
    lower = (c <= r).astype(BF16)
    upper = (c >= r).astype(BF16)
    ones = jnp.ones((CHUNK, CHUNK), BF16)
    row_rhs = jnp.concatenate([upper, lower, ones], axis=1)
    col_lhs = jnp.concatenate([lower, upper], axis=1)
    src = lax.broadcasted_iota(jnp.int32, (3 * GATE_W, GCOL_W), 0)
    dst = lax.broadcasted_iota(jnp.int32, (3 * GATE_W, GCOL_W), 1)
    gate = src % GATE_W
    place = ((gate % 8 < 4)
             & (dst == (gate // 8) * BCOL_W + 4 * (src // GATE_W) + gate % 8)).astype(BF16)
    lane_piece = lax.broadcasted_iota(jnp.int32, (CHUNK, 3 * GATE_W), 1) // GATE_W
    m = lax.broadcasted_iota(jnp.int32, (GATE_W, CHUNK), 0) % 8
    fwd_rows = m < 2
    bwd_rows = (m >= 2) & (m < 4)
    nt = (((1,), (1,)), ((), ()))

    gr = _softcap(gr_ref[...] + br_ref[...])
    li = pltpu.roll(gr, GATE_W - 4, axis=0)
    lf3 = _split3(_log_sigmoid(gr))

    chunks = [slice(k * CHUNK, (k + 1) * CHUNK) for k in range(t // CHUNK)]
    sums = []
    for sl in chunks:
        cum = jnp.zeros((CHUNK, 3 * GATE_W), F32)
        y = jnp.zeros((GATE_W, 3 * CHUNK), F32)
        for piece in lf3:
            x = piece[:, sl]
            zero = jnp.zeros_like(x)
            xd = jnp.concatenate([jnp.where(fwd_rows, x, zero), jnp.where(bwd_rows, x, zero)],
                                 axis=1)
            cum = cum + lax.dot_general(col_lhs, jnp.concatenate([xd, xd, xd], axis=0), nt,
                                        preferred_element_type=F32)
            y = y + _dot(x, row_rhs)
        sums.append((cum, y))
    for sl, (cum, y) in zip(chunks, sums):
        hi, mid, lo = [p.astype(F32) for p in _split3(cum * LOG2E)]
        pieces = jnp.where(lane_piece == 0, hi, jnp.where(lane_piece == 1, mid, lo))
        col_ref[sl, :] = _dot(pieces.astype(BF16), place).astype(BF16)
        total = y[:, 2 * CHUNK:]
        e = li[:, sl] - jnp.where(fwd_rows, y[:, :CHUNK], y[:, CHUNK:2 * CHUNK])
        used = fwd_rows | bwd_rows
        row_ref[0, :, sl] = jnp.where(used, e * LOG2E, 0.0)
        row_ref[1, :, sl] = jnp.where(used, jnp.exp(total + e), 0.0)
        row_ref[2, :, sl] = jnp.where(used, jnp.exp(total), 0.0)


def _gate_prep(gates_raw, bias):
    n = gates_raw.shape[1]
    t = min(GATE_TILE, n)
    return pl.pallas_call(
        _gate_kernel,
        grid=(n // t,),
        in_specs=[
            pl.BlockSpec((GATE_W, t), lambda i: (0, i)),
            _resident((GATE_W, 1)),
        ],
        out_specs=[
            pl.BlockSpec((t, GCOL_W), lambda i: (i, 0)),
            pl.BlockSpec((3, GATE_W, t), lambda i: (0, 0, i)),
        ],
        out_shape=[
            jax.ShapeDtypeStruct((n, GCOL_W), BF16),
            jax.ShapeDtypeStruct((3, GATE_W, n), F32),
        ],
        compiler_params=_params(1),
        name="mlstm_gate_prep",
    )(gates_raw, bias)


SCAN_SPLITS = 2
STATE_UNROLL = 8
OUTPUT_UNROLL = 8


def _scan_kernel(q_ref, kt_ref, v_ref, gcol_ref, grow_ref, ktc_ref, vc_ref, growc_ref,
                 h_ref, cst_ref, cs_ref):
    seq = q_ref.shape[0]
    nc = seq // CHUNK
    ncc = ktc_ref.shape[1] // CHUNK
    nc_part = nc // SCAN_SPLITS
    part = pl.program_id(2)
    r = lax.broadcasted_iota(jnp.int32, (CHUNK, CHUNK), 0)
    c = lax.broadcasted_iota(jnp.int32, (CHUNK, CHUNK), 1)
    masks = (r >= c, r <= c)
    left = c < DQK
    ones_blk = jnp.ones((CHUNK, DV), BF16)
    zeros_k = jnp.zeros((DQK, CHUNK), BF16)
    zeros_q = jnp.zeros((CHUNK, DQK), BF16)
    pair = pl.program_id(1)
    src = lax.broadcasted_iota(jnp.int32, (GCOL_W, 4 * CHUNK), 0)
    block = lax.broadcasted_iota(jnp.int32, (GCOL_W, 4 * CHUNK), 1) // CHUNK
    piece = src % BCOL_W
    sel = ((src // BCOL_W == pair) & (piece % 4 == block) & (piece < 12)).astype(BF16)

    def gate_row(ref, quantity, d, j, lanes):
        return ref[quantity, 2 * d + j:2 * d + j + 1, lanes]

    def state_increment(kt_j, v_j, w_row):
        ktw = (kt_j.astype(F32) * w_row).astype(BF16)
        return _dot(ktw, jnp.concatenate([v_j, ones_blk], axis=1))

    def decay2(a_row):
        return jnp.concatenate([a_row, a_row], axis=1)

    @pl.when(part == 0)
    def _():
        cst_ref[...] = jnp.zeros(cst_ref.shape, F32)
        for d in range(2):
            for step in range(ncc):
                k = step if d == 0 else ncc - 1 - step
                sl = slice(k * CHUNK, (k + 1) * CHUNK)
                ktp = ktc_ref[:, sl]
                vp = vc_ref[sl, :]
                for j in range(2):
                    inc = state_increment(ktp[j * DQK:(j + 1) * DQK], vp[:, j * DV:(j + 1) * DV],
                                          gate_row(growc_ref, 1, d, j, sl))
                    cst_ref[d, j] = cst_ref[d, j] * decay2(gate_row(growc_ref, 2, d, j, sl)) + inc

        def state_step(i, carry):
            for d in range(2):
                k = i if d == 0 else nc - 1 - i
                rows = pl.ds(pl.multiple_of(k * CHUNK, CHUNK), CHUNK)
                ktp = kt_ref[:, rows]
                vp = v_ref[rows, :]
                for j in range(2):
                    cst = cst_ref[d, j]
                    cs_ref[d, k, j * DQK:(j + 1) * DQK, :] = cst.astype(BF16)
                    inc = state_increment(ktp[j * DQK:(j + 1) * DQK], vp[:, j * DV:(j + 1) * DV],
                                          gate_row(grow_ref, 1, d, j, rows))
                    cst_ref[d, j] = cst * decay2(gate_row(grow_ref, 2, d, j, rows)) + inc
            return carry

        lax.fori_loop(0, nc, state_step, 0, unroll=STATE_UNROLL)

    def decay_stage(k):
        rows = pl.ds(pl.multiple_of(k * CHUNK, CHUNK), CHUNK)
        qp = q_ref[rows, :]
        ktp = kt_ref[:, rows]
        bb_all = _dot(gcol_ref[rows, :], sel)
        kbd = jnp.concatenate([jnp.concatenate([ktp[:DQK], zeros_k], axis=1),
                               jnp.concatenate([zeros_k, ktp[DQK:]], axis=1)], axis=0)
        s2 = _dot(qp, kbd)
        qf = qp.astype(F32)
        lhs = []
        for d in range(2):
            bb = [bb_all[:, (2 * d + j) * CHUNK:(2 * d + j + 1) * CHUNK] for j in range(2)]
            qs = (qf * jnp.exp2(jnp.where(left, bb[0], bb[1]))).astype(BF16)
            for j in range(2):
                e_row = gate_row(grow_ref, 0, d, j, rows)
                dm = jnp.exp2(jnp.where(masks[d], bb[j] + e_row, NEG_BIG))
                p = (s2[:, j * CHUNK:(j + 1) * CHUNK] * dm).astype(BF16)
                qs_j = [qs[:, :DQK], zeros_q] if j == 0 else [zeros_q, qs[:, DQK:]]
                lhs.append(jnp.concatenate([p] + qs_j, axis=1))
        return lhs

    def value_stage(k, i, lhs):
        rows = pl.ds(pl.multiple_of(k * CHUNK, CHUNK), CHUNK)
        out_rows = pl.ds(pl.multiple_of(i * CHUNK, CHUNK), CHUNK)
        vp = v_ref[rows, :]
        for j in range(2):
            v1 = jnp.concatenate([vp[:, j * DV:(j + 1) * DV], ones_blk], axis=1)
            h = None
            for d in range(2):
                out = _dot(lhs[2 * d + j], jnp.concatenate([v1, cs_ref[d, k]], axis=0))
                hd = out[:, :DV] / jnp.maximum(jnp.abs(out[:, DV:]), 1.0)
                h = hd if d == 0 else h + hd
            h_ref[out_rows, j * DV:(j + 1) * DV] = h

    def output_trip(t, carry):
        i0 = t * OUTPUT_UNROLL
        staged = [decay_stage(part * nc_part + i0 + g) for g in range(OUTPUT_UNROLL)]
        for g in range(OUTPUT_UNROLL):
            value_stage(part * nc_part + i0 + g, i0 + g, staged[g])
        return carry

    lax.fori_loop(0, nc_part // OUTPUT_UNROLL, output_trip, 0)


def _scan(q, kt, v, gcol, grow, ktc, vc, growc, bsz, seq, ctx_len):
    nc = seq // CHUNK
    return pl.pallas_call(
        _scan_kernel,
        grid=(bsz, PAIRS, SCAN_SPLITS),
        in_specs=[
            pl.BlockSpec((seq, 2 * DQK), lambda b, p, s: (b, p)),
            pl.BlockSpec((2 * DQK, seq), lambda b, p, s: (p, b)),
            pl.BlockSpec((seq, 2 * DV), lambda b, p, s: (b, p)),
            pl.BlockSpec((seq, GCOL_W), lambda b, p, s: (b, 0)),
            pl.BlockSpec((3, 8, seq), lambda b, p, s: (0, p, b)),
            pl.BlockSpec((2 * DQK, ctx_len), lambda b, p, s: (p, b)),
            pl.BlockSpec((ctx_len, 2 * DV), lambda b, p, s: (b, p)),
            pl.BlockSpec((3, 8, ctx_len), lambda b, p, s: (0, p, b)),
        ],
        out_specs=pl.BlockSpec((seq // SCAN_SPLITS, 2 * DV),
                               lambda b, p, s: (b * SCAN_SPLITS + s, p)),
        out_shape=jax.ShapeDtypeStruct((bsz * seq, V_W), F32),
        scratch_shapes=[pltpu.VMEM((2, 2, DQK, 2 * DV), F32),
                        pltpu.VMEM((2, nc, 2 * DQK, 2 * DV), BF16)],
        compiler_params=_params(3),
        name="mlstm_scan",
    )(q, kt, v, gcol, grow, ktc, vc, growc)


MLP_TILE = 1024
POST_TILE = 1024
SUB_ROWS = 256


def _mlp_pipeline(n_sub, mixer, finish, n2_ref, sh2_ref, sc2_ref, g2_ref, w1_ref, w2_ref):
    def normed(x1):
        return (_rmsnorm(x1, n2_ref[...]) * (1.0 + sc2_ref[0]) + sh2_ref[0]).astype(BF16)

    def slab(u, k):
        cols = slice(k * FF_TILE, (k + 1) * FF_TILE)
        hid = jnp.maximum(_dot(u, w1_ref[:, cols]), 0.0)
        return _dot((hid * hid).astype(BF16), w2_ref[cols, :])

    def advance(gen):
        try:
            next(gen)
            return gen, None
        except StopIteration as done:
            return None, done.value

    gen, x1 = mixer(0), None
    while gen is not None:
        gen, x1 = advance(gen)
    u = normed(x1)
    for s in range(n_sub):
        gen = mixer(s + 1) if s + 1 < n_sub else None
        x1_next = u_next = None
        acc = slab(u, 0)
        for k in range(1, D_FF // FF_TILE):
            if gen is not None:
                gen, x1_next = advance(gen)
                if gen is None:
                    u_next = normed(x1_next)
            acc = acc + slab(u, k)
        assert gen is None, "mixer has more phases than hidden slabs to hide them under"
        finish(s, x1 + g2_ref[0] * acc)
        x1, u = x1_next, u_next


def _mlstm_post_kernel(h_ref, o_ref, x_ref, ng_ref, wout_ref, g1_ref, n2_ref, sh2_ref, sc2_ref,
                       g2_ref, w1_ref, w2_ref, out_ref):
    def mixer(s):
        rows = slice(s * SUB_ROWS, (s + 1) * SUB_ROWS)
        gate = jax.nn.sigmoid(o_ref[rows, :])
        parts = []
        for k in range(HEADS):
            hk = h_ref[rows, k * DV:(k + 1) * DV]
            parts.append(hk * lax.rsqrt(jnp.mean(hk * hk, axis=-1, keepdims=True) + NORM_EPS))
        hn = jnp.concatenate(parts, axis=1) * ng_ref[...]
        y = (gate * hn).astype(BF16)
        return x_ref[rows, :] + g1_ref[0] * _dot(y, wout_ref[...])
        yield

    def finish(s, x2):
        out_ref[s * SUB_ROWS:(s + 1) * SUB_ROWS, :] = x2

    _mlp_pipeline(x_ref.shape[0] // SUB_ROWS, mixer, finish, n2_ref, sh2_ref, sc2_ref, g2_ref,
                  w1_ref, w2_ref)


def _mlstm_post(h, o, x2d, norm_g, w_out, g1, n2, sh2, sc2, g2, w1, w2, rows_per_mod):
    n = x2d.shape[0]
    tm = POST_TILE
    tiles_per_mod = rows_per_mod // tm
    row = pl.BlockSpec((tm, D_MODEL), lambda i: (i, 0))
    mod = pl.BlockSpec((1, 1, D_MODEL), lambda i: (i // tiles_per_mod, 0, 0))
    return pl.pallas_call(
        _mlstm_post_kernel,
        grid=(n // tm,),
        in_specs=[row, row, row, _resident((1, V_W)), _resident(w_out.shape), mod,
                  _resident((1, D_MODEL)), mod, mod, mod, _resident(w1.shape), _resident(w2.shape)],
        out_specs=row,
        out_shape=jax.ShapeDtypeStruct((n, D_MODEL), F32),
        compiler_params=_params(1),
        name="mlstm_post_mlp",
    )(h, o, x2d, norm_g.reshape(1, V_W), w_out, g1, n2.reshape(1, D_MODEL), sh2, sc2, g2, w1, w2)


def _pool_constants():
    t = np.arange(SUB_ROWS)
    pos = t % GRID_W
    bands, inv = [], []
    for window in POOL_WINDOWS:
        lo = np.clip(pos - window // 2, 0, GRID_W)
        hi = np.clip(pos - window // 2 + window, 0, GRID_W)
        same_group = t[:, None] // GRID_W == t[None, :] // GRID_W
        bands.append(same_group & (pos[None, :] >= lo[:, None]) & (pos[None, :] < hi[:, None]))
        inv.append(np.broadcast_to((1.0 / (hi - lo))[:, None], (SUB_ROWS, POOL_GW)))
    return (jnp.asarray(np.stack(bands), dtype=BF16), jnp.asarray(np.stack(inv), dtype=F32))


def _pool_layer_kernel(x_ref, n1_ref, sh1_ref, sc1_ref, g1_ref, band_ref, inv_ref, pw_ref, ps_ref,
                       n2_ref, sh2_ref, sc2_ref, g2_ref, w1_ref, w2_ref, fg_ref, out_ref):
    def mixer(s):
        x = x_ref[s * SUB_ROWS:(s + 1) * SUB_ROWS, :]
        xn = _rmsnorm(x, n1_ref[...]) * (1.0 + sc1_ref[0]) + sh1_ref[0]
        hi = xn.astype(BF16)
        lo = (xn - hi.astype(F32)).astype(BF16)
        ps = []
        for gi in range(len(POOL_WINDOWS)):
            cols = slice(gi * POOL_GW, (gi + 1) * POOL_GW)
            band = band_ref[gi]
            total = _dot(band, hi[:, cols]) + _dot(band, lo[:, cols])
            ps.append((total * inv_ref[gi] - xn[:, cols]).astype(BF16))
        yield
        ys = [_dot(p, pw_ref[gi]) for gi, p in enumerate(ps)]
        return x + g1_ref[0] * (jnp.concatenate(ys, axis=1) * ps_ref[...])

    def finish(s, x2):
        out_ref[s * SUB_ROWS:(s + 1) * SUB_ROWS, :] = _rmsnorm(x2, fg_ref[...])

    _mlp_pipeline(x_ref.shape[0] // SUB_ROWS, mixer, finish, n2_ref, sh2_ref, sc2_ref, g2_ref,
                  w1_ref, w2_ref)


def _pool_layer(x2d, n1, sh1, sc1, g1, pool_w, pool_scale, n2, sh2, sc2, g2, w1, w2,
                final_g, rows_per_mod):
    n = x2d.shape[0]
    tm = MLP_TILE
    tiles_per_mod = rows_per_mod // tm
    row = pl.BlockSpec((tm, D_MODEL), lambda i: (i, 0))
    mod = pl.BlockSpec((1, 1, D_MODEL), lambda i: (i // tiles_per_mod, 0, 0))
    vec = _resident((1, D_MODEL))
    bands, inv = _pool_constants()
    return pl.pallas_call(
        _pool_layer_kernel,
        grid=(n // tm,),
        in_specs=[row, vec, mod, mod, mod, _resident(bands.shape), _resident(inv.shape),
                  _resident(pool_w.shape), vec, vec, mod, mod, mod,
                  _resident(w1.shape), _resident(w2.shape), vec],
        out_specs=row,
        out_shape=jax.ShapeDtypeStruct((n, D_MODEL), F32),
        compiler_params=_params(1),
        name="pool_mlp_final",
    )(x2d, n1.reshape(1, D_MODEL), sh1, sc1, g1, bands, inv, pool_w,
      pool_scale.reshape(1, D_MODEL),
      n2.reshape(1, D_MODEL), sh2, sc2, g2, w1, w2, final_g.reshape(1, D_MODEL))


def kernel(x, c, ctx, c_ctx, ada_w, ada_b, norm1_g, norm2_g, mlstm_w_in, mlstm_gate_b,
           mlstm_norm_g, mlstm_w_out, pool_w, pool_scale, mlp_w1, mlp_w2, final_g):
    bsz, seq, dm = x.shape
    ctx_len = ctx.shape[1]
    assert dm == D_MODEL and seq % (2 * CHUNK) == 0 and ctx_len % CHUNK == 0
    assert seq % ROW_TILE == 0 and seq % MLP_TILE == 0 and seq % POST_TILE == 0
    assert (bsz * ctx_len) % SUB_ROWS == 0 and ROW_TILE % SUB_ROWS == 0 and bsz + 1 <= MOD_ROWS
    assert MLP_TILE % SUB_ROWS == 0 and SUB_ROWS % GRID_W == 0
    assert ada_w.shape[0] == 2 and mlstm_w_in.shape[0] == 1 and pool_w.shape[0] == 1

    cmat = jnp.concatenate(
        [c, c_ctx[None, :], jnp.zeros((MOD_ROWS - bsz - 1, dm), F32)], axis=0)
    mod = _modulation(cmat, ada_w, ada_b)

    def mod_rows(layer, which, rows):
        return mod[layer, rows, which * dm:(which + 1) * dm][:, None, :]

    lat = slice(0, bsz)
    cx = slice(bsz, bsz + 1)

    def pair_major(g):
        g4 = g.reshape(g.shape[:-1] + (4, PAIRS, 2))
        g4 = jnp.stack([g4[..., t, :, :] for t in (1, 3, 0, 2)], axis=-2)
        return g4.reshape(g.shape)

    w_in = mlstm_w_in[0].T
    gate_b = pair_major(mlstm_gate_b[0].reshape(GATE_W)).reshape(GATE_W, 1)

    x2d = x.reshape(bsz * seq, dm)
    depth = mlp_w1.shape[0]
    w1_all = mlp_w1.reshape(depth * dm, D_FF)
    w2_all = mlp_w2.reshape(depth * D_FF, dm)
    q, kt, v, o, gates, w1_a, w2_a, w1_b, w2_b, w_out = _front(
        x2d, norm1_g[0], mod_rows(0, 0, lat), mod_rows(0, 1, lat), w_in, seq, True,
        ((w1_all, 0, depth), (w2_all, 0, depth), (w1_all, 1, depth), (w2_all, 1, depth),
         (mlstm_w_out.reshape(dm, dm), 0, 1)))
    ktc, vc, gates_c = _front(
        ctx.reshape(bsz * ctx_len, dm), norm1_g[0], mod_rows(0, 0, cx), mod_rows(0, 1, cx),
        w_in, bsz * ctx_len, False)
    gcol, grow = _gate_prep(gates, gate_b)
    _, growc = _gate_prep(gates_c, gate_b)

    h = _scan(q, kt, v, gcol, grow, ktc, vc, growc, bsz, seq, ctx_len)
    x2d = _mlstm_post(
        h, o, x2d, mlstm_norm_g[0], w_out, mod_rows(0, 2, lat),
        norm2_g[0], mod_rows(0, 3, lat), mod_rows(0, 4, lat), mod_rows(0, 5, lat),
        w1_a, w2_a, seq)

    out = _pool_layer(
        x2d, norm1_g[1], mod_rows(1, 0, lat), mod_rows(1, 1, lat), mod_rows(1, 2, lat),
        pool_w[0].astype(BF16), pool_scale[0], norm2_g[1], mod_rows(1, 3, lat),
        mod_rows(1, 4, lat), mod_rows(1, 5, lat), w1_b, w2_b, final_g, seq)
    return out.reshape(bsz, seq, dm)
```

```python
import functools

import jax
import jax.numpy as jnp
import numpy as np
from jax import lax
from jax.experimental import pallas as pl
from jax.experimental.pallas import tpu as pltpu

F32 = jnp.float32
BF16 = jnp.bfloat16

D_MODEL = 1024
HEADS = 8
DV = 128
DQK = 64
QK_W = HEADS * DQK
V_W = HEADS * DV
GATE_W = 4 * HEADS
D_FF = 4 * D_MODEL
NORM_EPS = 1e-6
GATE_SOFTCAP = 15.0
GRID_W = 64
POOL_WINDOWS = (2, 4, 8, 16)
POOL_GW = D_MODEL // len(POOL_WINDOWS)

LANES = 128
CHUNK = LANES
MOD_ROWS = 8
MOD_TILE = 1536
VMEM_LIMIT = 62 * 1024 * 1024
ROW_TILE = 1024
FF_TILE = 1024
NEG_BIG = -1e30
LOG2E = 1.4426950408889634


def _params(n_axes):
    return pltpu.CompilerParams(
        dimension_semantics=("arbitrary",) * n_axes, vmem_limit_bytes=VMEM_LIMIT)


def _resident(shape):
    return pl.BlockSpec(shape, lambda *_: (0,) * len(shape), pipeline_mode=pl.Buffered(1))


def _rider_specs(riders, steps):
    in_specs, out_specs, out_shapes = [], [], []
    for w, layer, depth in riders:
        rows = w.shape[0] // depth
        assert rows % steps == 0
        block = (rows // steps, w.shape[1])
        in_specs.append(pl.BlockSpec(block, lambda i, layer=layer: (layer * steps + i, 0)))
        out_specs.append(pl.BlockSpec(block, lambda i: (i, 0)))
        out_shapes.append(jax.ShapeDtypeStruct((rows, w.shape[1]), BF16))
    return in_specs, out_specs, out_shapes


def _rmsnorm(x, g):
    return x * lax.rsqrt(jnp.mean(x * x, axis=-1, keepdims=True) + NORM_EPS) * g


def _dot(a, b):
    return jnp.dot(a, b, preferred_element_type=F32)


def _split3(x):
    hi = x.astype(BF16)
    rest = x - hi.astype(F32)
    mid = rest.astype(BF16)
    return hi, mid, (rest - mid.astype(F32)).astype(BF16)


def _mod_kernel(c_ref, w_ref, b_ref, o_ref):
    c = c_ref[...]
    s = c * jax.nn.sigmoid(c)
    lhs = jnp.concatenate([p.astype(F32) for p in _split3(s)], axis=0).astype(BF16)
    acc = jnp.zeros((3 * MOD_ROWS, w_ref.shape[2]), F32)
    for piece in _split3(w_ref[0]):
        acc = acc + _dot(lhs, piece)
    o_ref[0] = acc[:MOD_ROWS] + acc[MOD_ROWS:2 * MOD_ROWS] + acc[2 * MOD_ROWS:] + b_ref[0]


def _modulation(cmat, ada_w, ada_b):
    depth, _, n = ada_w.shape
    tn = MOD_TILE
    return pl.pallas_call(
        _mod_kernel,
        grid=(depth, n // tn),
        in_specs=[
            pl.BlockSpec((MOD_ROWS, D_MODEL), lambda i, j: (0, 0)),
            pl.BlockSpec((1, D_MODEL, tn), lambda i, j: (i, 0, j)),
            pl.BlockSpec((1, 1, tn), lambda i, j: (i, 0, j)),
        ],
        out_specs=pl.BlockSpec((1, MOD_ROWS, tn), lambda i, j: (i, 0, j)),
        out_shape=jax.ShapeDtypeStruct((depth, MOD_ROWS, n), F32),
        compiler_params=_params(2),
        name="adaln_modulation",
    )(cmat, ada_w, ada_b.reshape(depth, 1, n))


V_START = 2 * QK_W
O_START = V_START + V_W
G_START = O_START + V_W


KG_ROW = QK_W + 2 * V_W


def _front_kernel(with_qo, n_riders, x_ref, g_ref, sh_ref, sc_ref, wt32_ref, *rest):
    riders_in, rest = rest[:n_riders], rest[n_riders:]
    *outs, w_ref = rest
    outs, riders_out = outs[:len(outs) - n_riders], outs[len(outs) - n_riders:]
    for src, dst in zip(riders_in, riders_out):
        dst[...] = src[...].astype(BF16)
    nt = (((1,), (1,)), ((), ()))

    @pl.when(pl.program_id(0) == 0)
    def _():
        w_ref[:QK_W, :] = wt32_ref[:QK_W, :].astype(BF16)
        w_ref[QK_W:KG_ROW, :] = wt32_ref[V_START:G_START, :].astype(BF16)
        w_ref[KG_ROW:KG_ROW + QK_W, :] = wt32_ref[QK_W:V_START, :].astype(BF16)
        new = lax.broadcasted_iota(jnp.int32, (GATE_W, GATE_W), 0)
        old = lax.broadcasted_iota(jnp.int32, (GATE_W, GATE_W), 1)
        kind = new % 8 // 2
        gate_type = jnp.where(kind < 2, 2 * kind + 1, 2 * (kind - 2))
        perm = (old == gate_type * HEADS + 2 * (new // 8) + new % 2).astype(BF16)
        w_ref[KG_ROW + QK_W:, :] = _dot(perm, wt32_ref[G_START:, :].astype(BF16)).astype(BF16)

    def normed(s):
        x = x_ref[s * SUB_ROWS:(s + 1) * SUB_ROWS, :]
        return (_rmsnorm(x, g_ref[...]) * (1.0 + sc_ref[0]) + sh_ref[0]).astype(BF16)

    if with_qo:
        q_ref, kt_ref, v_ref, o_ref, gr_ref = outs
    else:
        kt_ref, v_ref, gr_ref = outs

    n_sub = x_ref.shape[0] // SUB_ROWS
    xb = normed(0)
    for s in range(n_sub):
        rows = slice(s * SUB_ROWS, (s + 1) * SUB_ROWS)

        def token_major(w_rows, xb=xb):
            return lax.dot_general(xb, w_ref[w_rows, :], nt, preferred_element_type=F32)

        if with_qo:
            q_ref[rows, :] = (token_major(slice(0, QK_W)) * (DQK ** -0.5)).astype(BF16)
        xb_next = normed(s + 1) if s + 1 < n_sub else None
        if with_qo:
            o_ref[rows, :] = token_major(slice(QK_W + V_W, KG_ROW))
        kg = lax.dot_general(w_ref[KG_ROW:, :], xb, nt, preferred_element_type=F32)
        kt_ref[:, rows] = kg[:QK_W].astype(BF16)
        gr_ref[:, rows] = kg[QK_W:]
        v_ref[rows, :] = token_major(slice(QK_W, QK_W + V_W)).astype(BF16)
        xb = xb_next


def _front(x2d, norm_g, sh, sc, w_in_t, rows_per_mod, with_qo, riders=()):
    n = x2d.shape[0]
    tm = min(ROW_TILE, n)
    steps = n // tm
    tiles_per_mod = rows_per_mod // tm
    row = lambda i: (i, 0)
    mod = lambda i: (i // tiles_per_mod, 0, 0)
    rider_in, rider_out, rider_shapes = _rider_specs(riders, steps)
    out_shape, out_specs = [], []
    if with_qo:
        out_shape.append(jax.ShapeDtypeStruct((n, QK_W), BF16))
        out_specs.append(pl.BlockSpec((tm, QK_W), row))
    out_shape.append(jax.ShapeDtypeStruct((QK_W, n), BF16))
    out_specs.append(pl.BlockSpec((QK_W, tm), lambda i: (0, i)))
    out_shape.append(jax.ShapeDtypeStruct((n, V_W), BF16))
    out_specs.append(pl.BlockSpec((tm, V_W), row))
    if with_qo:
        out_shape.append(jax.ShapeDtypeStruct((n, V_W), F32))
        out_specs.append(pl.BlockSpec((tm, V_W), row))
    out_shape.append(jax.ShapeDtypeStruct((GATE_W, n), F32))
    out_specs.append(pl.BlockSpec((GATE_W, tm), lambda i: (0, i)))
    out_shape += rider_shapes
    out_specs += rider_out
    return pl.pallas_call(
        functools.partial(_front_kernel, with_qo, len(riders)),
        grid=(steps,),
        in_specs=[
            pl.BlockSpec((tm, D_MODEL), row),
            _resident((1, D_MODEL)),
            pl.BlockSpec((1, 1, D_MODEL), mod),
            pl.BlockSpec((1, 1, D_MODEL), mod),
            _resident(w_in_t.shape),
        ] + rider_in,
        out_specs=out_specs,
        out_shape=out_shape,
        scratch_shapes=[pltpu.VMEM(w_in_t.shape, BF16)],
        compiler_params=_params(1),
        name="mlstm_front_qo" if with_qo else "mlstm_front_ctx",
    )(x2d, norm_g.reshape(1, D_MODEL), sh, sc, w_in_t, *[r[0] for r in riders])


def _softcap(g):
    return GATE_SOFTCAP * jnp.tanh(g * (1.0 / GATE_SOFTCAP))


def _log_sigmoid(g):
    return jnp.minimum(g, 0.0) - jnp.log1p(jnp.exp(-jnp.abs(g)))


PAIRS = HEADS // 2
BCOL_W = 16
GCOL_W = PAIRS * BCOL_W
GATE_TILE = 2048


def _gate_kernel(gr_ref, br_ref, col_ref, row_ref):
    t = gr_ref.shape[1]
    r = lax.broadcasted_iota(jnp.int32, (CHUNK, CHUNK), 0)
    c = lax.broadcasted_iota(jnp.int32, (CHUNK, CHUNK), 1)
    lower = (c <= r).astype(BF16)
    upper = (c >= r).astype(BF16)
    ones = jnp.ones((CHUNK, CHUNK), BF16)
    row_rhs = jnp.concatenate([upper, lower, ones], axis=1)
    col_lhs = jnp.concatenate([lower, upper], axis=1)
    src = lax.broadcasted_iota(jnp.int32, (3 * GATE_W, GCOL_W), 0)
    dst = lax.broadcasted_iota(jnp.int32, (3 * GATE_W, GCOL_W), 1)
    gate = src % GATE_W
    place = ((gate % 8 < 4)
             & (dst == (gate // 8) * BCOL_W + 4 * (src // GATE_W) + gate % 8)).astype(BF16)
    lane_piece = lax.broadcasted_iota(jnp.int32, (CHUNK, 3 * GATE_W), 1) // GATE_W
    m = lax.broadcasted_iota(jnp.int32, (GATE_W, CHUNK), 0) % 8
    fwd_rows = m < 2
    bwd_rows = (m >= 2) & (m < 4)
    nt = (((1,), (1,)), ((), ()))

    gr = _softcap(gr_ref[...] + br_ref[...])
    li = pltpu.roll(gr, GATE_W - 4, axis=0)
    lf3 = _split3(_log_sigmoid(gr))

    chunks = [slice(k * CHUNK, (k + 1) * CHUNK) for k in range(t // CHUNK)]
    sums = []
    for sl in chunks:
        cum = jnp.zeros((CHUNK, 3 * GATE_W), F32)
        y = jnp.zeros((GATE_W, 3 * CHUNK), F32)
        for piece in lf3:
            x = piece[:, sl]
            zero = jnp.zeros_like(x)
            xd = jnp.concatenate([jnp.where(fwd_rows, x, zero), jnp.where(bwd_rows, x, zero)],
                                 axis=1)
            cum = cum + lax.dot_general(col_lhs, jnp.concatenate([xd, xd, xd], axis=0), nt,
                                        preferred_element_type=F32)
            y = y + _dot(x, row_rhs)
        sums.append((cum, y))
    for sl, (cum, y) in zip(chunks, sums):
        hi, mid, lo = [p.astype(F32) for p in _split3(cum * LOG2E)]
        pieces = jnp.where(lane_piece == 0, hi, jnp.where(lane_piece == 1, mid, lo))
        col_ref[sl, :] = _dot(pieces.astype(BF16), place).astype(BF16)
        total = y[:, 2 * CHUNK:]
        e = li[:, sl] - jnp.where(fwd_rows, y[:, :CHUNK], y[:, CHUNK:2 * CHUNK])
        used = fwd_rows | bwd_rows
        row_ref[0, :, sl] = jnp.where(used, e * LOG2E, 0.0)
        row_ref[1, :, sl] = jnp.where(used, jnp.exp(total + e), 0.0)
        row_ref[2, :, sl] = jnp.where(used, jnp.exp(total), 0.0)


def _gate_prep(gates_raw, bias):
    n = gates_raw.shape[1]
    t = min(GATE_TILE, n)
    return pl.pallas_call(
        _gate_kernel,
        grid=(n // t,),
        in_specs=[
            pl.BlockSpec((GATE_W, t), lambda i: (0, i)),
            _resident((GATE_W, 1)),
        ],
        out_specs=[
            pl.BlockSpec((t, GCOL_W), lambda i: (i, 0)),
            pl.BlockSpec((3, GATE_W, t), lambda i: (0, 0, i)),
        ],
        out_shape=[
            jax.ShapeDtypeStruct((n, GCOL_W), BF16),
            jax.ShapeDtypeStruct((3, GATE_W, n), F32),
        ],
        compiler_params=_params(1),
        name="mlstm_gate_prep",
    )(gates_raw, bias)


SCAN_SPLITS = 2
STATE_UNROLL = 8
OUTPUT_UNROLL = 16


def _scan_kernel(q_ref, kt_ref, v_ref, gcol_ref, grow_ref, ktc_ref, vc_ref, growc_ref,
                 h_ref, cst_ref, cs_ref):
    seq = q_ref.shape[0]
    nc = seq // CHUNK
    ncc = ktc_ref.shape[1] // CHUNK
    nc_part = nc // SCAN_SPLITS
    part = pl.program_id(2)
    r = lax.broadcasted_iota(jnp.int32, (CHUNK, CHUNK), 0)
    c = lax.broadcasted_iota(jnp.int32, (CHUNK, CHUNK), 1)
    masks = (r >= c, r <= c)
    left = c < DQK
    ones_blk = jnp.ones((CHUNK, DV), BF16)
    zeros_k = jnp.zeros((DQK, CHUNK), BF16)
    zeros_q = jnp.zeros((CHUNK, DQK), BF16)
    pair = pl.program_id(1)
    src = lax.broadcasted_iota(jnp.int32, (GCOL_W, 4 * CHUNK), 0)
    block = lax.broadcasted_iota(jnp.int32, (GCOL_W, 4 * CHUNK), 1) // CHUNK
    piece = src % BCOL_W
    sel = ((src // BCOL_W == pair) & (piece % 4 == block) & (piece < 12)).astype(BF16)

    def gate_row(ref, quantity, d, j, lanes):
        return ref[quantity, 2 * d + j:2 * d + j + 1, lanes]

    def state_increment(kt_j, v_j, w_row):
        ktw = (kt_j.astype(F32) * w_row).astype(BF16)
        return _dot(ktw, jnp.concatenate([v_j, ones_blk], axis=1))

    def decay2(a_row):
        return jnp.concatenate([a_row, a_row], axis=1)

    @pl.when(part == 0)
    def _():
        cst_ref[...] = jnp.zeros(cst_ref.shape, F32)
        for d in range(2):
            for step in range(ncc):
                k = step if d == 0 else ncc - 1 - step
                sl = slice(k * CHUNK, (k + 1) * CHUNK)
                ktp = ktc_ref[:, sl]
                vp = vc_ref[sl, :]
                for j in range(2):
                    inc = state_increment(ktp[j * DQK:(j + 1) * DQK], vp[:, j * DV:(j + 1) * DV],
                                          gate_row(growc_ref, 1, d, j, sl))
                    cst_ref[d, j] = cst_ref[d, j] * decay2(gate_row(growc_ref, 2, d, j, sl)) + inc

        def state_step(i, carry):
            for d in range(2):
                k = i if d == 0 else nc - 1 - i
                rows = pl.ds(pl.multiple_of(k * CHUNK, CHUNK), CHUNK)
                ktp = kt_ref[:, rows]
                vp = v_ref[rows, :]
                for j in range(2):
                    cst = cst_ref[d, j]
                    cs_ref[d, k, j * DQK:(j + 1) * DQK, :] = cst.astype(BF16)
                    inc = state_increment(ktp[j * DQK:(j + 1) * DQK], vp[:, j * DV:(j + 1) * DV],
                                          gate_row(grow_ref, 1, d, j, rows))
                    cst_ref[d, j] = cst * decay2(gate_row(grow_ref, 2, d, j, rows)) + inc
            return carry

        lax.fori_loop(0, nc, state_step, 0, unroll=STATE_UNROLL)

    def decay_stage(k):
        rows = pl.ds(pl.multiple_of(k * CHUNK, CHUNK), CHUNK)
        qp = q_ref[rows, :]
        ktp = kt_ref[:, rows]
        bb_all = _dot(gcol_ref[rows, :], sel)
        kbd = jnp.concatenate([jnp.concatenate([ktp[:DQK], zeros_k], axis=1),
                               jnp.concatenate([zeros_k, ktp[DQK:]], axis=1)], axis=0)
        s2 = _dot(qp, kbd)
        qf = qp.astype(F32)
        lhs = []
        for d in range(2):
            bb = [bb_all[:, (2 * d + j) * CHUNK:(2 * d + j + 1) * CHUNK] for j in range(2)]
            qs = (qf * jnp.exp2(jnp.where(left, bb[0], bb[1]))).astype(BF16)
            for j in range(2):
                e_row = gate_row(grow_ref, 0, d, j, rows)
                dm = jnp.exp2(jnp.where(masks[d], bb[j] + e_row, NEG_BIG))
                p = (s2[:, j * CHUNK:(j + 1) * CHUNK] * dm).astype(BF16)
                qs_j = [qs[:, :DQK], zeros_q] if j == 0 else [zeros_q, qs[:, DQK:]]
                lhs.append(jnp.concatenate([p] + qs_j, axis=1))
        return lhs

    def value_stage(k, i, lhs):
        rows = pl.ds(pl.multiple_of(k * CHUNK, CHUNK), CHUNK)
        out_rows = pl.ds(pl.multiple_of(i * CHUNK, CHUNK), CHUNK)
        vp = v_ref[rows, :]
        for j in range(2):
            v1 = jnp.concatenate([vp[:, j * DV:(j + 1) * DV], ones_blk], axis=1)
            h = None
            for d in range(2):
                out = _dot(lhs[2 * d + j], jnp.concatenate([v1, cs_ref[d, k]], axis=0))
                hd = out[:, :DV] / jnp.maximum(jnp.abs(out[:, DV:]), 1.0)
                h = hd if d == 0 else h + hd
            h_ref[out_rows, j * DV:(j + 1) * DV] = h

    def output_trip(t, carry):
        i0 = t * OUTPUT_UNROLL
        staged = [decay_stage(part * nc_part + i0 + g) for g in range(OUTPUT_UNROLL)]
        for g in range(OUTPUT_UNROLL):
            value_stage(part * nc_part + i0 + g, i0 + g, staged[g])
        return carry

    lax.fori_loop(0, nc_part // OUTPUT_UNROLL, output_trip, 0)


def _scan(q, kt, v, gcol, grow, ktc, vc, growc, bsz, seq, ctx_len):
    nc = seq // CHUNK
    return pl.pallas_call(
        _scan_kernel,
        grid=(bsz, PAIRS, SCAN_SPLITS),
        in_specs=[
            pl.BlockSpec((seq, 2 * DQK), lambda b, p, s: (b, p)),
            pl.BlockSpec((2 * DQK, seq), lambda b, p, s: (p, b)),
            pl.BlockSpec((seq, 2 * DV), lambda b, p, s: (b, p)),
            pl.BlockSpec((seq, GCOL_W), lambda b, p, s: (b, 0)),
            pl.BlockSpec((3, 8, seq), lambda b, p, s: (0, p, b)),
            pl.BlockSpec((2 * DQK, ctx_len), lambda b, p, s: (p, b)),
            pl.BlockSpec((ctx_len, 2 * DV), lambda b, p, s: (b, p)),
            pl.BlockSpec((3, 8, ctx_len), lambda b, p, s: (0, p, b)),
        ],
        out_specs=pl.BlockSpec((seq // SCAN_SPLITS, 2 * DV),
                               lambda b, p, s: (b * SCAN_SPLITS + s, p)),
        out_shape=jax.ShapeDtypeStruct((bsz * seq, V_W), F32),
        scratch_shapes=[pltpu.VMEM((2, 2, DQK, 2 * DV), F32),
                        pltpu.VMEM((2, nc, 2 * DQK, 2 * DV), BF16)],
        compiler_params=_params(3),
        name="mlstm_scan",
    )(q, kt, v, gcol, grow, ktc, vc, growc)


MLP_TILE = 1024
POST_TILE = 1024
SUB_ROWS = 256


def _mlp_pipeline(n_sub, mixer, finish, n2_ref, sh2_ref, sc2_ref, g2_ref, w1_ref, w2_ref):
    def normed(x1):
        return (_rmsnorm(x1, n2_ref[...]) * (1.0 + sc2_ref[0]) + sh2_ref[0]).astype(BF16)

    def slab(u, k):
        cols = slice(k * FF_TILE, (k + 1) * FF_TILE)
        hid = jnp.maximum(_dot(u, w1_ref[:, cols]), 0.0)
        return _dot((hid * hid).astype(BF16), w2_ref[cols, :])

    def advance(gen):
        try:
            next(gen)
            return gen, None
        except StopIteration as done:
            return None, done.value

    gen, x1 = mixer(0), None
    while gen is not None:
        gen, x1 = advance(gen)
    u = normed(x1)
    for s in range(n_sub):
        gen = mixer(s + 1) if s + 1 < n_sub else None
        x1_next = u_next = None
        acc = slab(u, 0)
        for k in range(1, D_FF // FF_TILE):
            if gen is not None:
                gen, x1_next = advance(gen)
                if gen is None:
                    u_next = normed(x1_next)
            acc = acc + slab(u, k)
        assert gen is None, "mixer has more phases than hidden slabs to hide them under"
        finish(s, x1 + g2_ref[0] * acc)
        x1, u = x1_next, u_next


def _mlstm_post_kernel(h_ref, o_ref, x_ref, ng_ref, wout_ref, g1_ref, n2_ref, sh2_ref, sc2_ref,
                       g2_ref, w1_ref, w2_ref, out_ref):
    def mixer(s):
        rows = slice(s * SUB_ROWS, (s + 1) * SUB_ROWS)
        gate = jax.nn.sigmoid(o_ref[rows, :])
        parts = []
        for k in range(HEADS):
            hk = h_ref[rows, k * DV:(k + 1) * DV]
            parts.append(hk * lax.rsqrt(jnp.mean(hk * hk, axis=-1, keepdims=True) + NORM_EPS))
        hn = jnp.concatenate(parts, axis=1) * ng_ref[...]
        y = (gate * hn).astype(BF16)
        return x_ref[rows, :] + g1_ref[0] * _dot(y, wout_ref[...])
        yield

    def finish(s, x2):
        out_ref[s * SUB_ROWS:(s + 1) * SUB_ROWS, :] = x2

    _mlp_pipeline(x_ref.shape[0] // SUB_ROWS, mixer, finish, n2_ref, sh2_ref, sc2_ref, g2_ref,
                  w1_ref, w2_ref)


def _mlstm_post(h, o, x2d, norm_g, w_out, g1, n2, sh2, sc2, g2, w1, w2, rows_per_mod):
    n = x2d.shape[0]
    tm = POST_TILE
    tiles_per_mod = rows_per_mod // tm
    row = pl.BlockSpec((tm, D_MODEL), lambda i: (i, 0))
    mod = pl.BlockSpec((1, 1, D_MODEL), lambda i: (i // tiles_per_mod, 0, 0))
    return pl.pallas_call(
        _mlstm_post_kernel,
        grid=(n // tm,),
        in_specs=[row, row, row, _resident((1, V_W)), _resident(w_out.shape), mod,
                  _resident((1, D_MODEL)), mod, mod, mod, _resident(w1.shape), _resident(w2.shape)],
        out_specs=row,
        out_shape=jax.ShapeDtypeStruct((n, D_MODEL), F32),
        compiler_params=_params(1),
        name="mlstm_post_mlp",
    )(h, o, x2d, norm_g.reshape(1, V_W), w_out, g1, n2.reshape(1, D_MODEL), sh2, sc2, g2, w1, w2)


def _pool_constants():
    t = np.arange(SUB_ROWS)
    pos = t % GRID_W
    bands, inv = [], []
    for window in POOL_WINDOWS:
        lo = np.clip(pos - window // 2, 0, GRID_W)
        hi = np.clip(pos - window // 2 + window, 0, GRID_W)
        same_group = t[:, None] // GRID_W == t[None, :] // GRID_W
        bands.append(same_group & (pos[None, :] >= lo[:, None]) & (pos[None, :] < hi[:, None]))
        inv.append(np.broadcast_to((1.0 / (hi - lo))[:, None], (SUB_ROWS, POOL_GW)))
    return (jnp.asarray(np.stack(bands), dtype=BF16), jnp.asarray(np.stack(inv), dtype=F32))


def _pool_layer_kernel(x_ref, n1_ref, sh1_ref, sc1_ref, g1_ref, band_ref, inv_ref, pw_ref, ps_ref,
                       n2_ref, sh2_ref, sc2_ref, g2_ref, w1_ref, w2_ref, fg_ref, out_ref):
    def mixer(s):
        x = x_ref[s * SUB_ROWS:(s + 1) * SUB_ROWS, :]
        xn = _rmsnorm(x, n1_ref[...]) * (1.0 + sc1_ref[0]) + sh1_ref[0]
        hi = xn.astype(BF16)
        lo = (xn - hi.astype(F32)).astype(BF16)
        ps = []
        for gi in range(len(POOL_WINDOWS)):
            cols = slice(gi * POOL_GW, (gi + 1) * POOL_GW)
            band = band_ref[gi]
            total = _dot(band, hi[:, cols]) + _dot(band, lo[:, cols])
            ps.append((total * inv_ref[gi] - xn[:, cols]).astype(BF16))
        yield
        ys = [_dot(p, pw_ref[gi]) for gi, p in enumerate(ps)]
        return x + g1_ref[0] * (jnp.concatenate(ys, axis=1) * ps_ref[...])

    def finish(s, x2):
        out_ref[s * SUB_ROWS:(s + 1) * SUB_ROWS, :] = _rmsnorm(x2, fg_ref[...])

    _mlp_pipeline(x_ref.shape[0] // SUB_ROWS, mixer, finish, n2_ref, sh2_ref, sc2_ref, g2_ref,
                  w1_ref, w2_ref)


def _pool_layer(x2d, n1, sh1, sc1, g1, pool_w, pool_scale, n2, sh2, sc2, g2, w1, w2,
                final_g, rows_per_mod):
    n = x2d.shape[0]
    tm = MLP_TILE
    tiles_per_mod = rows_per_mod // tm
    row = pl.BlockSpec((tm, D_MODEL), lambda i: (i, 0))
    mod = pl.BlockSpec((1, 1, D_MODEL), lambda i: (i // tiles_per_mod, 0, 0))
    vec = _resident((1, D_MODEL))
    bands, inv = _pool_constants()
    return pl.pallas_call(
        _pool_layer_kernel,
        grid=(n // tm,),
        in_specs=[row, vec, mod, mod, mod, _resident(bands.shape), _resident(inv.shape),
                  _resident(pool_w.shape), vec, vec, mod, mod, mod,
                  _resident(w1.shape), _resident(w2.shape), vec],
        out_specs=row,
        out_shape=jax.ShapeDtypeStruct((n, D_MODEL), F32),
        compiler_params=_params(1),
        name="pool_mlp_final",
    )(x2d, n1.reshape(1, D_MODEL), sh1, sc1, g1, bands, inv, pool_w,
      pool_scale.reshape(1, D_MODEL),
      n2.reshape(1, D_MODEL), sh2, sc2, g2, w1, w2, final_g.reshape(1, D_MODEL))


def kernel(x, c, ctx, c_ctx, ada_w, ada_b, norm1_g, norm2_g, mlstm_w_in, mlstm_gate_b,
           mlstm_norm_g, mlstm_w_out, pool_w, pool_scale, mlp_w1, mlp_w2, final_g):
    bsz, seq, dm = x.shape
    ctx_len = ctx.shape[1]
    assert dm == D_MODEL and seq % (2 * CHUNK) == 0 and ctx_len % CHUNK == 0
    assert seq % ROW_TILE == 0 and seq % MLP_TILE == 0 and seq % POST_TILE == 0
    assert (bsz * ctx_len) % SUB_ROWS == 0 and ROW_TILE % SUB_ROWS == 0 and bsz + 1 <= MOD_ROWS
    assert MLP_TILE % SUB_ROWS == 0 and SUB_ROWS % GRID_W == 0
    assert ada_w.shape[0] == 2 and mlstm_w_in.shape[0] == 1 and pool_w.shape[0] == 1

    cmat = jnp.concatenate(
        [c, c_ctx[None, :], jnp.zeros((MOD_ROWS - bsz - 1, dm), F32)], axis=0)
    mod = _modulation(cmat, ada_w, ada_b)

    def mod_rows(layer, which, rows):
        return mod[layer, rows, which * dm:(which + 1) * dm][:, None, :]

    lat = slice(0, bsz)
    cx = slice(bsz, bsz + 1)

    def pair_major(g):
        g4 = g.reshape(g.shape[:-1] + (4, PAIRS, 2))
        g4 = jnp.stack([g4[..., t, :, :] for t in (1, 3, 0, 2)], axis=-2)
        return g4.reshape(g.shape)

    w_in = mlstm_w_in[0].T
    gate_b = pair_major(mlstm_gate_b[0].reshape(GATE_W)).reshape(GATE_W, 1)

    x2d = x.reshape(bsz * seq, dm)
    depth = mlp_w1.shape[0]
    w1_all = mlp_w1.reshape(depth * dm, D_FF)
    w2_all = mlp_w2.reshape(depth * D_FF, dm)
    q, kt, v, o, gates, w1_a, w2_a, w1_b, w2_b, w_out = _front(
        x2d, norm1_g[0], mod_rows(0, 0, lat), mod_rows(0, 1, lat), w_in, seq, True,
        ((w1_all, 0, depth), (w2_all, 0, depth), (w1_all, 1, depth), (w2_all, 1, depth),
         (mlstm_w_out.reshape(dm, dm), 0, 1)))
    ktc, vc, gates_c = _front(
        ctx.reshape(bsz * ctx_len, dm), norm1_g[0], mod_rows(0, 0, cx), mod_rows(0, 1, cx),
        w_in, bsz * ctx_len, False)
    gcol, grow = _gate_prep(gates, gate_b)
    _, growc = _gate_prep(gates_c, gate_b)

    h = _scan(q, kt, v, gcol, grow, ktc, vc, growc, bsz, seq, ctx_len)
    x2d = _mlstm_post(
        h, o, x2d, mlstm_norm_g[0], w_out, mod_rows(0, 2, lat),
        norm2_g[0], mod_rows(0, 3, lat), mod_rows(0, 4, lat), mod_rows(0, 5, lat),
        w1_a, w2_a, seq)

    out = _pool_layer(
        x2d, norm1_g[1], mod_rows(1, 0, lat), mod_rows(1, 1, lat), mod_rows(1, 2, lat),
        pool_w[0].astype(BF16), pool_scale[0], norm2_g[1], mod_rows(1, 3, lat),
        mod_rows(1, 4, lat), mod_rows(1, 5, lat), w1_b, w2_b, final_g, seq)
    return out.reshape(bsz, seq, dm)
```
